```python
import math
import jax
import jax.numpy as jnp
from jax import lax
import numpy as np

D_MODEL = 1024
BATCH = 4
SEQ = 8192
DEPTH = 1
DEC_BATCH = 32
DEC_SEQ = 4
PAST_LEN = 16384
PAGE_SIZE = 128

HEAD_DIM = 64
D_ATT = D_MODEL // 2
N_HEADS = D_ATT // HEAD_DIM
N_KV_HEADS = 2
IDX_HEADS = 4
IDX_DIM = 64
TOPK_MAX = 256
Q_BLOCK = 128
N_BUCKETS = 32
MAX_DISTANCE = 128
D_SSD = D_MODEL - D_ATT
SSD_HEAD_DIM = 64
SSD_HEADS = D_SSD // SSD_HEAD_DIM
SSD_GROUPS = 2
D_STATE = 64
CONV_W = 4
CONV_DIM = D_SSD + 2 * SSD_GROUPS * D_STATE
SSD_CHUNK = 128
N_EXPERT_GROUPS = 4
EXPERTS_PER_GROUP = 8
N_EXPERTS = N_EXPERT_GROUPS * EXPERTS_PER_GROUP
TOP_K_INNER = 2
D_EXPERT = D_MODEL // 2
ROW_BLOCK = 128
EPS = 1e-6
SPLIT_SIZES = (D_ATT, N_KV_HEADS * HEAD_DIM, N_KV_HEADS * HEAD_DIM, IDX_HEADS * IDX_DIM, IDX_DIM, IDX_HEADS, D_SSD, CONV_DIM, SSD_HEADS)
IN_DIM = D_ATT + 4 * N_KV_HEADS * HEAD_DIM // 2 + IDX_HEADS * IDX_DIM + IDX_DIM + IDX_HEADS + D_SSD + CONV_DIM + SSD_HEADS

kernel_name = 'hymba_dsa_ssd_hier_moe_step'


def rmsnorm(x, w):
    xf = x.astype(jnp.float32)
    y = xf * lax.rsqrt(jnp.mean(xf * xf, axis=-1, keepdims=True) + EPS)
    return (y * w.astype(jnp.float32)).astype(x.dtype)


def rel_bucket(dist):
    n = jnp.maximum(dist, 0)
    max_exact = N_BUCKETS // 2
    nf = jnp.maximum(n, 1).astype(jnp.float32)
    log_part = jnp.log(nf / max_exact) / math.log(MAX_DISTANCE / max_exact) * (N_BUCKETS - max_exact)
    large = jnp.minimum(max_exact + log_part.astype(jnp.int32), N_BUCKETS - 1)
    return jnp.where(n < max_exact, n, large)


def project(xn, w_in):
    points = np.cumsum(SPLIT_SIZES)[:-1].tolist()
    return jnp.split(xn @ w_in, points, axis=-1)


def indexer_topk(q_idx, w_idx, k_idx, q_pos, topk):
    s = jnp.einsum('bthd,bsd->bths', q_idx, k_idx).astype(jnp.float32) * (IDX_DIM ** -0.5)
    score = jnp.einsum('bths,bth->bts', jax.nn.relu(s), w_idx.astype(jnp.float32)) * (IDX_HEADS ** -0.5)
    key_pos = jnp.arange(k_idx.shape[1], dtype=jnp.int32)
    admissible = key_pos[None, :] <= q_pos[:, None]
    score = jnp.where(admissible[None], score, -jnp.inf)
    _, idx = lax.top_k(score, topk)
    valid = idx <= q_pos[None, :, None]
    return idx, valid


def sparse_attend(q, k_sel, v_sel, dist, valid, rel_bias):
    b, t = q.shape[0], q.shape[1]
    rep = N_HEADS // N_KV_HEADS
    kk = k_sel.shape[2]
    qg = q.reshape(b, t, N_KV_HEADS, rep, HEAD_DIM)
    logits = jnp.einsum('btgrd,btkgd->btgrk', qg, k_sel).astype(jnp.float32) * (HEAD_DIM ** -0.5)
    bias = rel_bias.astype(jnp.float32)[rel_bucket(dist)]
    bias = jnp.moveaxis(bias, 2, -1).reshape(b, t, N_KV_HEADS, rep, kk)
    logits = jnp.where(valid[:, :, None, None, :], logits + bias, -jnp.inf)
    p = jax.nn.softmax(logits, axis=-1).astype(v_sel.dtype)
    o = jnp.einsum('btgrk,btkgd->btgrd', p, v_sel)
    return o.reshape(b, t, N_HEADS * HEAD_DIM)


def gather_rows(a, idx):
    return jax.vmap(lambda ab, ib: ab[ib])(a, idx)


def attn_prompt(q, k, v, q_idx, k_idx, w_idx, rel_bias):
    b, T = q.shape[0], q.shape[1]
    topk = min(TOPK_MAX, T // 4)
    qb = min(Q_BLOCK, T)
    nblk = T // qb

    def one_block(i):
        start = i * qb
        sl = lambda a: lax.dynamic_slice_in_dim(a, start, qb, axis=1)
        q_pos = start + jnp.arange(qb, dtype=jnp.int32)
        idx, valid = indexer_topk(sl(q_idx), sl(w_idx), k_idx, q_pos, topk)
        dist = q_pos[None, :, None] - idx
        return sparse_attend(sl(q), gather_rows(k, idx), gather_rows(v, idx), dist, valid, rel_bias)

    out = lax.map(one_block, jnp.arange(nblk, dtype=jnp.int32))
    return jnp.moveaxis(out, 0, 1).reshape(b, T, D_ATT)


def attn_sample(q, k_new, v_new, q_idx, k_idx_new, w_idx, pool_k, pool_v, pool_kidx, page_table, rel_bias):
    b, T = q.shape[0], q.shape[1]
    past = page_table.shape[1] * PAGE_SIZE
    topk = min(TOPK_MAX, (past + T) // 4)
    kidx_past = pool_kidx[page_table].reshape(b, past, IDX_DIM)
    kidx_all = jnp.concatenate([kidx_past, k_idx_new.astype(kidx_past.dtype)], axis=1)
    q_pos = past + jnp.arange(T, dtype=jnp.int32)
    idx, valid = indexer_topk(q_idx, w_idx, kidx_all, q_pos, topk)
    in_past = idx < past
    pidx = jnp.minimum(idx, past - 1)
    phys = jax.vmap(lambda pt, i: pt[i])(page_table, pidx // PAGE_SIZE)
    off = pidx % PAGE_SIZE
    nidx = jnp.clip(idx - past, 0, T - 1)
    sel = in_past[..., None, None]
    k_sel = jnp.where(sel, pool_k[phys, off].astype(k_new.dtype), gather_rows(k_new, nidx))
    v_sel = jnp.where(sel, pool_v[phys, off].astype(v_new.dtype), gather_rows(v_new, nidx))
    dist = q_pos[None, :, None] - idx
    return sparse_attend(q, k_sel, v_sel, dist, valid, rel_bias)


def causal_conv(xbc, conv_state, w, bias):
    T = xbc.shape[1]
    xp = jnp.concatenate([conv_state.astype(xbc.dtype), xbc], axis=1)
    y = bias
    for j in range(CONV_W):
        y = y + xp[:, j:j + T] * w[j]
    return jax.nn.silu(y), xp[:, T:]


def ssd_scan(x, dt, A, Bm, Cm, h0):
    b, T = x.shape[0], x.shape[1]
    cl = SSD_CHUNK if T % SSD_CHUNK == 0 else T
    nc = T // cl
    rep = SSD_HEADS // SSD_GROUPS

    def chunk(a):
        return a.reshape((b, nc, cl) + a.shape[2:])

    xc = chunk(x.astype(jnp.float32))
    dtc = chunk(dt)
    Bc = chunk(jnp.repeat(Bm.astype(jnp.float32), rep, axis=2))
    Cc = chunk(jnp.repeat(Cm.astype(jnp.float32), rep, axis=2))
    acs = jnp.cumsum(dtc * A, axis=2)
    causal = jnp.tril(jnp.ones((cl, cl), dtype=bool))
    seg = acs[:, :, :, None, :] - acs[:, :, None, :, :]
    decay = jnp.exp(jnp.where(causal[None, None, :, :, None], seg, -jnp.inf))
    scores = jnp.einsum('bcihn,bcjhn->bcijh', Cc, Bc) * decay * dtc[:, :, None, :, :]
    y_diag = jnp.einsum('bcijh,bcjhp->bcihp', scores, xc)
    w_end = jnp.exp(acs[:, :, -1:, :] - acs) * dtc
    states = jnp.einsum('bclhn,bclhp->bchpn', Bc * w_end[..., None], xc)
    chunk_decay = jnp.exp(acs[:, :, -1, :])

    def step(h, inp):
        s_c, d_c = inp
        return h * d_c[:, :, None, None] + s_c, h

    h_last, h_in = lax.scan(step, h0, (jnp.moveaxis(states, 1, 0), jnp.moveaxis(chunk_decay, 1, 0)))
    h_in = jnp.moveaxis(h_in, 0, 1)
    y_off = jnp.einsum('bclhn,bchpn->bclhp', Cc, h_in) * jnp.exp(acs)[..., None]
    return (y_diag + y_off).reshape(b, T, SSD_HEADS, SSD_HEAD_DIM), h_last


def ssd_mix(z, xbc, dt_raw, conv_state, ssm_state, conv_w, conv_b, dt_bias, a_log, d_skip, norm_w):
    f32 = jnp.float32
    b, T = z.shape[0], z.shape[1]
    xbc_act, new_conv = causal_conv(xbc, conv_state, conv_w, conv_b)
    nb = SSD_GROUPS * D_STATE
    xs = xbc_act[..., :D_SSD].reshape(b, T, SSD_HEADS, SSD_HEAD_DIM)
    Bm = xbc_act[..., D_SSD:D_SSD + nb].reshape(b, T, SSD_GROUPS, D_STATE)
    Cm = xbc_act[..., D_SSD + nb:].reshape(b, T, SSD_GROUPS, D_STATE)
    dt = jax.nn.softplus(dt_raw.astype(f32) + dt_bias.astype(f32))
    A = -jnp.exp(a_log.astype(f32))
    y, h_last = ssd_scan(xs, dt, A, Bm, Cm, ssm_state.astype(f32))
    y = y + d_skip.astype(f32)[:, None] * xs.astype(f32)
    y = y.reshape(b, T, D_SSD) * jax.nn.silu(z.astype(f32))
    yg = y.reshape(b, T, SSD_GROUPS, D_SSD // SSD_GROUPS)
    yg = yg * lax.rsqrt(jnp.mean(yg * yg, axis=-1, keepdims=True) + EPS)
    y = yg.reshape(b, T, D_SSD) * norm_w.astype(f32)
    return y.astype(z.dtype), new_conv, h_last.astype(z.dtype)


def grouped_experts(xf, expert, gate, w_gate, w_up, w_down):
    M, D = xf.shape
    A = expert.shape[0] * expert.shape[1]
    e_flat = expert.reshape(-1).astype(jnp.int32)
    tok = jnp.repeat(jnp.arange(M, dtype=jnp.int32), TOP_K_INNER)
    gw = gate.reshape(-1)
    order = jnp.argsort(e_flat)
    e_s, tok_s, gw_s = e_flat[order], tok[order], gw[order]
    sizes = jnp.bincount(e_flat, length=N_EXPERTS).astype(jnp.int32)
    starts = jnp.cumsum(sizes) - sizes
    padded = (sizes + ROW_BLOCK - 1) // ROW_BLOCK * ROW_BLOCK
    pends = jnp.cumsum(padded)
    pstarts = pends - padded
    dest = pstarts[e_s] + jnp.arange(A, dtype=jnp.int32) - starts[e_s]
    cap = -(-(A + N_EXPERTS * (ROW_BLOCK - 1)) // ROW_BLOCK) * ROW_BLOCK
    n_blk = cap // ROW_BLOCK
    row_tok = jnp.zeros((cap,), jnp.int32).at[dest].set(tok_s)
    row_w = jnp.zeros((cap,), jnp.float32).at[dest].set(gw_s)
    blk_e = jnp.minimum(jnp.searchsorted(pends, jnp.arange(n_blk, dtype=jnp.int32) * ROW_BLOCK, side='right'), N_EXPERTS - 1)
    xr = xf[row_tok].reshape(n_blk, ROW_BLOCK, D)

    def run(args):
        xb, e = args
        hdn = jax.nn.silu(xb @ w_gate[e]) * (xb @ w_up[e])
        return hdn @ w_down[e]

    yr = lax.map(run, (xr, blk_e)).reshape(cap, D)
    contrib = (yr.astype(jnp.float32) * row_w[:, None]).astype(xf.dtype)
    return jnp.zeros_like(xf).at[row_tok].add(contrib)


def hier_moe(x, w_rg, b_rg, w_re, b_re, w_gate, w_up, w_down):
    shp = x.shape
    xf = x.reshape(-1, shp[-1])
    M = xf.shape[0]
    g_logits = (xf @ w_rg).astype(jnp.float32) + b_rg.astype(jnp.float32)
    g_prob = jax.nn.softmax(g_logits, axis=-1)
    grp = jnp.argmax(g_logits, axis=-1).astype(jnp.int32)
    g_w = jnp.take_along_axis(g_prob, grp[:, None], axis=1)
    e_logits = ((xf @ w_re).astype(jnp.float32) + b_re.astype(jnp.float32)).reshape(M, N_EXPERT_GROUPS, EXPERTS_PER_GROUP)
    e_in = jnp.take_along_axis(e_logits, grp[:, None, None], axis=1)[:, 0]
    top_v, top_i = lax.top_k(e_in, TOP_K_INNER)
    gate = g_w * jax.nn.softmax(top_v, axis=-1)
    expert = grp[:, None] * EXPERTS_PER_GROUP + top_i
    return grouped_experts(xf, expert, gate, w_gate, w_up, w_down).reshape(shp)


def trunk_layer(h, attn_fn, conv_state, ssm_state, norm1_w, w_in, conv_w, conv_b, dt_bias, a_log, d_skip,
                ssd_norm_w, w_out, norm2_w, w_rg, b_rg, w_re, b_re, w_gate, w_up, w_down):
    b, T = h.shape[0], h.shape[1]
    xn = rmsnorm(h, norm1_w)
    q, k, v, q_idx, k_idx, w_idx, z, xbc, dt_raw = project(xn, w_in)
    q = q.reshape(b, T, N_HEADS, HEAD_DIM)
    k = k.reshape(b, T, N_KV_HEADS, HEAD_DIM)
    v = v.reshape(b, T, N_KV_HEADS, HEAD_DIM)
    q_idx = q_idx.reshape(b, T, IDX_HEADS, IDX_DIM)
    attn_out = attn_fn(q, k, v, q_idx, k_idx, w_idx)
    ssd_out, new_conv, new_ssm = ssd_mix(z, xbc, dt_raw, conv_state, ssm_state, conv_w, conv_b,
                                         dt_bias, a_log, d_skip, ssd_norm_w)
    h = h + jnp.concatenate([attn_out, ssd_out], axis=-1) @ w_out
    h = h + hier_moe(rmsnorm(h, norm2_w), w_rg, b_rg, w_re, b_re, w_gate, w_up, w_down)
    return h, k, v, k_idx, new_conv, new_ssm


def setup_inputs(seed: int = 0) -> dict:
    key = jax.random.key(seed)
    ks = jax.random.split(key, 32)
    f32 = jnp.float32
    n_pages = PAST_LEN // PAGE_SIZE
    n_used = DEC_BATCH * n_pages
    n_pool = n_used + max(1, n_used // 4)

    def nrm(k, shape, scale):
        return jax.random.normal(k, shape, f32) * scale

    page_table = jax.random.permutation(ks[0], n_pool)[:n_used].reshape(DEC_BATCH, n_pages).astype(jnp.int32)
    dt0 = jnp.exp(jax.random.uniform(ks[1], (DEPTH, SSD_HEADS), f32, math.log(1e-3), math.log(1e-1)))
    dt_bias = dt0 + jnp.log(-jnp.expm1(-dt0))
    a_log = jnp.log(jax.random.uniform(ks[2], (DEPTH, SSD_HEADS), f32, 1.0, 16.0))
    return {
        'x_prompt': nrm(ks[3], (BATCH, SEQ, D_MODEL), 1.0),
        'x_sample': nrm(ks[4], (DEC_BATCH, DEC_SEQ, D_MODEL), 1.0),
        'cache_k': nrm(ks[5], (DEPTH, n_pool, PAGE_SIZE, N_KV_HEADS, HEAD_DIM), 1.0),
        'cache_v': nrm(ks[6], (DEPTH, n_pool, PAGE_SIZE, N_KV_HEADS, HEAD_DIM), 1.0),
        'cache_kidx': nrm(ks[7], (DEPTH, n_pool, PAGE_SIZE, IDX_DIM), 1.0),
        'state_conv': nrm(ks[8], (DEPTH, DEC_BATCH, CONV_W - 1, CONV_DIM), 1.0),
        'state_ssm': nrm(ks[9], (DEPTH, DEC_BATCH, SSD_HEADS, SSD_HEAD_DIM, D_STATE), 0.1),
        'page_table': page_table,
        'rel_bias': nrm(ks[10], (N_BUCKETS, N_HEADS), 0.3),
        'norm1_w': 1.0 + nrm(ks[11], (DEPTH, D_MODEL), 0.02),
        'w_in': nrm(ks[12], (DEPTH, D_MODEL, IN_DIM), D_MODEL ** -0.5),
        'conv_w': nrm(ks[13], (DEPTH, CONV_W, CONV_DIM), CONV_W ** -0.5),
        'conv_b': nrm(ks[14], (DEPTH, CONV_DIM), 0.02),
        'dt_bias': dt_bias,
        'a_log': a_log,
        'd_skip': 1.0 + nrm(ks[15], (DEPTH, SSD_HEADS), 0.02),
        'ssd_norm_w': 1.0 + nrm(ks[16], (DEPTH, D_SSD), 0.02),
        'w_out': nrm(ks[17], (DEPTH, D_ATT + D_SSD, D_MODEL), (D_ATT + D_SSD) ** -0.5),
        'norm2_w': 1.0 + nrm(ks[18], (DEPTH, D_MODEL), 0.02),
        'w_router_group': nrm(ks[19], (DEPTH, D_MODEL, N_EXPERT_GROUPS), D_MODEL ** -0.5),
        'b_router_group': nrm(ks[20], (DEPTH, N_EXPERT_GROUPS), 0.01),
        'w_router_expert': nrm(ks[21], (DEPTH, D_MODEL, N_EXPERTS), D_MODEL ** -0.5),
        'b_router_expert': nrm(ks[22], (DEPTH, N_EXPERTS), 0.01),
        'w_gate': nrm(ks[23], (DEPTH, N_EXPERTS, D_MODEL, D_EXPERT), D_MODEL ** -0.5),
        'w_up': nrm(ks[24], (DEPTH, N_EXPERTS, D_MODEL, D_EXPERT), D_MODEL ** -0.5),
        'w_down': nrm(ks[25], (DEPTH, N_EXPERTS, D_EXPERT, D_MODEL), D_EXPERT ** -0.5),
        'norm_f_w': 1.0 + nrm(ks[26], (D_MODEL,), 0.02),
    }


def reference(x_prompt, x_sample, cache_k, cache_v, cache_kidx, state_conv, state_ssm, page_table, rel_bias,
              norm1_w, w_in, conv_w, conv_b, dt_bias, a_log, d_skip, ssd_norm_w, w_out, norm2_w,
              w_router_group, b_router_group, w_router_expert, b_router_expert, w_gate, w_up, w_down, norm_f_w):
    h_p, h_s = x_prompt, x_sample
    kp, vp, ip, cp, sp = [], [], [], [], []
    ksm, vsm, ism, csm, ssm = [], [], [], [], []
    for l in range(DEPTH):
        lw = (norm1_w[l], w_in[l], conv_w[l], conv_b[l], dt_bias[l], a_log[l], d_skip[l], ssd_norm_w[l],
              w_out[l], norm2_w[l], w_router_group[l], b_router_group[l], w_router_expert[l],
              b_router_expert[l], w_gate[l], w_up[l], w_down[l])

        def attn_p(q, k, v, qi, ki, wi):
            return attn_prompt(q, k, v, qi, ki, wi, rel_bias)

        def attn_s(q, k, v, qi, ki, wi, l=l):
            return attn_sample(q, k, v, qi, ki, wi, cache_k[l], cache_v[l], cache_kidx[l], page_table, rel_bias)

        conv0 = jnp.zeros((h_p.shape[0], CONV_W - 1, CONV_DIM), h_p.dtype)
        ssm0 = jnp.zeros((h_p.shape[0], SSD_HEADS, SSD_HEAD_DIM, D_STATE), jnp.float32)
        h_p, k1, v1, i1, c1, s1 = trunk_layer(h_p, attn_p, conv0, ssm0, *lw)
        h_s, k2, v2, i2, c2, s2 = trunk_layer(h_s, attn_s, state_conv[l], state_ssm[l], *lw)
        kp.append(k1); vp.append(v1); ip.append(i1); cp.append(c1); sp.append(s1)
        ksm.append(k2); vsm.append(v2); ism.append(i2); csm.append(c2); ssm.append(s2)
    y_prompt = rmsnorm(h_p, norm_f_w)
    y_sample = rmsnorm(h_s, norm_f_w)
    return (y_prompt, y_sample,
            jnp.stack(kp), jnp.stack(vp), jnp.stack(ip), jnp.stack(cp), jnp.stack(sp),
            jnp.stack(ksm), jnp.stack(vsm), jnp.stack(ism), jnp.stack(csm), jnp.stack(ssm))
```

```python
import functools
import math

import jax
import jax.numpy as jnp
from jax import lax
from jax.experimental import pallas as pl
from jax.experimental.pallas import tpu as pltpu

F32 = jnp.float32
BF16 = jnp.bfloat16
I32 = jnp.int32
HIGHEST = lax.Precision.HIGHEST

HEAD_DIM = 64
N_HEADS = 8
N_KV_HEADS = 2
REP = N_HEADS // N_KV_HEADS
IDX_HEADS = 4
IDX_DIM = 64
TOPK_MAX = 256
N_BUCKETS = 32
MAX_DISTANCE = 128
D_SSD = 512
SSD_HEADS = 8
SSD_HEAD_DIM = 64
SSD_GROUPS = 2
D_STATE = 64
CONV_W = 4
CONV_DIM = D_SSD + 2 * SSD_GROUPS * D_STATE
SSD_CHUNK = 128
N_EXPERT_GROUPS = 4
EXPERTS_PER_GROUP = 8
N_EXPERTS = N_EXPERT_GROUPS * EXPERTS_PER_GROUP
TOP_K_INNER = 2
PAGE_SIZE = 128
EPS = 1e-6

LANE = 128
SUBLANE = 8
VMEM_LIMIT = 56 * 1024 * 1024
NEG = -1e30
INT_MIN = -(2 ** 31)

SPLITS = (512, 128, 128, 256, 128, 512, 768)
MISC_W = 64
MISC_DT = 68

NT = (((1,), (1,)), ((), ()))
TN = (((0,), (0,)), ((), ()))


def _params(sem):
    return pltpu.CompilerParams(dimension_semantics=sem, vmem_limit_bytes=VMEM_LIMIT)


def _sigmoid(x):
    return 1.0 / (1.0 + jnp.exp(-x))


def _sort_key(x):
    bits = lax.bitcast_convert_type(x, I32)
    return bits ^ ((bits >> 31) & 0x7FFFFFFF)


def _inproj_body(x_ref, nw_ref, w_ref, *out_refs):
    x = x_ref[...]
    ms = jnp.mean(x * x, axis=-1, keepdims=True)
    xb = ((x * lax.rsqrt(ms + EPS)) * nw_ref[...]).astype(BF16)
    off = 0
    for o_ref, n in zip(out_refs, SPLITS):
        o_ref[...] = jnp.dot(xb, w_ref[:, off:off + n], preferred_element_type=F32)
        off += n


def _inproj(x, norm_w, w_perm, tm):
    m, d = x.shape
    n_tot = sum(SPLITS)
    return pl.pallas_call(
        _inproj_body,
        grid=(m // tm,),
        in_specs=[
            pl.BlockSpec((tm, d), lambda i: (i, 0)),
            pl.BlockSpec((1, d), lambda i: (0, 0)),
            pl.BlockSpec((d, n_tot), lambda i: (0, 0)),
        ],
        out_specs=[pl.BlockSpec((tm, n), lambda i: (i, 0)) for n in SPLITS],
        out_shape=[jax.ShapeDtypeStruct((m, n), F32) for n in SPLITS],
        compiler_params=_params(("parallel",)),
        name="inproj",
    )(x, norm_w.reshape(1, d), w_perm)


def _ssd_body(z_ref, xbc_ref, misc_ref, cs_ref, h0_ref, cw_ref, cb_ref, dtb_ref, alog_ref, dsk_ref, nw_ref,
              y_ref, hl_ref, xp_ref, h_ref, yb_ref, *, cl, t_valid, nc):
    c = pl.program_id(1)

    @pl.when(c == 0)
    def _():
        xp_ref[5:8, :] = cs_ref[0]
        h_ref[...] = h0_ref[0]

    xp_ref[8:8 + cl, :] = xbc_ref[...]
    cw = cw_ref[...]
    yc = cb_ref[...]
    for j in range(CONV_W):
        yc = yc + xp_ref[5 + j:5 + j + cl, :] * cw[j:j + 1, :]
    act = yc * _sigmoid(yc)
    xp_ref[5:8, :] = xbc_ref[cl - 3:cl, :]

    xs = act[:, :D_SSD]
    bm = act[:, D_SSD:D_SSD + SSD_GROUPS * D_STATE]
    cm = act[:, D_SSD + SSD_GROUPS * D_STATE:]
    xb = xs.astype(BF16)
    cb16 = cm.astype(BF16)
    bb16 = bm.astype(BF16)

    dtr = misc_ref[:, MISC_DT:MISC_DT + SSD_HEADS] + dtb_ref[...]
    dt = jnp.maximum(dtr, 0.0) + jnp.log1p(jnp.exp(-jnp.abs(dtr)))
    if t_valid < cl:
        dt = jnp.where(lax.broadcasted_iota(I32, (cl, SSD_HEADS), 0) < t_valid, dt, 0.0)
    a_neg = -jnp.exp(alog_ref[...])
    a = dt * a_neg
    ri = lax.broadcasted_iota(I32, (cl, cl), 0)
    ci = lax.broadcasted_iota(I32, (cl, cl), 1)
    tril = ci <= ri
    acs = jnp.dot(jnp.where(tril, 1.0, 0.0), a, precision=HIGHEST, preferred_element_type=F32)
    eye = jnp.where(lax.broadcasted_iota(I32, (SSD_HEADS, SSD_HEADS), 0)
                    == lax.broadcasted_iota(I32, (SSD_HEADS, SSD_HEADS), 1), 1.0, 0.0)
    dt_t = lax.dot_general(eye, dt, NT, precision=HIGHEST, preferred_element_type=F32)
    acs_t = lax.dot_general(eye, acs, NT, precision=HIGHEST, preferred_element_type=F32)
    acs_last = acs[cl - 1:cl, :]
    w_end = jnp.exp(acs_last - acs) * dt
    e_acs = jnp.exp(acs)
    c_dec = jnp.exp(acs_last)
    dsk = dsk_ref[...]

    for g in range(SSD_GROUPS):
        gs = slice(g * D_STATE, (g + 1) * D_STATE)
        cg = cb16[:, gs]
        cbm = lax.dot_general(cg, bb16[:, gs], NT, preferred_element_type=F32)
        for r in range(SSD_HEADS // SSD_GROUPS):
            h = g * (SSD_HEADS // SSD_GROUPS) + r
            hs = slice(h * SSD_HEAD_DIM, (h + 1) * SSD_HEAD_DIM)
            seg = acs[:, h:h + 1] - acs_t[h:h + 1, :]
            decay = jnp.exp(jnp.where(tril, seg, -jnp.inf))
            sc = cbm * decay * dt_t[h:h + 1, :]
            xh = xs[:, hs]
            xhb = xb[:, hs]
            y_diag = jnp.dot(sc.astype(BF16), xhb, preferred_element_type=F32)
            bw = (bm[:, gs] * w_end[:, h:h + 1]).astype(BF16)
            st = lax.dot_general(xhb, bw, TN, preferred_element_type=F32)
            h_in = h_ref[h]
            y_off = lax.dot_general(cg, h_in.astype(BF16), NT, preferred_element_type=F32) * e_acs[:, h:h + 1]
            h_ref[h] = h_in * c_dec[:, h:h + 1] + st
            yb_ref[:, hs] = (y_diag + y_off) + dsk[:, h:h + 1] * xh

    zz = z_ref[...]
    y = yb_ref[...] * (zz * _sigmoid(zz))
    gw = D_SSD // SSD_GROUPS
    for g in range(SSD_GROUPS):
        yg = y[:, g * gw:(g + 1) * gw]
        yg = yg * lax.rsqrt(jnp.mean(yg * yg, axis=-1, keepdims=True) + EPS)
        y_ref[:, g * gw:(g + 1) * gw] = yg * nw_ref[:, g * gw:(g + 1) * gw]

    @pl.when(c == nc - 1)
    def _():
        hl_ref[0] = h_ref[...]


def _ssd(z, xbc, misc, conv_state, h0, conv_w, conv_b, dt_bias, a_log, d_skip, norm_w, *, b, nc, cl, t_valid):
    m = z.shape[0]
    row = lambda bi, ci: (bi * nc + ci, 0)
    const2 = lambda bi, ci: (0, 0)
    body = functools.partial(_ssd_body, cl=cl, t_valid=t_valid, nc=nc)
    return pl.pallas_call(
        body,
        grid=(b, nc),
        in_specs=[
            pl.BlockSpec((cl, D_SSD), row),
            pl.BlockSpec((cl, CONV_DIM), row),
            pl.BlockSpec((cl, LANE), row),
            pl.BlockSpec((1, CONV_W - 1, CONV_DIM), lambda bi, ci: (bi, 0, 0)),
            pl.BlockSpec((1, SSD_HEADS, SSD_HEAD_DIM, D_STATE), lambda bi, ci: (bi, 0, 0, 0)),
            pl.BlockSpec((CONV_W, CONV_DIM), const2),
            pl.BlockSpec((1, CONV_DIM), const2),
            pl.BlockSpec((1, SSD_HEADS), const2),
            pl.BlockSpec((1, SSD_HEADS), const2),
            pl.BlockSpec((1, SSD_HEADS), const2),
            pl.BlockSpec((1, D_SSD), const2),
        ],
        out_specs=[
            pl.BlockSpec((cl, D_SSD), row),
            pl.BlockSpec((1, SSD_HEADS, SSD_HEAD_DIM, D_STATE), lambda bi, ci: (bi, 0, 0, 0)),
        ],
        out_shape=[
            jax.ShapeDtypeStruct((m, D_SSD), F32),
            jax.ShapeDtypeStruct((b, SSD_HEADS, SSD_HEAD_DIM, D_STATE), F32),
        ],
        scratch_shapes=[
            pltpu.VMEM((SUBLANE + cl, CONV_DIM), F32),
            pltpu.VMEM((SSD_HEADS, SSD_HEAD_DIM, D_STATE), F32),
            pltpu.VMEM((cl, D_SSD), F32),
        ],
        compiler_params=_params(("parallel", "arbitrary")),
        name="ssd",
    )(z, xbc, misc, conv_state, h0, conv_w, conv_b.reshape(1, -1), dt_bias.reshape(1, -1),
      a_log.reshape(1, -1), d_skip.reshape(1, -1), norm_w.reshape(1, -1))


def _rel_bucket(dist):
    n = jnp.maximum(dist, 0)
    max_exact = N_BUCKETS // 2
    nf = jnp.maximum(n, 1).astype(F32)
    log_part = jnp.log(nf / max_exact) / math.log(MAX_DISTANCE / max_exact) * (N_BUCKETS - max_exact)
    large = jnp.minimum(max_exact + log_part.astype(I32), N_BUCKETS - 1)
    return jnp.where(n < max_exact, n, large)


def _attn_prompt_body(q_ref, qi_ref, mq_ref, k_ref, v_ref, mk_ref, bias_ref, o_ref,
                      keys_ref, qit_ref, qbd_ref, tri_ref, acc_ref, m_ref, l_ref, *, tq, kc, topk):
    i = pl.program_id(1)
    nch = (i * tq + tq + kc - 1) // kc
    nsub = kc // tq
    qpos = i * tq + lax.broadcasted_iota(I32, (1, tq), 1)

    qi_t = qi_ref[...].T
    qit_ref[...] = jnp.concatenate(
        [qi_t[h * IDX_DIM:(h + 1) * IDX_DIM, :] for h in range(IDX_HEADS)], axis=1).astype(BF16)
    w_rows = mq_ref[...].T[MISC_W:MISC_W + IDX_HEADS, :] * (IDX_DIM ** -0.5 * IDX_HEADS ** -0.5)
    q_t = q_ref[...].T * (HEAD_DIM ** -0.5)
    zeros = jnp.zeros((HEAD_DIM, tq), F32)
    cols = []
    for g in range(N_KV_HEADS):
        for r in range(REP):
            h = g * REP + r
            blk = q_t[h * HEAD_DIM:(h + 1) * HEAD_DIM, :]
            cols.append(jnp.concatenate([blk, zeros] if g == 0 else [zeros, blk], axis=0))
    qbd_ref[...] = jnp.concatenate(cols, axis=1).astype(BF16)
    tri_ref[...] = jnp.where(lax.broadcasted_iota(I32, (kc, kc), 1) < lax.broadcasted_iota(I32, (kc, kc), 0),
                             1.0, 0.0).astype(BF16)

    def idx_chunk(c, carry):
        k0 = pl.multiple_of(c * kc, kc)
        kidx = mk_ref[pl.ds(k0, kc), 0:IDX_DIM].astype(BF16)
        s = jnp.dot(kidx, qit_ref[...], preferred_element_type=F32)
        sc = jnp.maximum(s[:, 0:tq], 0.0) * w_rows[0:1, :]
        for h in range(1, IDX_HEADS):
            sc = sc + jnp.maximum(s[:, h * tq:(h + 1) * tq], 0.0) * w_rows[h:h + 1, :]
        kpos = k0 + lax.broadcasted_iota(I32, (kc, 1), 0)
        sc = jnp.where(kpos <= qpos, sc, -jnp.inf)
        keys_ref[pl.ds(k0, kc), :] = _sort_key(sc)
        return carry

    lax.fori_loop(0, nch, idx_chunk, 0)

    def count(pred_fn):
        def ch(c, part):
            k0 = pl.multiple_of(c * kc, kc)
            hit = jnp.where(pred_fn(keys_ref[pl.ds(k0, kc), :]), 1.0, 0.0)
            return part + jnp.sum(hit.reshape(kc // SUBLANE, SUBLANE, tq), axis=0)
        part = lax.fori_loop(0, nch, ch, jnp.zeros((SUBLANE, tq), F32))
        return jnp.sum(part, axis=0, keepdims=True)

    def bit_pass(bi, thr):
        cand = thr + lax.shift_left(jnp.int32(1), 31 - bi)
        cnt = count(lambda kk: kk >= cand)
        return jnp.where(cnt >= topk, cand, thr)

    thr = lax.fori_loop(0, 32, bit_pass, jnp.full((1, tq), INT_MIN, I32))
    need = topk - count(lambda kk: kk > thr)

    m_ref[...] = jnp.full(m_ref.shape, NEG, F32)
    l_ref[...] = jnp.zeros(l_ref.shape, F32)
    acc_ref[...] = jnp.zeros(acc_ref.shape, F32)

    def att_chunk(c, ties_before):
        k0 = pl.multiple_of(c * kc, kc)
        kk = keys_ref[pl.ds(k0, kc), :]
        eq = kk == thr
        eqf = jnp.where(eq, 1.0, 0.0)
        rank = jnp.dot(tri_ref[...], eqf.astype(BF16), preferred_element_type=F32) + ties_before
        kpos = k0 + lax.broadcasted_iota(I32, (kc, 1), 0)
        sel = ((kk > thr) | (eq & (rank < need))) & (kpos <= qpos)
        ties_before = ties_before + jnp.sum(eqf, axis=0, keepdims=True)

        kb = k_ref[pl.ds(k0, kc), :].astype(BF16)
        logits = jnp.dot(kb, qbd_ref[...], preferred_element_type=F32)
        dsel = [jnp.clip(i - (c * nsub + s), 0, 2) for s in range(nsub)]
        for g in range(N_KV_HEADS):
            ps, alphas = [], []
            for r in range(REP):
                h = g * REP + r
                cs = slice(h * tq, (h + 1) * tq)
                bias = jnp.concatenate([bias_ref[h, dsel[s]] for s in range(nsub)], axis=0)
                lg = jnp.where(sel, logits[:, cs] + bias, NEG)
                m_old = m_ref[:, cs]
                m_new = jnp.maximum(m_old, jnp.max(lg, axis=0, keepdims=True))
                p = jnp.exp(lg - m_new)
                alpha = jnp.exp(m_old - m_new)
                l_ref[:, cs] = alpha * l_ref[:, cs] + jnp.sum(p, axis=0, keepdims=True)
                m_ref[:, cs] = m_new
                ps.append(p.astype(BF16))
                alphas.append(alpha)
            vb = v_ref[pl.ds(k0, kc), g * HEAD_DIM:(g + 1) * HEAD_DIM].astype(BF16)
            pv = lax.dot_general(vb, jnp.concatenate(ps, axis=1), TN, preferred_element_type=F32)
            acc_ref[g] = jnp.concatenate(alphas, axis=1) * acc_ref[g] + pv
        return ties_before

    lax.fori_loop(0, nch, att_chunk, jnp.zeros((1, tq), F32))

    inv = 1.0 / l_ref[...]
    blocks = []
    for g in range(N_KV_HEADS):
        for r in range(REP):
            h = g * REP + r
            blocks.append(acc_ref[g][:, r * tq:(r + 1) * tq] * inv[:, h * tq:(h + 1) * tq])
    o_ref[...] = jnp.concatenate(blocks, axis=0).T


def _attn_prompt(q, qidx, misc, k, v, bias_t, *, b, t, tq, kc):
    m = q.shape[0]
    nq = t // tq
    topk = min(TOPK_MAX, t // 4)
    qrow = lambda bi, qi: (bi * nq + qi, 0)
    brow = lambda bi, qi: (bi, 0)
    body = functools.partial(_attn_prompt_body, tq=tq, kc=kc, topk=topk)
    return pl.pallas_call(
        body,
        grid=(b, nq),
        in_specs=[
            pl.BlockSpec((tq, N_HEADS * HEAD_DIM), qrow),
            pl.BlockSpec((tq, IDX_HEADS * IDX_DIM), qrow),
            pl.BlockSpec((tq, LANE), qrow),
            pl.BlockSpec((t, LANE), brow),
            pl.BlockSpec((t, LANE), brow),
            pl.BlockSpec((t, LANE), brow),
            pl.BlockSpec((N_HEADS, 3, tq, tq), lambda bi, qi: (0, 0, 0, 0)),
        ],
        out_specs=pl.BlockSpec((tq, N_HEADS * HEAD_DIM), qrow),
        out_shape=jax.ShapeDtypeStruct((m, N_HEADS * HEAD_DIM), F32),
        scratch_shapes=[
            pltpu.VMEM((t, tq), I32),
            pltpu.VMEM((IDX_DIM, IDX_HEADS * tq), BF16),
            pltpu.VMEM((N_KV_HEADS * HEAD_DIM, N_HEADS * tq), BF16),
            pltpu.VMEM((kc, kc), BF16),
            pltpu.VMEM((N_KV_HEADS, HEAD_DIM, REP * tq), F32),
            pltpu.VMEM((1, N_HEADS * tq), F32),
            pltpu.VMEM((1, N_HEADS * tq), F32),
        ],
        compiler_params=_params(("parallel", "arbitrary")),
        name="attn_prompt",
    )(q, qidx, misc, k, v, misc, bias_t)


def _attn_sample_body(pt_ref, qi_ref, w_ref, qbd_ref, bias_ref, mnew_ref, knew_ref, vnew_ref, *rest,
                      pps, n_pages, past, topk):
    kidx_refs = rest[0:pps]
    k_refs = rest[pps:2 * pps]
    v_refs = rest[2 * pps:3 * pps]
    o_ref = rest[3 * pps]
    keys_ref, thr_ref, need_ref, ties_ref, m_ref, l_ref, acc_ref = rest[3 * pps + 1:]
    del pt_ref
    j = pl.program_id(1)
    ng = n_pages // pps
    tp = SUBLANE
    rows = N_HEADS * tp
    qpos = past + lax.broadcasted_iota(I32, (tp, 1), 0)
    lane = lax.broadcasted_iota(I32, (1, PAGE_SIZE), 1)

    def score_page(kidx, slot, kpos0):
        s = lax.dot_general(qi_ref[0].astype(BF16), kidx.astype(BF16), NT, preferred_element_type=F32)
        w = w_ref[0] * (IDX_DIM ** -0.5 * IDX_HEADS ** -0.5)
        sc = jnp.maximum(s[0:tp], 0.0) * w[:, 0:1]
        for h in range(1, IDX_HEADS):
            sc = sc + jnp.maximum(s[h * tp:(h + 1) * tp], 0.0) * w[:, h:h + 1]
        sc = jnp.where(kpos0 + lane <= qpos, sc, -jnp.inf)
        keys_ref[slot] = _sort_key(sc)

    @pl.when(j < ng)
    def _():
        for i in range(pps):
            page = j * pps + i
            score_page(kidx_refs[i][0], page, page * PAGE_SIZE)

    @pl.when(j == ng - 1)
    def _():
        score_page(mnew_ref[0][:, 0:IDX_DIM], n_pages, past)
        for s in range(n_pages + 1, n_pages + pps):
            keys_ref[s] = jnp.full((tp, PAGE_SIZE), INT_MIN, I32)

        def count(pred_fn):
            hit = jnp.where(pred_fn(keys_ref[...]), 1.0, 0.0)
            return jnp.sum(jnp.sum(hit, axis=0), axis=1, keepdims=True)

        def bit_pass(bi, thr):
            cand = thr + lax.shift_left(jnp.int32(1), 31 - bi)
            return jnp.where(count(lambda kk: kk >= cand) >= topk, cand, thr)

        thr = lax.fori_loop(0, 32, bit_pass, jnp.full((tp, 1), INT_MIN, I32))
        thr_ref[...] = thr
        need_ref[...] = topk - count(lambda kk: kk > thr)
        ties_ref[...] = jnp.zeros(ties_ref.shape, F32)
        m_ref[...] = jnp.full(m_ref.shape, NEG, F32)
        l_ref[...] = jnp.zeros(l_ref.shape, F32)
        acc_ref[...] = jnp.zeros(acc_ref.shape, F32)

    def attend(pages):
        thr = thr_ref[...]
        need = need_ref[...]
        tri = jnp.where(lax.broadcasted_iota(I32, (PAGE_SIZE, PAGE_SIZE), 0)
                        < lax.broadcasted_iota(I32, (PAGE_SIZE, PAGE_SIZE), 1), 1.0, 0.0).astype(BF16)
        qbd = (qbd_ref[0] * (HEAD_DIM ** -0.5)).astype(BF16)
        ties = ties_ref[...]
        lgs, vbs = [], []
        for kp, vp, slot, kpos0, bias in pages:
            kk = keys_ref[slot]
            eq = kk == thr
            eqf = jnp.where(eq, 1.0, 0.0)
            rank = jnp.dot(eqf.astype(BF16), tri, preferred_element_type=F32) + ties
            sel = ((kk > thr) | (eq & (rank < need))) & (kpos0 + lane <= qpos)
            ties = ties + jnp.sum(eqf, axis=1, keepdims=True)
            self = jnp.where(sel, 1.0, 0.0)
            sel_rows = jnp.concatenate([self] * N_HEADS, axis=0) > 0.5
            lg = lax.dot_general(qbd, kp.astype(BF16), NT, preferred_element_type=F32)
            lgs.append(jnp.where(sel_rows, lg + bias, NEG))
            vbs.append(vp.astype(BF16))
        ties_ref[...] = ties
        lg = jnp.concatenate(lgs, axis=1)
        m_old = m_ref[...]
        m_new = jnp.maximum(m_old, jnp.max(lg, axis=1, keepdims=True))
        p = jnp.exp(lg - m_new)
        alpha = jnp.exp(m_old - m_new)
        l_ref[...] = alpha * l_ref[...] + jnp.sum(p, axis=1, keepdims=True)
        m_ref[...] = m_new
        pb = p.astype(BF16)
        pv = jnp.dot(pb[:, 0:PAGE_SIZE], vbs[0], preferred_element_type=F32)
        for n in range(1, len(pages)):
            pv = pv + jnp.dot(pb[:, n * PAGE_SIZE:(n + 1) * PAGE_SIZE], vbs[n], preferred_element_type=F32)
        acc_ref[...] = alpha * acc_ref[...] + pv

    @pl.when(j >= ng)
    def _():
        pages = []
        for i in range(pps):
            page = (j - ng) * pps + i
            bias = jnp.where(page == n_pages - 1, bias_ref[1], bias_ref[0])
            pages.append((k_refs[i][0], v_refs[i][0], page, page * PAGE_SIZE, bias))
        attend(pages)

    @pl.when(j == 2 * ng - 1)
    def _():
        attend([(knew_ref[0], vnew_ref[0], n_pages, past, bias_ref[2])])
        o_ref[0] = acc_ref[...] / l_ref[...]


def _attn_sample(page_table, qi4, w8, qbd, bias_s, misc_new, k_new, v_new, pool_kidx, pool_k, pool_v, *, pps):
    b, n_pages = page_table.shape
    past = n_pages * PAGE_SIZE
    t_new = 4
    topk = min(TOPK_MAX, (past + t_new) // 4)
    ng = n_pages // pps
    tp = SUBLANE
    rows = N_HEADS * tp
    per_b = lambda bi, j, pt: (bi, 0, 0)

    def kidx_map(i):
        return lambda bi, j, pt: (pt[bi * n_pages + jnp.minimum(j, ng - 1) * pps + i], 0, 0)

    def kv_map(i):
        return lambda bi, j, pt: (pt[bi * n_pages + jnp.maximum(j - ng, 0) * pps + i], 0, 0)

    in_specs = [
        pl.BlockSpec((1, IDX_HEADS * tp, IDX_DIM), per_b),
        pl.BlockSpec((1, tp, LANE), per_b),
        pl.BlockSpec((1, rows, LANE), per_b),
        pl.BlockSpec((3, rows, PAGE_SIZE), lambda bi, j, pt: (0, 0, 0)),
        pl.BlockSpec((1, PAGE_SIZE, LANE), per_b),
        pl.BlockSpec((1, PAGE_SIZE, LANE), per_b),
        pl.BlockSpec((1, PAGE_SIZE, LANE), per_b),
    ]
    in_specs += [pl.BlockSpec((1, PAGE_SIZE, IDX_DIM), kidx_map(i)) for i in range(pps)]
    in_specs += [pl.BlockSpec((1, PAGE_SIZE, LANE), kv_map(i)) for i in range(pps)]
    in_specs += [pl.BlockSpec((1, PAGE_SIZE, LANE), kv_map(i)) for i in range(pps)]
    body = functools.partial(_attn_sample_body, pps=pps, n_pages=n_pages, past=past, topk=topk)
    grid_spec = pltpu.PrefetchScalarGridSpec(
        num_scalar_prefetch=1,
        grid=(b, 2 * ng),
        in_specs=in_specs,
        out_specs=pl.BlockSpec((1, rows, LANE), per_b),
        scratch_shapes=[
            pltpu.VMEM((n_pages + pps, tp, PAGE_SIZE), I32),
            pltpu.VMEM((tp, 1), I32),
            pltpu.VMEM((tp, 1), F32),
            pltpu.VMEM((tp, 1), F32),
            pltpu.VMEM((rows, 1), F32),
            pltpu.VMEM((rows, 1), F32),
            pltpu.VMEM((rows, LANE), F32),
        ],
    )
    return pl.pallas_call(
        body,
        grid_spec=grid_spec,
        out_shape=jax.ShapeDtypeStruct((b, rows, LANE), F32),
        compiler_params=_params(("parallel", "arbitrary")),
        name="attn_sample",
    )(page_table.reshape(-1), qi4, w8, qbd, bias_s, misc_new, k_new, v_new,
      *([pool_kidx] * pps), *([pool_k] * pps), *([pool_v] * pps))


def _outproj_body(h_ref, a_ref, s_ref, wo_ref, n2_ref, wr_ref, br_ref, h1_ref, xn_ref, ri_ref, rw_ref):
    d_att = a_ref.shape[1]
    mix = (jnp.dot(a_ref[...].astype(BF16), wo_ref[0:d_att, :], preferred_element_type=F32)
           + jnp.dot(s_ref[...].astype(BF16), wo_ref[d_att:, :], preferred_element_type=F32))
    h1 = h_ref[...] + mix
    h1_ref[...] = h1
    ms = jnp.mean(h1 * h1, axis=-1, keepdims=True)
    xn = (h1 * lax.rsqrt(ms + EPS)) * n2_ref[...]
    xn_ref[...] = xn
    logits = jnp.dot(xn.astype(BF16), wr_ref[...], preferred_element_type=F32) + br_ref[...]
    tm = logits.shape[0]
    lane = lax.broadcasted_iota(I32, (tm, LANE), 1).astype(F32)
    ninf = -jnp.inf
    gl = jnp.where(lane < N_EXPERT_GROUPS, logits, ninf)
    gmax = jnp.max(gl, axis=-1, keepdims=True)
    grp = jnp.min(jnp.where(gl == gmax, lane, float(LANE)), axis=-1, keepdims=True)
    g_w = 1.0 / jnp.sum(jnp.exp(gl - gmax), axis=-1, keepdims=True)
    lo = N_EXPERT_GROUPS + grp * EXPERTS_PER_GROUP
    el = jnp.where((lane >= lo) & (lane < lo + EXPERTS_PER_GROUP), logits, ninf)
    v1 = jnp.max(el, axis=-1, keepdims=True)
    i1 = jnp.min(jnp.where(el == v1, lane, float(LANE)), axis=-1, keepdims=True)
    el2 = jnp.where(lane == i1, ninf, el)
    v2 = jnp.max(el2, axis=-1, keepdims=True)
    i2 = jnp.min(jnp.where(el2 == v2, lane, float(LANE)), axis=-1, keepdims=True)
    e2 = jnp.exp(v2 - v1)
    den = 1.0 + e2
    ids = jnp.where(lane == 0.0, i1 - N_EXPERT_GROUPS, jnp.where(lane == 1.0, i2 - N_EXPERT_GROUPS, 0.0))
    ri_ref[...] = ids.astype(I32)
    rw_ref[...] = jnp.where(lane == 0.0, g_w * (1.0 / den), jnp.where(lane == 1.0, g_w * (e2 / den), 0.0))


def _outproj(h, attn, ssd, w_out, norm2_w, w_router, b_router, tm):
    m, d = h.shape
    row = lambda i: (i, 0)
    const = lambda i: (0, 0)
    return pl.pallas_call(
        _outproj_body,
        grid=(m // tm,),
        in_specs=[
            pl.BlockSpec((tm, d), row),
            pl.BlockSpec((tm, attn.shape[1]), row),
            pl.BlockSpec((tm, ssd.shape[1]), row),
            pl.BlockSpec(w_out.shape, const),
            pl.BlockSpec((1, d), const),
            pl.BlockSpec((d, LANE), const),
            pl.BlockSpec((1, LANE), const),
        ],
        out_specs=[
            pl.BlockSpec((tm, d), row),
            pl.BlockSpec((tm, d), row),
            pl.BlockSpec((tm, LANE), row),
            pl.BlockSpec((tm, LANE), row),
        ],
        out_shape=[
            jax.ShapeDtypeStruct((m, d), F32),
            jax.ShapeDtypeStruct((m, d), F32),
            jax.ShapeDtypeStruct((m, LANE), I32),
            jax.ShapeDtypeStruct((m, LANE), F32),
        ],
        compiler_params=_params(("parallel",)),
        name="outproj_router",
    )(h, attn, ssd, w_out, norm2_w.reshape(1, d), w_router, b_router)


def _row_gather(idx_ref, n, src_hbm, dst, sem, idx_of):
    def issue(r, carry):
        pltpu.make_async_copy(src_hbm.at[pl.ds(idx_of(idx_ref, r), 1), :], dst.at[pl.ds(r, 1), :], sem).start()
        return carry
    lax.fori_loop(0, n, issue, 0)
    pltpu.make_async_copy(src_hbm.at[pl.ds(0, n), :], dst, sem).wait()


def _expert_body(blk_e_ref, tok_ref, roww_ref, x_hbm, wg_ref, wu_ref, wd_ref, o_ref, xbuf, sem):
    del blk_e_ref
    rb = xbuf.shape[0]
    _row_gather(tok_ref, rb, x_hbm, xbuf, sem, lambda ref, r: ref[0, 0, r])
    xb = xbuf[...].astype(BF16)
    hg = jnp.dot(xb, wg_ref[0], preferred_element_type=F32)
    hu = jnp.dot(xb, wu_ref[0], preferred_element_type=F32)
    hd = ((hg * _sigmoid(hg)) * hu).astype(BF16)
    y = jnp.dot(hd, wd_ref[0], preferred_element_type=F32)
    o_ref[...] = y * roww_ref[...]


def _experts(blk_e, row_tok, row_w, xn, w_gate, w_up, w_down, rb):
    cap = row_tok.shape[0]
    n_blk = cap // rb
    d = xn.shape[1]
    de = w_gate.shape[2]
    grid_spec = pltpu.PrefetchScalarGridSpec(
        num_scalar_prefetch=1,
        grid=(n_blk,),
        in_specs=[
            pl.BlockSpec((1, 1, rb), lambda i, be: (i, 0, 0), memory_space=pltpu.SMEM),
            pl.BlockSpec((rb, 1), lambda i, be: (i, 0)),
            pl.BlockSpec(memory_space=pl.ANY),
            pl.BlockSpec((1, d, de), lambda i, be: (be[i], 0, 0)),
            pl.BlockSpec((1, d, de), lambda i, be: (be[i], 0, 0)),
            pl.BlockSpec((1, de, d), lambda i, be: (be[i], 0, 0)),
        ],
        out_specs=pl.BlockSpec((rb, d), lambda i, be: (i, 0)),
        scratch_shapes=[pltpu.VMEM((rb, d), F32), pltpu.SemaphoreType.DMA],
    )
    return pl.pallas_call(
        _expert_body,
        grid_spec=grid_spec,
        out_shape=jax.ShapeDtypeStruct((cap, d), F32),
        compiler_params=_params(("arbitrary",)),
        name="experts",
    )(blk_e, row_tok.reshape(n_blk, 1, rb), row_w.reshape(cap, 1), xn, w_gate, w_up, w_down)


def _combine_body(pos_ref, h1_ref, yw_hbm, nf_ref, o_ref, buf0, buf1, sem0, sem1):
    tm = h1_ref.shape[0]
    _row_gather(pos_ref, tm, yw_hbm, buf0, sem0, lambda ref, r: ref[0, 0, 2 * r])
    _row_gather(pos_ref, tm, yw_hbm, buf1, sem1, lambda ref, r: ref[0, 0, 2 * r + 1])
    h = h1_ref[...] + (buf0[...] + buf1[...])
    ms = jnp.mean(h * h, axis=-1, keepdims=True)
    o_ref[...] = (h * lax.rsqrt(ms + EPS)) * nf_ref[...]


def _combine(pos, h1, yw, norm_f_w, tm):
    m, d = h1.shape
    return pl.pallas_call(
        _combine_body,
        grid=(m // tm,),
        in_specs=[
            pl.BlockSpec((1, 1, 2 * tm), lambda i: (i, 0, 0), memory_space=pltpu.SMEM),
            pl.BlockSpec((tm, d), lambda i: (i, 0)),
            pl.BlockSpec(memory_space=pl.ANY),
            pl.BlockSpec((1, d), lambda i: (0, 0)),
        ],
        out_specs=pl.BlockSpec((tm, d), lambda i: (i, 0)),
        out_shape=jax.ShapeDtypeStruct((m, d), F32),
        scratch_shapes=[pltpu.VMEM((tm, d), F32), pltpu.VMEM((tm, d), F32),
                        pltpu.SemaphoreType.DMA, pltpu.SemaphoreType.DMA],
        compiler_params=_params(("arbitrary",)),
        name="combine",
    )(pos.reshape(m // tm, 1, 2 * tm), h1, yw, norm_f_w.reshape(1, d))


def _permute_w_in(w_in):
    d = w_in.shape[0]
    c = 0
    parts = {}
    for name, n in (("q", 512), ("k", 128), ("v", 128), ("qi", 256), ("ki", 64), ("wi", 4),
                    ("z", 512), ("xbc", 768), ("dt", 8)):
        parts[name] = w_in[:, c:c + n]
        c += n
    pad = jnp.zeros((d, LANE - 64 - 4 - 8), w_in.dtype)
    return jnp.concatenate([parts["q"], parts["k"], parts["v"], parts["qi"], parts["ki"], parts["wi"],
                            parts["dt"], pad, parts["z"], parts["xbc"]], axis=1).astype(BF16)


def _route_tables(expert, gate, rb):
    a = expert.shape[0] * TOP_K_INNER
    e_flat = expert.reshape(-1)
    gw = gate.reshape(-1)
    order = jnp.argsort(e_flat)
    e_s = e_flat[order]
    sizes = jnp.bincount(e_flat, length=N_EXPERTS).astype(I32)
    starts = jnp.cumsum(sizes) - sizes
    padded = (sizes + rb - 1) // rb * rb
    pends = jnp.cumsum(padded)
    pstarts = pends - padded
    dest = pstarts[e_s] + jnp.arange(a, dtype=I32) - starts[e_s]
    cap = -(-(a + N_EXPERTS * (rb - 1)) // rb) * rb
    n_blk = cap // rb
    row_tok = jnp.zeros((cap,), I32).at[dest].set(order.astype(I32) // TOP_K_INNER)
    row_w = jnp.zeros((cap,), F32).at[dest].set(gw[order])
    pos = jnp.zeros((a,), I32).at[order].set(dest)
    blk_e = jnp.minimum(jnp.searchsorted(pends, jnp.arange(n_blk, dtype=I32) * rb, side="right"),
                        N_EXPERTS - 1).astype(I32)
    return blk_e, row_tok, row_w, pos


def _pick(n, prefs):
    for p in prefs:
        if n % p == 0:
            return p
    return n


def _moe_and_final(h, attn, ssd, w_out, norm2_w, w_router, b_router, w_gate, w_up, w_down, norm_f_w):
    m = h.shape[0]
    tm = _pick(m, (512, 256, 128))
    h1, xn, ri, rw = _outproj(h, attn, ssd, w_out, norm2_w, w_router, b_router, tm)
    rb = 256 if m >= 4096 else 128
    blk_e, row_tok, row_w, pos = _route_tables(ri[:, :TOP_K_INNER], rw[:, :TOP_K_INNER], rb)
    yw = _experts(blk_e, row_tok, row_w, xn, w_gate, w_up, w_down, rb)
    return _combine(pos, h1, yw, norm_f_w, _pick(m, (256, 128)))


def kernel(x_prompt, x_sample, cache_k, cache_v, cache_kidx, state_conv, state_ssm, page_table, rel_bias,
           norm1_w, w_in, conv_w, conv_b, dt_bias, a_log, d_skip, ssd_norm_w, w_out, norm2_w,
           w_router_group, b_router_group, w_router_expert, b_router_expert, w_gate, w_up, w_down, norm_f_w):
    bp, tp_len, d = x_prompt.shape
    bs, ts, _ = x_sample.shape
    depth = w_in.shape[0]
    assert depth == 1 and ts == 4 and tp_len % SSD_CHUNK == 0
    l = 0

    w_perm = _permute_w_in(w_in[l])
    w_out_b = w_out[l].astype(BF16)
    n_r = N_EXPERT_GROUPS + N_EXPERTS
    w_router = jnp.concatenate([w_router_group[l], w_router_expert[l],
                                jnp.zeros((d, LANE - n_r), F32)], axis=1).astype(BF16)
    b_router = jnp.concatenate([b_router_group[l], b_router_expert[l], jnp.zeros((LANE - n_r,), F32)]).reshape(1, LANE)
    wg_b, wu_b, wd_b = w_gate[l].astype(BF16), w_up[l].astype(BF16), w_down[l].astype(BF16)
    rel_t = rel_bias.astype(F32).T

    mp = bp * tp_len
    xp = x_prompt.reshape(mp, d)
    q_p, k_p, v_p, qi_p, misc_p, z_p, xbc_p = _inproj(xp, norm1_w[l], w_perm, _pick(mp, (512, 256, 128)))
    conv0 = jnp.zeros((bp, CONV_W - 1, CONV_DIM), F32)
    ssm0 = jnp.zeros((bp, SSD_HEADS, SSD_HEAD_DIM, D_STATE), F32)
    ssd_p, ssm_p = _ssd(z_p, xbc_p, misc_p, conv0, ssm0, conv_w[l], conv_b[l], dt_bias[l], a_log[l], d_skip[l],
                        ssd_norm_w[l], b=bp, nc=tp_len // SSD_CHUNK, cl=SSD_CHUNK, t_valid=SSD_CHUNK)
    tq = 128
    kc = 256 if tp_len % 256 == 0 else 128
    jj = jnp.arange(tq, dtype=I32)[:, None]
    qq = jnp.arange(tq, dtype=I32)[None, :]
    dist_p = jnp.stack([dd * tq + qq - jj for dd in range(3)])
    bias_p = rel_t[:, _rel_bucket(dist_p)]
    attn_p = _attn_prompt(q_p, qi_p, misc_p, k_p, v_p, bias_p, b=bp, t=tp_len, tq=tq, kc=kc)
    y_p = _moe_and_final(xp, attn_p, ssd_p, w_out_b, norm2_w[l], w_router, b_router, wg_b, wu_b, wd_b, norm_f_w)

    ms = bs * ts
    xs = x_sample.reshape(ms, d)
    q_s, k_s, v_s, qi_s, misc_s, z_s, xbc_s = _inproj(xs, norm1_w[l], w_perm, _pick(ms, (512, 256, 128)))
    tpad = SUBLANE

    def pad_t(a):
        n = a.shape[1]
        return jnp.pad(a.reshape(bs, ts, n), ((0, 0), (0, tpad - ts), (0, 0))).reshape(bs * tpad, n)

    ssd_s8, ssm_s = _ssd(pad_t(z_s), pad_t(xbc_s), pad_t(misc_s), state_conv[l], state_ssm[l], conv_w[l], conv_b[l],
                         dt_bias[l], a_log[l], d_skip[l], ssd_norm_w[l], b=bs, nc=1, cl=tpad, t_valid=ts)
    ssd_s = ssd_s8.reshape(bs, tpad, D_SSD)[:, :ts].reshape(ms, D_SSD)

    n_pages = page_table.shape[1]
    past = n_pages * PAGE_SIZE
    padq = ((0, 0), (0, 0), (0, tpad - ts), (0, 0))
    qi4 = jnp.pad(qi_s.reshape(bs, ts, IDX_HEADS, IDX_DIM).transpose(0, 2, 1, 3), padq)
    qi4 = qi4.reshape(bs, IDX_HEADS * tpad, IDX_DIM)
    w8 = jnp.pad(misc_s.reshape(bs, ts, LANE)[:, :, MISC_W:MISC_W + IDX_HEADS],
                 ((0, 0), (0, tpad - ts), (0, LANE - IDX_HEADS)))
    qh = jnp.pad(q_s.reshape(bs, ts, N_HEADS, HEAD_DIM).transpose(0, 2, 1, 3), padq)
    qh = qh.reshape(bs, N_KV_HEADS, REP * tpad, HEAD_DIM)
    zq = jnp.zeros_like(qh[:, 0])
    qbd = jnp.concatenate([jnp.concatenate([qh[:, 0], zq], axis=-1),
                           jnp.concatenate([zq, qh[:, 1]], axis=-1)], axis=1)
    tt = jnp.tile(jnp.arange(tpad, dtype=I32), N_HEADS)[:, None]
    hh = jnp.repeat(jnp.arange(N_HEADS, dtype=I32), tpad)[:, None]
    jl = jnp.arange(PAGE_SIZE, dtype=I32)[None, :]
    dist_s = jnp.stack([2 * PAGE_SIZE + tt - jl + PAGE_SIZE, PAGE_SIZE + tt - jl, tt - jl])
    bias_s = rel_t[hh[None], _rel_bucket(dist_s)]
    padk = ((0, 0), (0, PAGE_SIZE - ts), (0, 0))
    misc_new = jnp.pad(misc_s.reshape(bs, ts, LANE), padk)
    k_new = jnp.pad(k_s.reshape(bs, ts, LANE), padk)
    v_new = jnp.pad(v_s.reshape(bs, ts, LANE), padk)
    n_pool = cache_k.shape[1]
    o_s = _attn_sample(page_table, qi4, w8, qbd, bias_s, misc_new, k_new, v_new,
                       cache_kidx[l], cache_k[l].reshape(n_pool, PAGE_SIZE, LANE),
                       cache_v[l].reshape(n_pool, PAGE_SIZE, LANE), pps=8)
    o_s = o_s.reshape(bs, N_KV_HEADS, REP, tpad, N_KV_HEADS, HEAD_DIM)
    o_s = jnp.stack([o_s[:, g, :, :ts, g, :] for g in range(N_KV_HEADS)], axis=1)
    attn_s = o_s.transpose(0, 3, 1, 2, 4).reshape(ms, N_HEADS * HEAD_DIM)
    y_s = _moe_and_final(xs, attn_s, ssd_s, w_out_b, norm2_w[l], w_router, b_router, wg_b, wu_b, wd_b, norm_f_w)

    def cache_out(k_, b_, t_):
        return k_.reshape(1, b_, t_, N_KV_HEADS, HEAD_DIM)

    return (
        y_p.reshape(bp, tp_len, d),
        y_s.reshape(bs, ts, d),
        cache_out(k_p, bp, tp_len), cache_out(v_p, bp, tp_len),
        misc_p[:, :IDX_DIM].reshape(1, bp, tp_len, IDX_DIM),
        xbc_p.reshape(bp, tp_len, CONV_DIM)[:, tp_len - (CONV_W - 1):][None],
        ssm_p[None],
        cache_out(k_s, bs, ts), cache_out(v_s, bs, ts),
        misc_s[:, :IDX_DIM].reshape(1, bs, ts, IDX_DIM),
        xbc_s.reshape(bs, ts, CONV_DIM)[:, ts - (CONV_W - 1):][None],
        ssm_s[None],
    )
```

```python
import functools
import math

import jax
import jax.numpy as jnp
from jax import lax
from jax.experimental import pallas as pl
from jax.experimental.pallas import tpu as pltpu

F32 = jnp.float32
BF16 = jnp.bfloat16
I32 = jnp.int32
HIGHEST = lax.Precision.HIGHEST

HEAD_DIM = 64
N_HEADS = 8
N_KV_HEADS = 2
REP = N_HEADS // N_KV_HEADS
IDX_HEADS = 4
IDX_DIM = 64
TOPK_MAX = 256
N_BUCKETS = 32
MAX_DISTANCE = 128
D_SSD = 512
SSD_HEADS = 8
SSD_HEAD_DIM = 64
SSD_GROUPS = 2
D_STATE = 64
CONV_W = 4
CONV_DIM = D_SSD + 2 * SSD_GROUPS * D_STATE
SSD_CHUNK = 128
N_EXPERT_GROUPS = 4
EXPERTS_PER_GROUP = 8
N_EXPERTS = N_EXPERT_GROUPS * EXPERTS_PER_GROUP
TOP_K_INNER = 2
PAGE_SIZE = 128
EPS = 1e-6

LANE = 128
SUBLANE = 8
VMEM_LIMIT = 56 * 1024 * 1024
NEG = -1e30
INT_MIN = -(2 ** 31)
LOG2E = math.log2(math.e)

SPLITS = (512, 128, 128, 256, 128, 512, 768)
MISC_W = 64
MISC_DT = 68

NT = (((1,), (1,)), ((), ()))
TN = (((0,), (0,)), ((), ()))


def _params(sem):
    return pltpu.CompilerParams(dimension_semantics=sem, vmem_limit_bytes=VMEM_LIMIT)


def _sigmoid(x):
    return 1.0 / (1.0 + jnp.exp(-x))


def _sort_key(x):
    bits = lax.bitcast_convert_type(x, I32)
    return bits ^ ((bits >> 31) & 0x7FFFFFFF)


def _inproj_body(x_ref, nw_ref, w_ref, *out_refs):
    x = x_ref[...]
    ms = jnp.mean(x * x, axis=-1, keepdims=True)
    xb = ((x * lax.rsqrt(ms + EPS)) * nw_ref[...]).astype(BF16)
    off = 0
    for o_ref, n in zip(out_refs, SPLITS):
        o_ref[...] = jnp.dot(xb, w_ref[:, off:off + n], preferred_element_type=F32)
        off += n


def _inproj(x, norm_w, w_perm, tm):
    m, d = x.shape
    n_tot = sum(SPLITS)
    return pl.pallas_call(
        _inproj_body,
        grid=(m // tm,),
        in_specs=[
            pl.BlockSpec((tm, d), lambda i: (i, 0)),
            pl.BlockSpec((1, d), lambda i: (0, 0)),
            pl.BlockSpec((d, n_tot), lambda i: (0, 0)),
        ],
        out_specs=[pl.BlockSpec((tm, n), lambda i: (i, 0)) for n in SPLITS],
        out_shape=[jax.ShapeDtypeStruct((m, n), F32) for n in SPLITS],
        compiler_params=_params(("parallel",)),
        name="inproj",
    )(x, norm_w.reshape(1, d), w_perm)


def _ssd_body(z_ref, xbc_ref, misc_ref, cs_ref, h0_ref, cw_ref, cb_ref, dtb_ref, alog_ref, dsk_ref, nw_ref,
              y_ref, hl_ref, xp_ref, h_ref, yb_ref, *, cl, t_valid, nc):
    c = pl.program_id(1)

    @pl.when(c == 0)
    def _():
        xp_ref[5:8, :] = cs_ref[0]
        h_ref[...] = h0_ref[0]

    xp_ref[8:8 + cl, :] = xbc_ref[...]
    cw = cw_ref[...]
    yc = cb_ref[...]
    for j in range(CONV_W):
        yc = yc + xp_ref[5 + j:5 + j + cl, :] * cw[j:j + 1, :]
    act = yc * _sigmoid(yc)
    xp_ref[5:8, :] = xbc_ref[cl - 3:cl, :]

    xs = act[:, :D_SSD]
    bm = act[:, D_SSD:D_SSD + SSD_GROUPS * D_STATE]
    cm = act[:, D_SSD + SSD_GROUPS * D_STATE:]
    xb = xs.astype(BF16)
    cb16 = cm.astype(BF16)
    bb16 = bm.astype(BF16)

    dtr = misc_ref[:, MISC_DT:MISC_DT + SSD_HEADS] + dtb_ref[...]
    dt = jnp.maximum(dtr, 0.0) + jnp.log1p(jnp.exp(-jnp.abs(dtr)))
    if t_valid < cl:
        dt = jnp.where(lax.broadcasted_iota(I32, (cl, SSD_HEADS), 0) < t_valid, dt, 0.0)
    a_neg = -jnp.exp(alog_ref[...])
    a = dt * a_neg
    ri = lax.broadcasted_iota(I32, (cl, cl), 0)
    ci = lax.broadcasted_iota(I32, (cl, cl), 1)
    tril = ci <= ri
    acs = jnp.dot(jnp.where(tril, 1.0, 0.0), a, precision=HIGHEST, preferred_element_type=F32)
    eye = jnp.where(lax.broadcasted_iota(I32, (SSD_HEADS, SSD_HEADS), 0)
                    == lax.broadcasted_iota(I32, (SSD_HEADS, SSD_HEADS), 1), 1.0, 0.0)
    dt_t = lax.dot_general(eye, dt, NT, precision=HIGHEST, preferred_element_type=F32)
    acs_t = lax.dot_general(eye, acs, NT, precision=HIGHEST, preferred_element_type=F32)
    acs_last = acs[cl - 1:cl, :]
    w_end = jnp.exp(acs_last - acs) * dt
    e_acs = jnp.exp(acs)
    c_dec = jnp.exp(acs_last)
    dsk = dsk_ref[...]

    for g in range(SSD_GROUPS):
        gs = slice(g * D_STATE, (g + 1) * D_STATE)
        cg = cb16[:, gs]
        cbm = lax.dot_general(cg, bb16[:, gs], NT, preferred_element_type=F32)
        for r in range(SSD_HEADS // SSD_GROUPS):
            h = g * (SSD_HEADS // SSD_GROUPS) + r
            hs = slice(h * SSD_HEAD_DIM, (h + 1) * SSD_HEAD_DIM)
            seg = acs[:, h:h + 1] - acs_t[h:h + 1, :]
            decay = jnp.exp(jnp.where(tril, seg, -jnp.inf))
            sc = cbm * decay * dt_t[h:h + 1, :]
            xh = xs[:, hs]
            xhb = xb[:, hs]
            y_diag = jnp.dot(sc.astype(BF16), xhb, preferred_element_type=F32)
            bw = (bm[:, gs] * w_end[:, h:h + 1]).astype(BF16)
            st = lax.dot_general(xhb, bw, TN, preferred_element_type=F32)
            h_in = h_ref[h]
            y_off = lax.dot_general(cg, h_in.astype(BF16), NT, preferred_element_type=F32) * e_acs[:, h:h + 1]
            h_ref[h] = h_in * c_dec[:, h:h + 1] + st
            yb_ref[:, hs] = (y_diag + y_off) + dsk[:, h:h + 1] * xh

    zz = z_ref[...]
    y = yb_ref[...] * (zz * _sigmoid(zz))
    gw = D_SSD // SSD_GROUPS
    for g in range(SSD_GROUPS):
        yg = y[:, g * gw:(g + 1) * gw]
        yg = yg * lax.rsqrt(jnp.mean(yg * yg, axis=-1, keepdims=True) + EPS)
        y_ref[:, g * gw:(g + 1) * gw] = yg * nw_ref[:, g * gw:(g + 1) * gw]

    @pl.when(c == nc - 1)
    def _():
        hl_ref[0] = h_ref[...]


def _ssd(z, xbc, misc, conv_state, h0, conv_w, conv_b, dt_bias, a_log, d_skip, norm_w, *, b, nc, cl, t_valid):
    m = z.shape[0]
    row = lambda bi, ci: (bi * nc + ci, 0)
    const2 = lambda bi, ci: (0, 0)
    body = functools.partial(_ssd_body, cl=cl, t_valid=t_valid, nc=nc)
    return pl.pallas_call(
        body,
        grid=(b, nc),
        in_specs=[
            pl.BlockSpec((cl, D_SSD), row),
            pl.BlockSpec((cl, CONV_DIM), row),
            pl.BlockSpec((cl, LANE), row),
            pl.BlockSpec((1, CONV_W - 1, CONV_DIM), lambda bi, ci: (bi, 0, 0)),
            pl.BlockSpec((1, SSD_HEADS, SSD_HEAD_DIM, D_STATE), lambda bi, ci: (bi, 0, 0, 0)),
            pl.BlockSpec((CONV_W, CONV_DIM), const2),
            pl.BlockSpec((1, CONV_DIM), const2),
            pl.BlockSpec((1, SSD_HEADS), const2),
            pl.BlockSpec((1, SSD_HEADS), const2),
            pl.BlockSpec((1, SSD_HEADS), const2),
            pl.BlockSpec((1, D_SSD), const2),
        ],
        out_specs=[
            pl.BlockSpec((cl, D_SSD), row),
            pl.BlockSpec((1, SSD_HEADS, SSD_HEAD_DIM, D_STATE), lambda bi, ci: (bi, 0, 0, 0)),
        ],
        out_shape=[
            jax.ShapeDtypeStruct((m, D_SSD), F32),
            jax.ShapeDtypeStruct((b, SSD_HEADS, SSD_HEAD_DIM, D_STATE), F32),
        ],
        scratch_shapes=[
            pltpu.VMEM((SUBLANE + cl, CONV_DIM), F32),
            pltpu.VMEM((SSD_HEADS, SSD_HEAD_DIM, D_STATE), F32),
            pltpu.VMEM((cl, D_SSD), F32),
        ],
        compiler_params=_params(("parallel", "arbitrary")),
        name="ssd",
    )(z, xbc, misc, conv_state, h0, conv_w, conv_b.reshape(1, -1), dt_bias.reshape(1, -1),
      a_log.reshape(1, -1), d_skip.reshape(1, -1), norm_w.reshape(1, -1))


def _rel_bucket(dist):
    n = jnp.maximum(dist, 0)
    max_exact = N_BUCKETS // 2
    nf = jnp.maximum(n, 1).astype(F32)
    log_part = jnp.log(nf / max_exact) / math.log(MAX_DISTANCE / max_exact) * (N_BUCKETS - max_exact)
    large = jnp.minimum(max_exact + log_part.astype(I32), N_BUCKETS - 1)
    return jnp.where(n < max_exact, n, large)


def _attn_prompt_body(q_ref, qi_ref, mq_ref, k_ref, v_ref, mk_ref, bias_ref, o_ref,
                      keys_ref, qit_ref, qbd_ref, tri_ref, acc_ref, *, tq, kc, sc_rows, topk):
    i = pl.program_id(1)
    nch = (i * tq + tq + kc - 1) // kc
    nsel = (i * tq + tq + sc_rows - 1) // sc_rows
    nsub = kc // tq
    n_far = jnp.maximum((i - 1) // nsub, 0)
    qpos = i * tq + lax.broadcasted_iota(I32, (1, tq), 1)

    qi_t = qi_ref[...].T
    qit_ref[...] = jnp.concatenate(
        [qi_t[h * IDX_DIM:(h + 1) * IDX_DIM, :] for h in range(IDX_HEADS)], axis=1).astype(BF16)
    w_rows = mq_ref[...].T[MISC_W:MISC_W + IDX_HEADS, :] * (IDX_DIM ** -0.5 * IDX_HEADS ** -0.5)
    q_t = q_ref[...].T * (HEAD_DIM ** -0.5 * LOG2E)
    zeros = jnp.zeros((HEAD_DIM, tq), F32)
    cols = []
    for g in range(N_KV_HEADS):
        for r in range(REP):
            h = g * REP + r
            blk = q_t[h * HEAD_DIM:(h + 1) * HEAD_DIM, :]
            cols.append(jnp.concatenate([blk, zeros] if g == 0 else [zeros, blk], axis=0))
    qbd_ref[...] = jnp.concatenate(cols, axis=1).astype(BF16)
    tri_ref[...] = jnp.where(lax.broadcasted_iota(I32, (kc, kc), 1) < lax.broadcasted_iota(I32, (kc, kc), 0),
                             1.0, 0.0).astype(BF16)

    def idx_chunk(c, carry):
        k0 = pl.multiple_of(c * kc, kc)
        kidx = mk_ref[pl.ds(k0, kc), 0:IDX_DIM].astype(BF16)
        s = jnp.dot(kidx, qit_ref[...], preferred_element_type=F32)
        sc = jnp.maximum(s[:, 0:tq], 0.0) * w_rows[0:1, :]
        for h in range(1, IDX_HEADS):
            sc = sc + jnp.maximum(s[:, h * tq:(h + 1) * tq], 0.0) * w_rows[h:h + 1, :]
        kpos = k0 + lax.broadcasted_iota(I32, (kc, 1), 0)
        sc = jnp.where(kpos <= qpos, sc, -jnp.inf)
        keys_ref[pl.ds(k0, kc), :] = _sort_key(sc)
        return carry

    lax.fori_loop(0, nch, idx_chunk, 0)

    def fill_chunk(c, carry):
        keys_ref[pl.ds(pl.multiple_of(c * kc, kc), kc), :] = jnp.full((kc, tq), INT_MIN, I32)
        return carry

    lax.fori_loop(nch, nsel * (sc_rows // kc), fill_chunk, 0)

    n_acc = 8
    acc_rows = n_acc * SUBLANE

    def count(pred_fn):
        def ch(c, part):
            k0 = pl.multiple_of(c * sc_rows, sc_rows)
            hit = jnp.where(pred_fn(keys_ref[pl.ds(k0, sc_rows), :]), 1.0, 0.0)
            return part + jnp.sum(hit.reshape(sc_rows // acc_rows, acc_rows, tq), axis=0)
        part = lax.fori_loop(0, nsel, ch, jnp.zeros((acc_rows, tq), F32))
        return jnp.sum(part, axis=0, keepdims=True)

    def bit_pass(bi, thr):
        cand = thr + lax.shift_left(jnp.int32(1), 31 - bi)
        cnt = count(lambda kk: kk >= cand)
        return jnp.where(cnt >= topk, cand, thr)

    thr = lax.fori_loop(0, 32, bit_pass, jnp.full((1, tq), INT_MIN, I32))
    need = topk - count(lambda kk: kk > thr)

    acc_ref[...] = jnp.zeros(acc_ref.shape, F32)

    def att_chunk(far, c, carry):
        ties_before, ms, ls = carry
        k0 = pl.multiple_of(c * kc, kc)
        kk = keys_ref[pl.ds(k0, kc), :]
        eq = kk == thr
        eqf = jnp.where(eq, 1.0, 0.0)
        rank = jnp.dot(tri_ref[...], eqf.astype(BF16), preferred_element_type=F32) + ties_before
        sel = (kk > thr) | (eq & (rank < need))
        if not far:
            sel = sel & (k0 + lax.broadcasted_iota(I32, (kc, 1), 0) <= qpos)
        ties_before = ties_before + jnp.sum(eqf, axis=0, keepdims=True)

        kb = k_ref[pl.ds(k0, kc), :].astype(BF16)
        logits = jnp.dot(kb, qbd_ref[...], preferred_element_type=F32)
        ms_new, ls_new = [], []
        for g in range(N_KV_HEADS):
            ps, alphas = [], []
            for r in range(REP):
                h = g * REP + r
                cs = slice(h * tq, (h + 1) * tq)
                if far:
                    shift = bias_ref[h, 2, 0:1, :]
                    lg = jnp.where(sel, logits[:, cs], NEG)
                    m_new = jnp.maximum(ms[h], jnp.max(lg, axis=0, keepdims=True) + shift)
                    p = jnp.exp2(lg - (m_new - shift))
                else:
                    bias = jnp.concatenate(
                        [bias_ref[h, jnp.clip(i - (c * nsub + s), 0, 2)] for s in range(nsub)], axis=0)
                    lg = jnp.where(sel, logits[:, cs] + bias, NEG)
                    m_new = jnp.maximum(ms[h], jnp.max(lg, axis=0, keepdims=True))
                    p = jnp.exp2(lg - m_new)
                alpha = jnp.exp2(ms[h] - m_new)
                ls_new.append(alpha * ls[h] + jnp.sum(p, axis=0, keepdims=True))
                ms_new.append(m_new)
                ps.append(p.astype(BF16))
                alphas.append(alpha)
            vb = v_ref[pl.ds(k0, kc), g * HEAD_DIM:(g + 1) * HEAD_DIM].astype(BF16)
            pv = lax.dot_general(vb, jnp.concatenate(ps, axis=1), TN, preferred_element_type=F32)
            acc_ref[g] = jnp.concatenate(alphas, axis=1) * acc_ref[g] + pv
        return ties_before, tuple(ms_new), tuple(ls_new)

    carry = (jnp.zeros((1, tq), F32),
             tuple(jnp.full((1, tq), NEG, F32) for _ in range(N_HEADS)),
             tuple(jnp.zeros((1, tq), F32) for _ in range(N_HEADS)))
    carry = lax.fori_loop(0, n_far, functools.partial(att_chunk, True), carry)
    _, _, ls = lax.fori_loop(n_far, nch, functools.partial(att_chunk, False), carry)

    blocks = []
    for g in range(N_KV_HEADS):
        for r in range(REP):
            blocks.append(acc_ref[g][:, r * tq:(r + 1) * tq] * (1.0 / ls[g * REP + r]))
    o_ref[...] = jnp.concatenate(blocks, axis=0).T


def _attn_prompt(q, qidx, misc, k, v, bias_t, *, b, t, tq, kc):
    m = q.shape[0]
    nq = t // tq
    topk = min(TOPK_MAX, t // 4)
    qrow = lambda bi, qi: (bi * nq + qi, 0)
    brow = lambda bi, qi: (bi, 0)
    sc_rows = _pick(t, (1024, 512, 256))
    body = functools.partial(_attn_prompt_body, tq=tq, kc=kc, sc_rows=sc_rows, topk=topk)
    return pl.pallas_call(
        body,
        grid=(b, nq),
        in_specs=[
            pl.BlockSpec((tq, N_HEADS * HEAD_DIM), qrow),
            pl.BlockSpec((tq, IDX_HEADS * IDX_DIM), qrow),
            pl.BlockSpec((tq, LANE), qrow),
            pl.BlockSpec((t, LANE), brow),
            pl.BlockSpec((t, LANE), brow),
            pl.BlockSpec((t, LANE), brow),
            pl.BlockSpec((N_HEADS, 3, tq, tq), lambda bi, qi: (0, 0, 0, 0)),
        ],
        out_specs=pl.BlockSpec((tq, N_HEADS * HEAD_DIM), qrow),
        out_shape=jax.ShapeDtypeStruct((m, N_HEADS * HEAD_DIM), F32),
        scratch_shapes=[
            pltpu.VMEM((t, tq), I32),
            pltpu.VMEM((IDX_DIM, IDX_HEADS * tq), BF16),
            pltpu.VMEM((N_KV_HEADS * HEAD_DIM, N_HEADS * tq), BF16),
            pltpu.VMEM((kc, kc), BF16),
            pltpu.VMEM((N_KV_HEADS, HEAD_DIM, REP * tq), F32),
        ],
        compiler_params=_params(("parallel", "arbitrary")),
        name="attn_prompt",
    )(q, qidx, misc, k, v, misc, bias_t)


def _attn_sample_body(pt_ref, qi_ref, w_ref, qg_ref, bias_ref, kinew_ref, knew_ref, vnew_ref, *rest,
                      pps, n_pages, past, topk):
    kidx_refs = rest[0:pps]
    k_refs = rest[pps:2 * pps]
    v_refs = rest[2 * pps:3 * pps]
    o_ref = rest[3 * pps]
    keys_ref, thr_ref, need_ref, ties_ref, m_ref, l_ref, acc_ref = rest[3 * pps + 1:]
    del pt_ref
    j = pl.program_id(1)
    ng = n_pages // pps
    tp = SUBLANE
    grows = REP * tp
    qpos = past + lax.broadcasted_iota(I32, (tp, 1), 0)
    lane = lax.broadcasted_iota(I32, (1, PAGE_SIZE), 1)

    def score_page(kidx_t, slot, kpos0):
        s = jnp.dot(qi_ref[0].astype(BF16), kidx_t.astype(BF16), preferred_element_type=F32)
        w = w_ref[0] * (IDX_DIM ** -0.5 * IDX_HEADS ** -0.5)
        sc = jnp.maximum(s[0:tp], 0.0) * w[:, 0:1]
        for h in range(1, IDX_HEADS):
            sc = sc + jnp.maximum(s[h * tp:(h + 1) * tp], 0.0) * w[:, h:h + 1]
        sc = jnp.where(kpos0 + lane <= qpos, sc, -jnp.inf)
        keys_ref[slot] = _sort_key(sc)

    @pl.when(j < ng)
    def _():
        for i in range(pps):
            page = j * pps + i
            score_page(kidx_refs[i][0], page, page * PAGE_SIZE)

    @pl.when(j == ng - 1)
    def _():
        score_page(kinew_ref[0], n_pages, past)
        for s in range(n_pages + 1, n_pages + pps):
            keys_ref[s] = jnp.full((tp, PAGE_SIZE), INT_MIN, I32)

        def count(pred_fn):
            hit = jnp.where(pred_fn(keys_ref[...]), 1.0, 0.0)
            return jnp.sum(jnp.sum(hit, axis=0), axis=1, keepdims=True)

        def bit_pass(bi, thr):
            cand = thr + lax.shift_left(jnp.int32(1), 31 - bi)
            return jnp.where(count(lambda kk: kk >= cand) >= topk, cand, thr)

        thr = lax.fori_loop(0, 32, bit_pass, jnp.full((tp, 1), INT_MIN, I32))
        thr_ref[...] = thr
        need_ref[...] = topk - count(lambda kk: kk > thr)
        ties_ref[...] = jnp.zeros(ties_ref.shape, F32)
        m_ref[...] = jnp.full(m_ref.shape, NEG, F32)
        l_ref[...] = jnp.zeros(l_ref.shape, F32)
        acc_ref[...] = jnp.zeros(acc_ref.shape, F32)

    def attend(pages):
        thr = thr_ref[...]
        need = need_ref[...]
        tri = jnp.where(lax.broadcasted_iota(I32, (PAGE_SIZE, PAGE_SIZE), 0)
                        < lax.broadcasted_iota(I32, (PAGE_SIZE, PAGE_SIZE), 1), 1.0, 0.0).astype(BF16)
        qg = (qg_ref[0] * (HEAD_DIM ** -0.5 * LOG2E)).astype(BF16)
        ties = ties_ref[...]
        lgs, vbs = [], []
        for kp, vp, slot, kpos0, bias in pages:
            kk = keys_ref[slot]
            eq = kk == thr
            eqf = jnp.where(eq, 1.0, 0.0)
            rank = jnp.dot(eqf.astype(BF16), tri, preferred_element_type=F32) + ties
            sel = ((kk > thr) | (eq & (rank < need))) & (kpos0 + lane <= qpos)
            ties = ties + jnp.sum(eqf, axis=1, keepdims=True)
            self = jnp.where(sel, 1.0, 0.0)
            sel_rows = jnp.concatenate([self] * N_HEADS, axis=0) > 0.5
            kb = kp.astype(BF16)
            lg = jnp.concatenate([jnp.dot(qg[g], kb[g], preferred_element_type=F32) for g in range(N_KV_HEADS)],
                                 axis=0)
            lgs.append(jnp.where(sel_rows, lg + bias, NEG))
            vbs.append(vp.astype(BF16))
        ties_ref[...] = ties
        lg = jnp.concatenate(lgs, axis=1)
        m_old = m_ref[...]
        m_new = jnp.maximum(m_old, jnp.max(lg, axis=1, keepdims=True))
        p = jnp.exp2(lg - m_new)
        alpha = jnp.exp2(m_old - m_new)
        l_ref[...] = alpha * l_ref[...] + jnp.sum(p, axis=1, keepdims=True)
        m_ref[...] = m_new
        pb = p.astype(BF16)
        pvs = []
        for g in range(N_KV_HEADS):
            pg = pb[g * grows:(g + 1) * grows]
            pv = None
            for n in range(len(pages)):
                d = lax.dot_general(pg[:, n * PAGE_SIZE:(n + 1) * PAGE_SIZE], vbs[n][g], NT,
                                    preferred_element_type=F32)
                pv = d if pv is None else pv + d
            pvs.append(pv)
        acc_ref[...] = alpha * acc_ref[...] + jnp.concatenate(pvs, axis=0)

    @pl.when(j >= ng)
    def _():
        pages = []
        for i in range(pps):
            page = (j - ng) * pps + i
            bias = jnp.where(page == n_pages - 1, bias_ref[1], bias_ref[0])
            pages.append((k_refs[i][0], v_refs[i][0], page, page * PAGE_SIZE, bias))
        attend(pages)

    @pl.when(j == 2 * ng - 1)
    def _():
        attend([(knew_ref[0], vnew_ref[0], n_pages, past, bias_ref[2])])
        o_ref[0] = acc_ref[...] / l_ref[...]


def _attn_sample(page_table, qi4, w8, qg, bias_s, kidx_new_t, k_new_t, v_new_t, pool_kidx_t, pool_k_t, pool_v_t,
                 *, pps):
    b, n_pages = page_table.shape
    past = n_pages * PAGE_SIZE
    t_new = 4
    topk = min(TOPK_MAX, (past + t_new) // 4)
    ng = n_pages // pps
    tp = SUBLANE
    rows = N_HEADS * tp
    per_b = lambda bi, j, pt: (bi, 0, 0)
    per_b4 = lambda bi, j, pt: (bi, 0, 0, 0)

    def kidx_map(i):
        return lambda bi, j, pt: (pt[bi * n_pages + jnp.minimum(j, ng - 1) * pps + i], 0, 0)

    def kv_map(i):
        return lambda bi, j, pt: (pt[bi * n_pages + jnp.maximum(j - ng, 0) * pps + i], 0, 0, 0)

    kv_block = (1, N_KV_HEADS, HEAD_DIM, PAGE_SIZE)
    in_specs = [
        pl.BlockSpec((1, IDX_HEADS * tp, IDX_DIM), per_b),
        pl.BlockSpec((1, tp, LANE), per_b),
        pl.BlockSpec((1, N_KV_HEADS, REP * tp, HEAD_DIM), per_b4),
        pl.BlockSpec((3, rows, PAGE_SIZE), lambda bi, j, pt: (0, 0, 0)),
        pl.BlockSpec((1, IDX_DIM, PAGE_SIZE), per_b),
        pl.BlockSpec(kv_block, per_b4),
        pl.BlockSpec(kv_block, per_b4),
    ]
    in_specs += [pl.BlockSpec((1, IDX_DIM, PAGE_SIZE), kidx_map(i)) for i in range(pps)]
    in_specs += [pl.BlockSpec(kv_block, kv_map(i)) for i in range(pps)]
    in_specs += [pl.BlockSpec(kv_block, kv_map(i)) for i in range(pps)]
    body = functools.partial(_attn_sample_body, pps=pps, n_pages=n_pages, past=past, topk=topk)
    grid_spec = pltpu.PrefetchScalarGridSpec(
        num_scalar_prefetch=1,
        grid=(b, 2 * ng),
        in_specs=in_specs,
        out_specs=pl.BlockSpec((1, rows, HEAD_DIM), per_b),
        scratch_shapes=[
            pltpu.VMEM((n_pages + pps, tp, PAGE_SIZE), I32),
            pltpu.VMEM((tp, 1), I32),
            pltpu.VMEM((tp, 1), F32),
            pltpu.VMEM((tp, 1), F32),
            pltpu.VMEM((rows, 1), F32),
            pltpu.VMEM((rows, 1), F32),
            pltpu.VMEM((rows, HEAD_DIM), F32),
        ],
    )
    return pl.pallas_call(
        body,
        grid_spec=grid_spec,
        out_shape=jax.ShapeDtypeStruct((b, rows, HEAD_DIM), F32),
        compiler_params=_params(("parallel", "arbitrary")),
        name="attn_sample",
    )(page_table.reshape(-1), qi4, w8, qg, bias_s, kidx_new_t, k_new_t, v_new_t,
      *([pool_kidx_t] * pps), *([pool_k_t] * pps), *([pool_v_t] * pps))


def _outproj_body(h_ref, a_ref, s_ref, wo_ref, n2_ref, wr_ref, br_ref, h1_ref, xn_ref, ri_ref, rw_ref):
    d_att = a_ref.shape[1]
    mix = (jnp.dot(a_ref[...].astype(BF16), wo_ref[0:d_att, :], preferred_element_type=F32)
           + jnp.dot(s_ref[...].astype(BF16), wo_ref[d_att:, :], preferred_element_type=F32))
    h1 = h_ref[...] + mix
    h1_ref[...] = h1
    ms = jnp.mean(h1 * h1, axis=-1, keepdims=True)
    xn = (h1 * lax.rsqrt(ms + EPS)) * n2_ref[...]
    xn_ref[...] = xn
    logits = jnp.dot(xn.astype(BF16), wr_ref[...], preferred_element_type=F32) + br_ref[...]
    tm = logits.shape[0]
    lane = lax.broadcasted_iota(I32, (tm, LANE), 1).astype(F32)
    ninf = -jnp.inf
    gl = jnp.where(lane < N_EXPERT_GROUPS, logits, ninf)
    gmax = jnp.max(gl, axis=-1, keepdims=True)
    grp = jnp.min(jnp.where(gl == gmax, lane, float(LANE)), axis=-1, keepdims=True)
    g_w = 1.0 / jnp.sum(jnp.exp(gl - gmax), axis=-1, keepdims=True)
    lo = N_EXPERT_GROUPS + grp * EXPERTS_PER_GROUP
    el = jnp.where((lane >= lo) & (lane < lo + EXPERTS_PER_GROUP), logits, ninf)
    v1 = jnp.max(el, axis=-1, keepdims=True)
    i1 = jnp.min(jnp.where(el == v1, lane, float(LANE)), axis=-1, keepdims=True)
    el2 = jnp.where(lane == i1, ninf, el)
    v2 = jnp.max(el2, axis=-1, keepdims=True)
    i2 = jnp.min(jnp.where(el2 == v2, lane, float(LANE)), axis=-1, keepdims=True)
    e2 = jnp.exp(v2 - v1)
    den = 1.0 + e2
    ids = jnp.where(lane == 0.0, i1 - N_EXPERT_GROUPS, jnp.where(lane == 1.0, i2 - N_EXPERT_GROUPS, 0.0))
    ri_ref[...] = ids.astype(I32)
    rw_ref[...] = jnp.where(lane == 0.0, g_w * (1.0 / den), jnp.where(lane == 1.0, g_w * (e2 / den), 0.0))


def _outproj(h, attn, ssd, w_out, norm2_w, w_router, b_router, tm):
    m, d = h.shape
    row = lambda i: (i, 0)
    const = lambda i: (0, 0)
    return pl.pallas_call(
        _outproj_body,
        grid=(m // tm,),
        in_specs=[
            pl.BlockSpec((tm, d), row),
            pl.BlockSpec((tm, attn.shape[1]), row),
            pl.BlockSpec((tm, ssd.shape[1]), row),
            pl.BlockSpec(w_out.shape, const),
            pl.BlockSpec((1, d), const),
            pl.BlockSpec((d, LANE), const),
            pl.BlockSpec((1, LANE), const),
        ],
        out_specs=[
            pl.BlockSpec((tm, d), row),
            pl.BlockSpec((tm, d), row),
            pl.BlockSpec((tm, LANE), row),
            pl.BlockSpec((tm, LANE), row),
        ],
        out_shape=[
            jax.ShapeDtypeStruct((m, d), F32),
            jax.ShapeDtypeStruct((m, d), F32),
            jax.ShapeDtypeStruct((m, LANE), I32),
            jax.ShapeDtypeStruct((m, LANE), F32),
        ],
        compiler_params=_params(("parallel",)),
        name="outproj_router",
    )(h, attn, ssd, w_out, norm2_w.reshape(1, d), w_router, b_router)


def _row_gather(idx_ref, n, src_hbm, dst, sem, idx_of):
    def issue(r, carry):
        pltpu.make_async_copy(src_hbm.at[pl.ds(idx_of(idx_ref, r), 1), :], dst.at[pl.ds(r, 1), :], sem).start()
        return carry
    lax.fori_loop(0, n, issue, 0)
    pltpu.make_async_copy(src_hbm.at[pl.ds(0, n), :], dst, sem).wait()


def _expert_body(blk_e_ref, tok_ref, roww_ref, x_hbm, wg_ref, wu_ref, wd_ref, o_ref, xbuf, sem):
    del blk_e_ref
    rb = xbuf.shape[0]
    _row_gather(tok_ref, rb, x_hbm, xbuf, sem, lambda ref, r: ref[0, 0, r])
    xb = xbuf[...].astype(BF16)
    hg = jnp.dot(xb, wg_ref[0], preferred_element_type=F32)
    hu = jnp.dot(xb, wu_ref[0], preferred_element_type=F32)
    hd = ((hg * _sigmoid(hg)) * hu).astype(BF16)
    y = jnp.dot(hd, wd_ref[0], preferred_element_type=F32)
    o_ref[...] = y * roww_ref[...]


def _experts(blk_e, row_tok, row_w, xn, w_gate, w_up, w_down, rb):
    cap = row_tok.shape[0]
    n_blk = cap // rb
    d = xn.shape[1]
    de = w_gate.shape[2]
    grid_spec = pltpu.PrefetchScalarGridSpec(
        num_scalar_prefetch=1,
        grid=(n_blk,),
        in_specs=[
            pl.BlockSpec((1, 1, rb), lambda i, be: (i, 0, 0), memory_space=pltpu.SMEM),
            pl.BlockSpec((rb, 1), lambda i, be: (i, 0)),
            pl.BlockSpec(memory_space=pl.ANY),
            pl.BlockSpec((1, d, de), lambda i, be: (be[i], 0, 0)),
            pl.BlockSpec((1, d, de), lambda i, be: (be[i], 0, 0)),
            pl.BlockSpec((1, de, d), lambda i, be: (be[i], 0, 0)),
        ],
        out_specs=pl.BlockSpec((rb, d), lambda i, be: (i, 0)),
        scratch_shapes=[pltpu.VMEM((rb, d), F32), pltpu.SemaphoreType.DMA],
    )
    return pl.pallas_call(
        _expert_body,
        grid_spec=grid_spec,
        out_shape=jax.ShapeDtypeStruct((cap, d), F32),
        compiler_params=_params(("arbitrary",)),
        name="experts",
    )(blk_e, row_tok.reshape(n_blk, 1, rb), row_w.reshape(cap, 1), xn, w_gate, w_up, w_down)


def _combine_body(pos_ref, h1_ref, yw_hbm, nf_ref, o_ref, buf0, buf1, sem0, sem1):
    tm = h1_ref.shape[0]
    _row_gather(pos_ref, tm, yw_hbm, buf0, sem0, lambda ref, r: ref[0, 0, 2 * r])
    _row_gather(pos_ref, tm, yw_hbm, buf1, sem1, lambda ref, r: ref[0, 0, 2 * r + 1])
    h = h1_ref[...] + (buf0[...] + buf1[...])
    ms = jnp.mean(h * h, axis=-1, keepdims=True)
    o_ref[...] = (h * lax.rsqrt(ms + EPS)) * nf_ref[...]


def _combine(pos, h1, yw, norm_f_w, tm):
    m, d = h1.shape
    return pl.pallas_call(
        _combine_body,
        grid=(m // tm,),
        in_specs=[
            pl.BlockSpec((1, 1, 2 * tm), lambda i: (i, 0, 0), memory_space=pltpu.SMEM),
            pl.BlockSpec((tm, d), lambda i: (i, 0)),
            pl.BlockSpec(memory_space=pl.ANY),
            pl.BlockSpec((1, d), lambda i: (0, 0)),
        ],
        out_specs=pl.BlockSpec((tm, d), lambda i: (i, 0)),
        out_shape=jax.ShapeDtypeStruct((m, d), F32),
        scratch_shapes=[pltpu.VMEM((tm, d), F32), pltpu.VMEM((tm, d), F32),
                        pltpu.SemaphoreType.DMA, pltpu.SemaphoreType.DMA],
        compiler_params=_params(("arbitrary",)),
        name="combine",
    )(pos.reshape(m // tm, 1, 2 * tm), h1, yw, norm_f_w.reshape(1, d))


def _permute_w_in(w_in):
    d = w_in.shape[0]
    c = 0
    parts = {}
    for name, n in (("q", 512), ("k", 128), ("v", 128), ("qi", 256), ("ki", 64), ("wi", 4),
                    ("z", 512), ("xbc", 768), ("dt", 8)):
        parts[name] = w_in[:, c:c + n]
        c += n
    pad = jnp.zeros((d, LANE - 64 - 4 - 8), w_in.dtype)
    return jnp.concatenate([parts["q"], parts["k"], parts["v"], parts["qi"], parts["ki"], parts["wi"],
                            parts["dt"], pad, parts["z"], parts["xbc"]], axis=1).astype(BF16)


def _route_tables(expert, gate, rb):
    a = expert.shape[0] * TOP_K_INNER
    e_flat = expert.reshape(-1)
    gw = gate.reshape(-1)
    order = jnp.argsort(e_flat)
    e_s = e_flat[order]
    sizes = jnp.bincount(e_flat, length=N_EXPERTS).astype(I32)
    starts = jnp.cumsum(sizes) - sizes
    padded = (sizes + rb - 1) // rb * rb
    pends = jnp.cumsum(padded)
    pstarts = pends - padded
    dest = pstarts[e_s] + jnp.arange(a, dtype=I32) - starts[e_s]
    cap = -(-(a + N_EXPERTS * (rb - 1)) // rb) * rb
    n_blk = cap // rb
    row_tok = jnp.zeros((cap,), I32).at[dest].set(order.astype(I32) // TOP_K_INNER)
    row_w = jnp.zeros((cap,), F32).at[dest].set(gw[order])
    pos = jnp.zeros((a,), I32).at[order].set(dest)
    blk_e = jnp.minimum(jnp.searchsorted(pends, jnp.arange(n_blk, dtype=I32) * rb, side="right"),
                        N_EXPERTS - 1).astype(I32)
    return blk_e, row_tok, row_w, pos


def _pick(n, prefs):
    for p in prefs:
        if n % p == 0:
            return p
    return n


def _moe_and_final(h, attn, ssd, w_out, norm2_w, w_router, b_router, w_gate, w_up, w_down, norm_f_w):
    m = h.shape[0]
    tm = _pick(m, (512, 256, 128))
    h1, xn, ri, rw = _outproj(h, attn, ssd, w_out, norm2_w, w_router, b_router, tm)
    rb = 256 if m >= 4096 else 128
    blk_e, row_tok, row_w, pos = _route_tables(ri[:, :TOP_K_INNER], rw[:, :TOP_K_INNER], rb)
    yw = _experts(blk_e, row_tok, row_w, xn, w_gate, w_up, w_down, rb)
    return _combine(pos, h1, yw, norm_f_w, _pick(m, (256, 128)))


def kernel(x_prompt, x_sample, cache_k, cache_v, cache_kidx, state_conv, state_ssm, page_table, rel_bias,
           norm1_w, w_in, conv_w, conv_b, dt_bias, a_log, d_skip, ssd_norm_w, w_out, norm2_w,
           w_router_group, b_router_group, w_router_expert, b_router_expert, w_gate, w_up, w_down, norm_f_w):
    bp, tp_len, d = x_prompt.shape
    bs, ts, _ = x_sample.shape
    depth = w_in.shape[0]
    assert depth == 1 and ts == 4 and tp_len % SSD_CHUNK == 0
    l = 0

    w_perm = _permute_w_in(w_in[l])
    w_out_b = w_out[l].astype(BF16)
    n_r = N_EXPERT_GROUPS + N_EXPERTS
    w_router = jnp.concatenate([w_router_group[l], w_router_expert[l],
                                jnp.zeros((d, LANE - n_r), F32)], axis=1).astype(BF16)
    b_router = jnp.concatenate([b_router_group[l], b_router_expert[l], jnp.zeros((LANE - n_r,), F32)]).reshape(1, LANE)
    wg_b, wu_b, wd_b = w_gate[l].astype(BF16), w_up[l].astype(BF16), w_down[l].astype(BF16)
    rel_t = rel_bias.astype(F32).T

    mp = bp * tp_len
    xp = x_prompt.reshape(mp, d)
    q_p, k_p, v_p, qi_p, misc_p, z_p, xbc_p = _inproj(xp, norm1_w[l], w_perm, _pick(mp, (512, 256, 128)))
    conv0 = jnp.zeros((bp, CONV_W - 1, CONV_DIM), F32)
    ssm0 = jnp.zeros((bp, SSD_HEADS, SSD_HEAD_DIM, D_STATE), F32)
    ssd_p, ssm_p = _ssd(z_p, xbc_p, misc_p, conv0, ssm0, conv_w[l], conv_b[l], dt_bias[l], a_log[l], d_skip[l],
                        ssd_norm_w[l], b=bp, nc=tp_len // SSD_CHUNK, cl=SSD_CHUNK, t_valid=SSD_CHUNK)
    tq = 128
    kc = 256 if tp_len % 256 == 0 else 128
    jj = jnp.arange(tq, dtype=I32)[:, None]
    qq = jnp.arange(tq, dtype=I32)[None, :]
    dist_p = jnp.stack([dd * tq + qq - jj for dd in range(3)])
    bias_p = rel_t[:, _rel_bucket(dist_p)] * LOG2E
    attn_p = _attn_prompt(q_p, qi_p, misc_p, k_p, v_p, bias_p, b=bp, t=tp_len, tq=tq, kc=kc)
    y_p = _moe_and_final(xp, attn_p, ssd_p, w_out_b, norm2_w[l], w_router, b_router, wg_b, wu_b, wd_b, norm_f_w)

    ms = bs * ts
    xs = x_sample.reshape(ms, d)
    q_s, k_s, v_s, qi_s, misc_s, z_s, xbc_s = _inproj(xs, norm1_w[l], w_perm, _pick(ms, (512, 256, 128)))
    tpad = SUBLANE

    def pad_t(a):
        n = a.shape[1]
        return jnp.pad(a.reshape(bs, ts, n), ((0, 0), (0, tpad - ts), (0, 0))).reshape(bs * tpad, n)

    ssd_s8, ssm_s = _ssd(pad_t(z_s), pad_t(xbc_s), pad_t(misc_s), state_conv[l], state_ssm[l], conv_w[l], conv_b[l],
                         dt_bias[l], a_log[l], d_skip[l], ssd_norm_w[l], b=bs, nc=1, cl=tpad, t_valid=ts)
    ssd_s = ssd_s8.reshape(bs, tpad, D_SSD)[:, :ts].reshape(ms, D_SSD)

    n_pages = page_table.shape[1]
    past = n_pages * PAGE_SIZE
    padq = ((0, 0), (0, 0), (0, tpad - ts), (0, 0))
    qi4 = jnp.pad(qi_s.reshape(bs, ts, IDX_HEADS, IDX_DIM).transpose(0, 2, 1, 3), padq)
    qi4 = qi4.reshape(bs, IDX_HEADS * tpad, IDX_DIM)
    w8 = jnp.pad(misc_s.reshape(bs, ts, LANE)[:, :, MISC_W:MISC_W + IDX_HEADS],
                 ((0, 0), (0, tpad - ts), (0, LANE - IDX_HEADS)))
    qh = jnp.pad(q_s.reshape(bs, ts, N_HEADS, HEAD_DIM).transpose(0, 2, 1, 3), padq)
    qg = qh.reshape(bs, N_KV_HEADS, REP * tpad, HEAD_DIM)
    tt = jnp.tile(jnp.arange(tpad, dtype=I32), N_HEADS)[:, None]
    hh = jnp.repeat(jnp.arange(N_HEADS, dtype=I32), tpad)[:, None]
    jl = jnp.arange(PAGE_SIZE, dtype=I32)[None, :]
    dist_s = jnp.stack([2 * PAGE_SIZE + tt - jl + PAGE_SIZE, PAGE_SIZE + tt - jl, tt - jl])
    bias_s = rel_t[hh[None], _rel_bucket(dist_s)] * LOG2E

    def new_page_t(a, lead):
        a = a.reshape(bs, ts, lead, HEAD_DIM).transpose(0, 2, 3, 1)
        return jnp.pad(a, ((0, 0), (0, 0), (0, 0), (0, PAGE_SIZE - ts)))

    kidx_new_t = new_page_t(misc_s[:, :IDX_DIM], 1)[:, 0]
    o_s = _attn_sample(page_table, qi4, w8, qg, bias_s, kidx_new_t,
                       new_page_t(k_s, N_KV_HEADS), new_page_t(v_s, N_KV_HEADS),
                       cache_kidx[l].transpose(0, 2, 1), cache_k[l].transpose(0, 2, 3, 1),
                       cache_v[l].transpose(0, 2, 3, 1), pps=8)
    o_s = o_s.reshape(bs, N_KV_HEADS, REP, tpad, HEAD_DIM)[:, :, :, :ts]
    attn_s = o_s.transpose(0, 3, 1, 2, 4).reshape(ms, N_HEADS * HEAD_DIM)
    y_s = _moe_and_final(xs, attn_s, ssd_s, w_out_b, norm2_w[l], w_router, b_router, wg_b, wu_b, wd_b, norm_f_w)

    def cache_out(k_, b_, t_):
        return k_.reshape(1, b_, t_, N_KV_HEADS, HEAD_DIM)

    return (
        y_p.reshape(bp, tp_len, d),
        y_s.reshape(bs, ts, d),
        cache_out(k_p, bp, tp_len), cache_out(v_p, bp, tp_len),
        misc_p[:, :IDX_DIM].reshape(1, bp, tp_len, IDX_DIM),
        xbc_p.reshape(bp, tp_len, CONV_DIM)[:, tp_len - (CONV_W - 1):][None],
        ssm_p[None],
        cache_out(k_s, bs, ts), cache_out(v_s, bs, ts),
        misc_s[:, :IDX_DIM].reshape(1, bs, ts, IDX_DIM),
        xbc_s.reshape(bs, ts, CONV_DIM)[:, ts - (CONV_W - 1):][None],
        ssm_s[None],
    )
```

```python
import functools
import math

import jax
import jax.numpy as jnp
from jax import lax
from jax.experimental import pallas as pl
from jax.experimental.pallas import tpu as pltpu

F32 = jnp.float32
BF16 = jnp.bfloat16
I32 = jnp.int32
HIGHEST = lax.Precision.HIGHEST

HEAD_DIM = 64
N_HEADS = 8
N_KV_HEADS = 2
REP = N_HEADS // N_KV_HEADS
IDX_HEADS = 4
IDX_DIM = 64
TOPK_MAX = 256
N_BUCKETS = 32
MAX_DISTANCE = 128
D_SSD = 512
SSD_HEADS = 8
SSD_HEAD_DIM = 64
SSD_GROUPS = 2
D_STATE = 64
CONV_W = 4
CONV_DIM = D_SSD + 2 * SSD_GROUPS * D_STATE
SSD_CHUNK = 128
N_EXPERT_GROUPS = 4
EXPERTS_PER_GROUP = 8
N_EXPERTS = N_EXPERT_GROUPS * EXPERTS_PER_GROUP
TOP_K_INNER = 2
PAGE_SIZE = 128
EPS = 1e-6

LANE = 128
SUBLANE = 8
VMEM_LIMIT = 56 * 1024 * 1024
NEG = -1e30
INT_MIN = -(2 ** 31)
LOG2E = math.log2(math.e)

SPLITS = (512, 128, 128, 256, 128, 512, 768)
MISC_W = 64
MISC_DT = 68

NT = (((1,), (1,)), ((), ()))
TN = (((0,), (0,)), ((), ()))


def _params(sem):
    return pltpu.CompilerParams(dimension_semantics=sem, vmem_limit_bytes=VMEM_LIMIT)


def _sigmoid(x):
    return 1.0 / (1.0 + jnp.exp(-x))


def _sort_key(x):
    bits = lax.bitcast_convert_type(x, I32)
    return bits ^ ((bits >> 31) & 0x7FFFFFFF)


def _inproj_body(x_ref, nw_ref, w_ref, *out_refs):
    x = x_ref[...]
    ms = jnp.mean(x * x, axis=-1, keepdims=True)
    xb = ((x * lax.rsqrt(ms + EPS)) * nw_ref[...]).astype(BF16)
    off = 0
    for o_ref, n in zip(out_refs, SPLITS):
        o_ref[...] = jnp.dot(xb, w_ref[:, off:off + n], preferred_element_type=F32)
        off += n


def _inproj(x, norm_w, w_perm, tm):
    m, d = x.shape
    n_tot = sum(SPLITS)
    return pl.pallas_call(
        _inproj_body,
        grid=(m // tm,),
        in_specs=[
            pl.BlockSpec((tm, d), lambda i: (i, 0)),
            pl.BlockSpec((1, d), lambda i: (0, 0)),
            pl.BlockSpec((d, n_tot), lambda i: (0, 0)),
        ],
        out_specs=[pl.BlockSpec((tm, n), lambda i: (i, 0)) for n in SPLITS],
        out_shape=[jax.ShapeDtypeStruct((m, n), F32) for n in SPLITS],
        compiler_params=_params(("parallel",)),
        name="inproj",
    )(x, norm_w.reshape(1, d), w_perm)


def _ssd_body(z_ref, xbc_ref, misc_ref, cs_ref, h0_ref, cw_ref, cb_ref, dtb_ref, alog_ref, dsk_ref, nw_ref,
              y_ref, hl_ref, xp_ref, h_ref, yb_ref, *, cl, t_valid, nc):
    c = pl.program_id(1)

    @pl.when(c == 0)
    def _():
        xp_ref[5:8, :] = cs_ref[0]
        h_ref[...] = h0_ref[0]

    xp_ref[8:8 + cl, :] = xbc_ref[...]
    cw = cw_ref[...]
    yc = cb_ref[...]
    for j in range(CONV_W):
        yc = yc + xp_ref[5 + j:5 + j + cl, :] * cw[j:j + 1, :]
    act = yc * _sigmoid(yc)
    xp_ref[5:8, :] = xbc_ref[cl - 3:cl, :]

    xs = act[:, :D_SSD]
    bm = act[:, D_SSD:D_SSD + SSD_GROUPS * D_STATE]
    cm = act[:, D_SSD + SSD_GROUPS * D_STATE:]
    xb = xs.astype(BF16)
    cb16 = cm.astype(BF16)
    bb16 = bm.astype(BF16)

    dtr = misc_ref[:, MISC_DT:MISC_DT + SSD_HEADS] + dtb_ref[...]
    dt = jnp.maximum(dtr, 0.0) + jnp.log1p(jnp.exp(-jnp.abs(dtr)))
    if t_valid < cl:
        dt = jnp.where(lax.broadcasted_iota(I32, (cl, SSD_HEADS), 0) < t_valid, dt, 0.0)
    a_neg = -jnp.exp(alog_ref[...])
    a = dt * a_neg
    ri = lax.broadcasted_iota(I32, (cl, cl), 0)
    ci = lax.broadcasted_iota(I32, (cl, cl), 1)
    tril = ci <= ri
    acs = jnp.dot(jnp.where(tril, 1.0, 0.0), a, precision=HIGHEST, preferred_element_type=F32)
    eye = jnp.where(lax.broadcasted_iota(I32, (SSD_HEADS, SSD_HEADS), 0)
                    == lax.broadcasted_iota(I32, (SSD_HEADS, SSD_HEADS), 1), 1.0, 0.0)
    dt_t = lax.dot_general(eye, dt, NT, precision=HIGHEST, preferred_element_type=F32)
    acs_t = lax.dot_general(eye, acs, NT, precision=HIGHEST, preferred_element_type=F32)
    acs_last = acs[cl - 1:cl, :]
    w_end = jnp.exp(acs_last - acs) * dt
    e_acs = jnp.exp(acs)
    c_dec = jnp.exp(acs_last)
    dsk = dsk_ref[...]

    for g in range(SSD_GROUPS):
        gs = slice(g * D_STATE, (g + 1) * D_STATE)
        cg = cb16[:, gs]
        cbm = lax.dot_general(cg, bb16[:, gs], NT, preferred_element_type=F32)
        for r in range(SSD_HEADS // SSD_GROUPS):
            h = g * (SSD_HEADS // SSD_GROUPS) + r
            hs = slice(h * SSD_HEAD_DIM, (h + 1) * SSD_HEAD_DIM)
            seg = acs[:, h:h + 1] - acs_t[h:h + 1, :]
            decay = jnp.exp(jnp.where(tril, seg, -jnp.inf))
            sc = cbm * decay * dt_t[h:h + 1, :]
            xh = xs[:, hs]
            xhb = xb[:, hs]
            y_diag = jnp.dot(sc.astype(BF16), xhb, preferred_element_type=F32)
            bw = (bm[:, gs] * w_end[:, h:h + 1]).astype(BF16)
            st = lax.dot_general(xhb, bw, TN, preferred_element_type=F32)
            h_in = h_ref[h]
            y_off = lax.dot_general(cg, h_in.astype(BF16), NT, preferred_element_type=F32) * e_acs[:, h:h + 1]
            h_ref[h] = h_in * c_dec[:, h:h + 1] + st
            yb_ref[:, hs] = (y_diag + y_off) + dsk[:, h:h + 1] * xh

    zz = z_ref[...]
    y = yb_ref[...] * (zz * _sigmoid(zz))
    gw = D_SSD // SSD_GROUPS
    for g in range(SSD_GROUPS):
        yg = y[:, g * gw:(g + 1) * gw]
        yg = yg * lax.rsqrt(jnp.mean(yg * yg, axis=-1, keepdims=True) + EPS)
        y_ref[:, g * gw:(g + 1) * gw] = yg * nw_ref[:, g * gw:(g + 1) * gw]

    @pl.when(c == nc - 1)
    def _():
        hl_ref[0] = h_ref[...]


def _ssd(z, xbc, misc, conv_state, h0, conv_w, conv_b, dt_bias, a_log, d_skip, norm_w, *, b, nc, cl, t_valid):
    m = z.shape[0]
    row = lambda bi, ci: (bi * nc + ci, 0)
    const2 = lambda bi, ci: (0, 0)
    body = functools.partial(_ssd_body, cl=cl, t_valid=t_valid, nc=nc)
    return pl.pallas_call(
        body,
        grid=(b, nc),
        in_specs=[
            pl.BlockSpec((cl, D_SSD), row),
            pl.BlockSpec((cl, CONV_DIM), row),
            pl.BlockSpec((cl, LANE), row),
            pl.BlockSpec((1, CONV_W - 1, CONV_DIM), lambda bi, ci: (bi, 0, 0)),
            pl.BlockSpec((1, SSD_HEADS, SSD_HEAD_DIM, D_STATE), lambda bi, ci: (bi, 0, 0, 0)),
            pl.BlockSpec((CONV_W, CONV_DIM), const2),
            pl.BlockSpec((1, CONV_DIM), const2),
            pl.BlockSpec((1, SSD_HEADS), const2),
            pl.BlockSpec((1, SSD_HEADS), const2),
            pl.BlockSpec((1, SSD_HEADS), const2),
            pl.BlockSpec((1, D_SSD), const2),
        ],
        out_specs=[
            pl.BlockSpec((cl, D_SSD), row),
            pl.BlockSpec((1, SSD_HEADS, SSD_HEAD_DIM, D_STATE), lambda bi, ci: (bi, 0, 0, 0)),
        ],
        out_shape=[
            jax.ShapeDtypeStruct((m, D_SSD), F32),
            jax.ShapeDtypeStruct((b, SSD_HEADS, SSD_HEAD_DIM, D_STATE), F32),
        ],
        scratch_shapes=[
            pltpu.VMEM((SUBLANE + cl, CONV_DIM), F32),
            pltpu.VMEM((SSD_HEADS, SSD_HEAD_DIM, D_STATE), F32),
            pltpu.VMEM((cl, D_SSD), F32),
        ],
        compiler_params=_params(("parallel", "arbitrary")),
        name="ssd",
    )(z, xbc, misc, conv_state, h0, conv_w, conv_b.reshape(1, -1), dt_bias.reshape(1, -1),
      a_log.reshape(1, -1), d_skip.reshape(1, -1), norm_w.reshape(1, -1))


def _rel_bucket(dist):
    n = jnp.maximum(dist, 0)
    max_exact = N_BUCKETS // 2
    nf = jnp.maximum(n, 1).astype(F32)
    log_part = jnp.log(nf / max_exact) / math.log(MAX_DISTANCE / max_exact) * (N_BUCKETS - max_exact)
    large = jnp.minimum(max_exact + log_part.astype(I32), N_BUCKETS - 1)
    return jnp.where(n < max_exact, n, large)


def _attn_prompt_body(q_ref, qi_ref, mq_ref, k_ref, v_ref, mk_ref, bias_ref, o_ref,
                      keys_ref, qit_ref, qbd_ref, tri_ref, acc_ref, *, tq, kc, sc_rows, topk):
    i = pl.program_id(1)
    nch = (i * tq + tq + kc - 1) // kc
    nsel = (i * tq + tq + sc_rows - 1) // sc_rows
    nsub = kc // tq
    n_far = jnp.maximum((i - 1) // nsub, 0)
    qpos = i * tq + lax.broadcasted_iota(I32, (1, tq), 1)

    qi_t = qi_ref[...].T
    qit_ref[...] = jnp.concatenate(
        [qi_t[h * IDX_DIM:(h + 1) * IDX_DIM, :] for h in range(IDX_HEADS)], axis=1).astype(BF16)
    w_rows = mq_ref[...].T[MISC_W:MISC_W + IDX_HEADS, :] * (IDX_DIM ** -0.5 * IDX_HEADS ** -0.5)
    q_t = q_ref[...].T * (HEAD_DIM ** -0.5 * LOG2E)
    zeros = jnp.zeros((HEAD_DIM, tq), F32)
    cols = []
    for g in range(N_KV_HEADS):
        for r in range(REP):
            h = g * REP + r
            blk = q_t[h * HEAD_DIM:(h + 1) * HEAD_DIM, :]
            cols.append(jnp.concatenate([blk, zeros] if g == 0 else [zeros, blk], axis=0))
    qbd_ref[...] = jnp.concatenate(cols, axis=1).astype(BF16)
    tri_ref[...] = jnp.where(lax.broadcasted_iota(I32, (kc, kc), 1) < lax.broadcasted_iota(I32, (kc, kc), 0),
                             1.0, 0.0).astype(BF16)

    def idx_chunk(c, carry):
        k0 = pl.multiple_of(c * kc, kc)
        kidx = mk_ref[pl.ds(k0, kc), 0:IDX_DIM].astype(BF16)
        s = jnp.dot(kidx, qit_ref[...], preferred_element_type=F32)
        sc = jnp.maximum(s[:, 0:tq], 0.0) * w_rows[0:1, :]
        for h in range(1, IDX_HEADS):
            sc = sc + jnp.maximum(s[:, h * tq:(h + 1) * tq], 0.0) * w_rows[h:h + 1, :]
        kpos = k0 + lax.broadcasted_iota(I32, (kc, 1), 0)
        sc = jnp.where(kpos <= qpos, sc, -jnp.inf)
        keys_ref[pl.ds(k0, kc), :] = _sort_key(sc)
        return carry

    lax.fori_loop(0, nch, idx_chunk, 0)

    def fill_chunk(c, carry):
        keys_ref[pl.ds(pl.multiple_of(c * kc, kc), kc), :] = jnp.full((kc, tq), INT_MIN, I32)
        return carry

    lax.fori_loop(nch, nsel * (sc_rows // kc), fill_chunk, 0)

    n_acc = 8
    acc_rows = n_acc * SUBLANE

    def count(pred_fn):
        def ch(c, part):
            k0 = pl.multiple_of(c * sc_rows, sc_rows)
            hit = jnp.where(pred_fn(keys_ref[pl.ds(k0, sc_rows), :]), 1.0, 0.0)
            return part + jnp.sum(hit.reshape(sc_rows // acc_rows, acc_rows, tq), axis=0)
        part = lax.fori_loop(0, nsel, ch, jnp.zeros((acc_rows, tq), F32))
        return jnp.sum(part, axis=0, keepdims=True)

    def bit_pass(bi, thr):
        cand = thr + lax.shift_left(jnp.int32(1), 31 - bi)
        cnt = count(lambda kk: kk >= cand)
        return jnp.where(cnt >= topk, cand, thr)

    thr = lax.fori_loop(0, 32, bit_pass, jnp.full((1, tq), INT_MIN, I32))
    need = topk - count(lambda kk: kk > thr)

    acc_ref[...] = jnp.zeros(acc_ref.shape, F32)

    def att_chunk(far, c, carry):
        ties_before, ms, ls = carry
        k0 = pl.multiple_of(c * kc, kc)
        kk = keys_ref[pl.ds(k0, kc), :]
        eq = kk == thr
        eqf = jnp.where(eq, 1.0, 0.0)
        rank = jnp.dot(tri_ref[...], eqf.astype(BF16), preferred_element_type=F32) + ties_before
        sel = (kk > thr) | (eq & (rank < need))
        if not far:
            sel = sel & (k0 + lax.broadcasted_iota(I32, (kc, 1), 0) <= qpos)
        ties_before = ties_before + jnp.sum(eqf, axis=0, keepdims=True)

        kb = k_ref[pl.ds(k0, kc), :].astype(BF16)
        logits = jnp.dot(kb, qbd_ref[...], preferred_element_type=F32)
        ms_new, ls_new = [], []
        for g in range(N_KV_HEADS):
            ps, alphas = [], []
            for r in range(REP):
                h = g * REP + r
                cs = slice(h * tq, (h + 1) * tq)
                if far:
                    shift = bias_ref[h, 2, 0:1, :]
                    lg = jnp.where(sel, logits[:, cs], NEG)
                    m_new = jnp.maximum(ms[h], jnp.max(lg, axis=0, keepdims=True) + shift)
                    p = jnp.exp2(lg - (m_new - shift))
                else:
                    bias = jnp.concatenate(
                        [bias_ref[h, jnp.clip(i - (c * nsub + s), 0, 2)] for s in range(nsub)], axis=0)
                    lg = jnp.where(sel, logits[:, cs] + bias, NEG)
                    m_new = jnp.maximum(ms[h], jnp.max(lg, axis=0, keepdims=True))
                    p = jnp.exp2(lg - m_new)
                alpha = jnp.exp2(ms[h] - m_new)
                ls_new.append(alpha * ls[h] + jnp.sum(p, axis=0, keepdims=True))
                ms_new.append(m_new)
                ps.append(p.astype(BF16))
                alphas.append(alpha)
            vb = v_ref[pl.ds(k0, kc), g * HEAD_DIM:(g + 1) * HEAD_DIM].astype(BF16)
            pv = lax.dot_general(vb, jnp.concatenate(ps, axis=1), TN, preferred_element_type=F32)
            acc_ref[g] = jnp.concatenate(alphas, axis=1) * acc_ref[g] + pv
        return ties_before, tuple(ms_new), tuple(ls_new)

    carry = (jnp.zeros((1, tq), F32),
             tuple(jnp.full((1, tq), NEG, F32) for _ in range(N_HEADS)),
             tuple(jnp.zeros((1, tq), F32) for _ in range(N_HEADS)))
    carry = lax.fori_loop(0, n_far, functools.partial(att_chunk, True), carry)
    _, _, ls = lax.fori_loop(n_far, nch, functools.partial(att_chunk, False), carry)

    blocks = []
    for g in range(N_KV_HEADS):
        for r in range(REP):
            blocks.append(acc_ref[g][:, r * tq:(r + 1) * tq] * (1.0 / ls[g * REP + r]))
    o_ref[...] = jnp.concatenate(blocks, axis=0).T


def _attn_prompt(q, qidx, misc, k, v, bias_t, *, b, t, tq, kc):
    m = q.shape[0]
    nq = t // tq
    topk = min(TOPK_MAX, t // 4)
    qrow = lambda bi, qi: (bi * nq + qi, 0)
    brow = lambda bi, qi: (bi, 0)
    sc_rows = _pick(t, (1024, 512, 256))
    body = functools.partial(_attn_prompt_body, tq=tq, kc=kc, sc_rows=sc_rows, topk=topk)
    return pl.pallas_call(
        body,
        grid=(b, nq),
        in_specs=[
            pl.BlockSpec((tq, N_HEADS * HEAD_DIM), qrow),
            pl.BlockSpec((tq, IDX_HEADS * IDX_DIM), qrow),
            pl.BlockSpec((tq, LANE), qrow),
            pl.BlockSpec((t, LANE), brow),
            pl.BlockSpec((t, LANE), brow),
            pl.BlockSpec((t, LANE), brow),
            pl.BlockSpec((N_HEADS, 3, tq, tq), lambda bi, qi: (0, 0, 0, 0)),
        ],
        out_specs=pl.BlockSpec((tq, N_HEADS * HEAD_DIM), qrow),
        out_shape=jax.ShapeDtypeStruct((m, N_HEADS * HEAD_DIM), F32),
        scratch_shapes=[
            pltpu.VMEM((t, tq), I32),
            pltpu.VMEM((IDX_DIM, IDX_HEADS * tq), BF16),
            pltpu.VMEM((N_KV_HEADS * HEAD_DIM, N_HEADS * tq), BF16),
            pltpu.VMEM((kc, kc), BF16),
            pltpu.VMEM((N_KV_HEADS, HEAD_DIM, REP * tq), F32),
        ],
        compiler_params=_params(("parallel", "arbitrary")),
        name="attn_prompt",
    )(q, qidx, misc, k, v, misc, bias_t)


def _attn_sample_body(pt_ref, qi_ref, w_ref, qg_ref, bias_ref, kinew_ref, knew_ref, vnew_ref, *rest,
                      pps, n_pages, past, topk):
    kidx_refs = rest[0:pps]
    k_refs = rest[pps:2 * pps]
    v_refs = rest[2 * pps:3 * pps]
    o_ref = rest[3 * pps]
    keys_ref, thr_ref, need_ref, ties_ref, m_ref, l_ref, acc_ref = rest[3 * pps + 1:]
    del pt_ref
    j = pl.program_id(1)
    ng = n_pages // pps
    tp = SUBLANE
    grows = REP * tp
    qpos = past + lax.broadcasted_iota(I32, (tp, 1), 0)
    lane = lax.broadcasted_iota(I32, (1, PAGE_SIZE), 1)

    def score_page(kidx_t, slot, kpos0):
        s = jnp.dot(qi_ref[0].astype(BF16), kidx_t.astype(BF16), preferred_element_type=F32)
        w = w_ref[0] * (IDX_DIM ** -0.5 * IDX_HEADS ** -0.5)
        sc = jnp.maximum(s[0:tp], 0.0) * w[:, 0:1]
        for h in range(1, IDX_HEADS):
            sc = sc + jnp.maximum(s[h * tp:(h + 1) * tp], 0.0) * w[:, h:h + 1]
        sc = jnp.where(kpos0 + lane <= qpos, sc, -jnp.inf)
        keys_ref[slot] = _sort_key(sc)

    @pl.when(j < ng)
    def _():
        for i in range(pps):
            page = j * pps + i
            score_page(kidx_refs[i][0], page, page * PAGE_SIZE)

    @pl.when(j == ng - 1)
    def _():
        score_page(kinew_ref[0], n_pages, past)
        for s in range(n_pages + 1, n_pages + pps):
            keys_ref[s] = jnp.full((tp, PAGE_SIZE), INT_MIN, I32)

        n_acc = 8
        n_slots = keys_ref.shape[0]

        def count(pred_fn):
            hit = jnp.where(pred_fn(keys_ref[...]), 1.0, 0.0)
            part = jnp.sum(hit.reshape(n_slots // n_acc, n_acc, tp, PAGE_SIZE), axis=0)
            return jnp.sum(jnp.sum(part, axis=0), axis=1, keepdims=True)

        def bit_pass(bi, thr):
            cand = thr + lax.shift_left(jnp.int32(1), 31 - bi)
            return jnp.where(count(lambda kk: kk >= cand) >= topk, cand, thr)

        thr = lax.fori_loop(0, 32, bit_pass, jnp.full((tp, 1), INT_MIN, I32))
        thr_ref[...] = thr
        need_ref[...] = topk - count(lambda kk: kk > thr)
        ties_ref[...] = jnp.zeros(ties_ref.shape, F32)
        m_ref[...] = jnp.full(m_ref.shape, NEG, F32)
        l_ref[...] = jnp.zeros(l_ref.shape, F32)
        acc_ref[...] = jnp.zeros(acc_ref.shape, F32)

    def attend(pages):
        thr = thr_ref[...]
        need = need_ref[...]
        tri = jnp.where(lax.broadcasted_iota(I32, (PAGE_SIZE, PAGE_SIZE), 0)
                        < lax.broadcasted_iota(I32, (PAGE_SIZE, PAGE_SIZE), 1), 1.0, 0.0).astype(BF16)
        qg = (qg_ref[0] * (HEAD_DIM ** -0.5 * LOG2E)).astype(BF16)
        ties = ties_ref[...]
        lgs, vbs = [], []
        for kp, vp, slot, kpos0, bias in pages:
            kk = keys_ref[slot]
            eq = kk == thr
            eqf = jnp.where(eq, 1.0, 0.0)
            rank = jnp.dot(eqf.astype(BF16), tri, preferred_element_type=F32) + ties
            sel = ((kk > thr) | (eq & (rank < need))) & (kpos0 + lane <= qpos)
            ties = ties + jnp.sum(eqf, axis=1, keepdims=True)
            self = jnp.where(sel, 1.0, 0.0)
            sel_rows = jnp.concatenate([self] * N_HEADS, axis=0) > 0.5
            kb = kp.astype(BF16)
            lg = jnp.concatenate([jnp.dot(qg[g], kb[g], preferred_element_type=F32) for g in range(N_KV_HEADS)],
                                 axis=0)
            lgs.append(jnp.where(sel_rows, lg + bias, NEG))
            vbs.append(vp.astype(BF16))
        ties_ref[...] = ties
        lg = jnp.concatenate(lgs, axis=1)
        m_old = m_ref[...]
        m_new = jnp.maximum(m_old, jnp.max(lg, axis=1, keepdims=True))
        p = jnp.exp2(lg - m_new)
        alpha = jnp.exp2(m_old - m_new)
        l_ref[...] = alpha * l_ref[...] + jnp.sum(p, axis=1, keepdims=True)
        m_ref[...] = m_new
        pb = p.astype(BF16)
        pvs = []
        for g in range(N_KV_HEADS):
            pg = pb[g * grows:(g + 1) * grows]
            pv = None
            for n in range(len(pages)):
                d = lax.dot_general(pg[:, n * PAGE_SIZE:(n + 1) * PAGE_SIZE], vbs[n][g], NT,
                                    preferred_element_type=F32)
                pv = d if pv is None else pv + d
            pvs.append(pv)
        acc_ref[...] = alpha * acc_ref[...] + jnp.concatenate(pvs, axis=0)

    @pl.when(j >= ng)
    def _():
        pages = []
        for i in range(pps):
            page = (j - ng) * pps + i
            bias = jnp.where(page == n_pages - 1, bias_ref[1], bias_ref[0])
            pages.append((k_refs[i][0], v_refs[i][0], page, page * PAGE_SIZE, bias))
        attend(pages)

    @pl.when(j == 2 * ng - 1)
    def _():
        attend([(knew_ref[0], vnew_ref[0], n_pages, past, bias_ref[2])])
        o_ref[0] = acc_ref[...] / l_ref[...]


def _attn_sample(page_table, qi4, w8, qg, bias_s, kidx_new_t, k_new_t, v_new_t, pool_kidx_t, pool_k_t, pool_v_t,
                 *, pps):
    b, n_pages = page_table.shape
    past = n_pages * PAGE_SIZE
    t_new = 4
    topk = min(TOPK_MAX, (past + t_new) // 4)
    ng = n_pages // pps
    tp = SUBLANE
    rows = N_HEADS * tp
    per_b = lambda bi, j, pt: (bi, 0, 0)
    per_b4 = lambda bi, j, pt: (bi, 0, 0, 0)

    def kidx_map(i):
        return lambda bi, j, pt: (pt[bi * n_pages + jnp.minimum(j, ng - 1) * pps + i], 0, 0)

    def kv_map(i):
        return lambda bi, j, pt: (pt[bi * n_pages + jnp.maximum(j - ng, 0) * pps + i], 0, 0, 0)

    kv_block = (1, N_KV_HEADS, HEAD_DIM, PAGE_SIZE)
    in_specs = [
        pl.BlockSpec((1, IDX_HEADS * tp, IDX_DIM), per_b),
        pl.BlockSpec((1, tp, LANE), per_b),
        pl.BlockSpec((1, N_KV_HEADS, REP * tp, HEAD_DIM), per_b4),
        pl.BlockSpec((3, rows, PAGE_SIZE), lambda bi, j, pt: (0, 0, 0)),
        pl.BlockSpec((1, IDX_DIM, PAGE_SIZE), per_b),
        pl.BlockSpec(kv_block, per_b4),
        pl.BlockSpec(kv_block, per_b4),
    ]
    in_specs += [pl.BlockSpec((1, IDX_DIM, PAGE_SIZE), kidx_map(i)) for i in range(pps)]
    in_specs += [pl.BlockSpec(kv_block, kv_map(i)) for i in range(pps)]
    in_specs += [pl.BlockSpec(kv_block, kv_map(i)) for i in range(pps)]
    body = functools.partial(_attn_sample_body, pps=pps, n_pages=n_pages, past=past, topk=topk)
    grid_spec = pltpu.PrefetchScalarGridSpec(
        num_scalar_prefetch=1,
        grid=(b, 2 * ng),
        in_specs=in_specs,
        out_specs=pl.BlockSpec((1, rows, HEAD_DIM), per_b),
        scratch_shapes=[
            pltpu.VMEM((n_pages + pps, tp, PAGE_SIZE), I32),
            pltpu.VMEM((tp, 1), I32),
            pltpu.VMEM((tp, 1), F32),
            pltpu.VMEM((tp, 1), F32),
            pltpu.VMEM((rows, 1), F32),
            pltpu.VMEM((rows, 1), F32),
            pltpu.VMEM((rows, HEAD_DIM), F32),
        ],
    )
    return pl.pallas_call(
        body,
        grid_spec=grid_spec,
        out_shape=jax.ShapeDtypeStruct((b, rows, HEAD_DIM), F32),
        compiler_params=_params(("parallel", "arbitrary")),
        name="attn_sample",
    )(page_table.reshape(-1), qi4, w8, qg, bias_s, kidx_new_t, k_new_t, v_new_t,
      *([pool_kidx_t] * pps), *([pool_k_t] * pps), *([pool_v_t] * pps))


def _outproj_body(h_ref, a_ref, s_ref, wo_ref, n2_ref, wr_ref, br_ref, h1_ref, xn_ref, ri_ref, rw_ref):
    d_att = a_ref.shape[1]
    mix = (jnp.dot(a_ref[...].astype(BF16), wo_ref[0:d_att, :], preferred_element_type=F32)
           + jnp.dot(s_ref[...].astype(BF16), wo_ref[d_att:, :], preferred_element_type=F32))
    h1 = h_ref[...] + mix
    h1_ref[...] = h1
    ms = jnp.mean(h1 * h1, axis=-1, keepdims=True)
    xn = (h1 * lax.rsqrt(ms + EPS)) * n2_ref[...]
    xn_ref[...] = xn
    logits = jnp.dot(xn.astype(BF16), wr_ref[...], preferred_element_type=F32) + br_ref[...]
    tm = logits.shape[0]
    lane = lax.broadcasted_iota(I32, (tm, LANE), 1).astype(F32)
    ninf = -jnp.inf
    gl = jnp.where(lane < N_EXPERT_GROUPS, logits, ninf)
    gmax = jnp.max(gl, axis=-1, keepdims=True)
    grp = jnp.min(jnp.where(gl == gmax, lane, float(LANE)), axis=-1, keepdims=True)
    g_w = 1.0 / jnp.sum(jnp.exp(gl - gmax), axis=-1, keepdims=True)
    lo = N_EXPERT_GROUPS + grp * EXPERTS_PER_GROUP
    el = jnp.where((lane >= lo) & (lane < lo + EXPERTS_PER_GROUP), logits, ninf)
    v1 = jnp.max(el, axis=-1, keepdims=True)
    i1 = jnp.min(jnp.where(el == v1, lane, float(LANE)), axis=-1, keepdims=True)
    el2 = jnp.where(lane == i1, ninf, el)
    v2 = jnp.max(el2, axis=-1, keepdims=True)
    i2 = jnp.min(jnp.where(el2 == v2, lane, float(LANE)), axis=-1, keepdims=True)
    e2 = jnp.exp(v2 - v1)
    den = 1.0 + e2
    ids = jnp.where(lane == 0.0, i1 - N_EXPERT_GROUPS, jnp.where(lane == 1.0, i2 - N_EXPERT_GROUPS, 0.0))
    ri_ref[...] = ids.astype(I32)
    rw_ref[...] = jnp.where(lane == 0.0, g_w * (1.0 / den), jnp.where(lane == 1.0, g_w * (e2 / den), 0.0))


def _outproj(h, attn, ssd, w_out, norm2_w, w_router, b_router, tm):
    m, d = h.shape
    row = lambda i: (i, 0)
    const = lambda i: (0, 0)
    return pl.pallas_call(
        _outproj_body,
        grid=(m // tm,),
        in_specs=[
            pl.BlockSpec((tm, d), row),
            pl.BlockSpec((tm, attn.shape[1]), row),
            pl.BlockSpec((tm, ssd.shape[1]), row),
            pl.BlockSpec(w_out.shape, const),
            pl.BlockSpec((1, d), const),
            pl.BlockSpec((d, LANE), const),
            pl.BlockSpec((1, LANE), const),
        ],
        out_specs=[
            pl.BlockSpec((tm, d), row),
            pl.BlockSpec((tm, d), row),
            pl.BlockSpec((tm, LANE), row),
            pl.BlockSpec((tm, LANE), row),
        ],
        out_shape=[
            jax.ShapeDtypeStruct((m, d), F32),
            jax.ShapeDtypeStruct((m, d), F32),
            jax.ShapeDtypeStruct((m, LANE), I32),
            jax.ShapeDtypeStruct((m, LANE), F32),
        ],
        compiler_params=_params(("parallel",)),
        name="outproj_router",
    )(h, attn, ssd, w_out, norm2_w.reshape(1, d), w_router, b_router)


def _start_rows(idx_ref, n, src_hbm, dst, sem, idx_of):
    for r in range(n):
        pltpu.make_async_copy(src_hbm.at[pl.ds(idx_of(idx_ref, r), 1), :], dst.at[pl.ds(r, 1), :], sem).start()


def _wait_rows(n, src_hbm, dst, sem):
    pltpu.make_async_copy(src_hbm.at[pl.ds(0, n), :], dst, sem).wait()


def _expert_body(blk_e_ref, tok_ref, tok_next_ref, roww_ref, x_hbm, wg_ref, wu_ref, wd_ref, o_ref, xbuf, sem):
    del blk_e_ref
    i = pl.program_id(0)
    last = pl.num_programs(0) - 1
    slot = lax.rem(i, 2)
    other = 1 - slot
    rb = xbuf.shape[1]
    tok_of = lambda ref, r: ref[0, 0, r]

    @pl.when(i == 0)
    def _():
        _start_rows(tok_ref, rb, x_hbm, xbuf.at[0], sem.at[0], tok_of)

    _start_rows(tok_next_ref, rb, x_hbm, xbuf.at[other], sem.at[other], tok_of)
    _wait_rows(rb, x_hbm, xbuf.at[slot], sem.at[slot])
    xb = xbuf[slot].astype(BF16)
    hg = jnp.dot(xb, wg_ref[0], preferred_element_type=F32)
    hu = jnp.dot(xb, wu_ref[0], preferred_element_type=F32)
    hd = ((hg * _sigmoid(hg)) * hu).astype(BF16)
    y = jnp.dot(hd, wd_ref[0], preferred_element_type=F32)
    o_ref[...] = y * roww_ref[...]

    @pl.when(i == last)
    def _():
        _wait_rows(rb, x_hbm, xbuf.at[other], sem.at[other])


def _experts(blk_e, row_tok, row_w, xn, w_gate, w_up, w_down, rb):
    cap = row_tok.shape[0]
    n_blk = cap // rb
    d = xn.shape[1]
    de = w_gate.shape[2]
    tok3 = row_tok.reshape(n_blk, 1, rb)
    grid_spec = pltpu.PrefetchScalarGridSpec(
        num_scalar_prefetch=1,
        grid=(n_blk,),
        in_specs=[
            pl.BlockSpec((1, 1, rb), lambda i, be: (i, 0, 0), memory_space=pltpu.SMEM),
            pl.BlockSpec((1, 1, rb), lambda i, be: (jnp.minimum(i + 1, n_blk - 1), 0, 0), memory_space=pltpu.SMEM),
            pl.BlockSpec((rb, 1), lambda i, be: (i, 0)),
            pl.BlockSpec(memory_space=pl.ANY),
            pl.BlockSpec((1, d, de), lambda i, be: (be[i], 0, 0)),
            pl.BlockSpec((1, d, de), lambda i, be: (be[i], 0, 0)),
            pl.BlockSpec((1, de, d), lambda i, be: (be[i], 0, 0)),
        ],
        out_specs=pl.BlockSpec((rb, d), lambda i, be: (i, 0)),
        scratch_shapes=[pltpu.VMEM((2, rb, d), F32), pltpu.SemaphoreType.DMA((2,))],
    )
    return pl.pallas_call(
        _expert_body,
        grid_spec=grid_spec,
        out_shape=jax.ShapeDtypeStruct((cap, d), F32),
        compiler_params=_params(("arbitrary",)),
        name="experts",
    )(blk_e, tok3, tok3, row_w.reshape(cap, 1), xn, w_gate, w_up, w_down)


def _combine_body(pos_ref, pos_next_ref, h1_ref, yw_hbm, nf_ref, o_ref, buf, sem):
    i = pl.program_id(0)
    last = pl.num_programs(0) - 1
    slot = lax.rem(i, 2)
    other = 1 - slot
    tm = h1_ref.shape[0]

    def start(ref, s):
        for kk in range(TOP_K_INNER):
            _start_rows(ref, tm, yw_hbm, buf.at[s, kk], sem.at[s, kk],
                        lambda rf, r, kk=kk: rf[0, 0, TOP_K_INNER * r + kk])

    def wait(s):
        for kk in range(TOP_K_INNER):
            _wait_rows(tm, yw_hbm, buf.at[s, kk], sem.at[s, kk])

    @pl.when(i == 0)
    def _():
        start(pos_ref, 0)

    start(pos_next_ref, other)
    wait(slot)
    h = h1_ref[...] + (buf[slot, 0] + buf[slot, 1])
    ms = jnp.mean(h * h, axis=-1, keepdims=True)
    o_ref[...] = (h * lax.rsqrt(ms + EPS)) * nf_ref[...]

    @pl.when(i == last)
    def _():
        wait(other)


def _combine(pos, h1, yw, norm_f_w, tm):
    m, d = h1.shape
    nt = m // tm
    pos3 = pos.reshape(nt, 1, TOP_K_INNER * tm)
    return pl.pallas_call(
        _combine_body,
        grid=(nt,),
        in_specs=[
            pl.BlockSpec((1, 1, TOP_K_INNER * tm), lambda i: (i, 0, 0), memory_space=pltpu.SMEM),
            pl.BlockSpec((1, 1, TOP_K_INNER * tm), lambda i: (jnp.minimum(i + 1, nt - 1), 0, 0),
                         memory_space=pltpu.SMEM),
            pl.BlockSpec((tm, d), lambda i: (i, 0)),
            pl.BlockSpec(memory_space=pl.ANY),
            pl.BlockSpec((1, d), lambda i: (0, 0)),
        ],
        out_specs=pl.BlockSpec((tm, d), lambda i: (i, 0)),
        out_shape=jax.ShapeDtypeStruct((m, d), F32),
        scratch_shapes=[pltpu.VMEM((2, TOP_K_INNER, tm, d), F32), pltpu.SemaphoreType.DMA((2, TOP_K_INNER))],
        compiler_params=_params(("arbitrary",)),
        name="combine",
    )(pos3, pos3, h1, yw, norm_f_w.reshape(1, d))


def _permute_w_in(w_in):
    d = w_in.shape[0]
    c = 0
    parts = {}
    for name, n in (("q", 512), ("k", 128), ("v", 128), ("qi", 256), ("ki", 64), ("wi", 4),
                    ("z", 512), ("xbc", 768), ("dt", 8)):
        parts[name] = w_in[:, c:c + n]
        c += n
    pad = jnp.zeros((d, LANE - 64 - 4 - 8), w_in.dtype)
    return jnp.concatenate([parts["q"], parts["k"], parts["v"], parts["qi"], parts["ki"], parts["wi"],
                            parts["dt"], pad, parts["z"], parts["xbc"]], axis=1).astype(BF16)


def _route_tables(expert, gate, rb):
    a = expert.shape[0] * TOP_K_INNER
    e_flat = expert.reshape(-1)
    pid = jnp.arange(a, dtype=I32)
    _, order, gw_s = lax.sort((e_flat, pid, gate.reshape(-1)), num_keys=1, is_stable=True)
    eids = jnp.arange(N_EXPERTS, dtype=I32)
    sizes = jnp.sum((e_flat[:, None] == eids[None, :]).astype(I32), axis=0)
    gends = jnp.cumsum(sizes)
    padded = (sizes + rb - 1) // rb * rb
    pends = jnp.cumsum(padded)
    gap = padded - sizes
    cap = -(-(a + N_EXPERTS * (rb - 1)) // rb) * rb
    n_blk = cap // rb
    dest = pid + jnp.sum(jnp.where(gends[None, :] <= pid[:, None], gap[None, :], 0), axis=1)
    r = jnp.arange(cap, dtype=I32)
    done = pends[None, :] <= r[:, None]
    src = r - jnp.sum(jnp.where(done, gap[None, :], 0), axis=1)
    e_r = jnp.minimum(jnp.sum(done.astype(I32), axis=1), N_EXPERTS - 1)
    valid = src < jnp.sum(jnp.where(eids[None, :] == e_r[:, None], gends[None, :], 0), axis=1)
    src = jnp.minimum(src, a - 1)
    row_tok = jnp.where(valid, order[src] // TOP_K_INNER, 0)
    row_w = jnp.where(valid, gw_s[src], 0.0)
    _, pos = lax.sort((order, dest), num_keys=1)
    blk_e = jnp.minimum(jnp.sum((pends[None, :] <= (jnp.arange(n_blk, dtype=I32) * rb)[:, None]).astype(I32), axis=1),
                        N_EXPERTS - 1)
    return blk_e, row_tok, row_w, pos


def _pick(n, prefs):
    for p in prefs:
        if n % p == 0:
            return p
    return n


def _moe_and_final(h, attn, ssd, w_out, norm2_w, w_router, b_router, w_gate, w_up, w_down, norm_f_w):
    m = h.shape[0]
    tm = _pick(m, (512, 256, 128))
    h1, xn, ri, rw = _outproj(h, attn, ssd, w_out, norm2_w, w_router, b_router, tm)
    rb = 256 if m >= 4096 else 128
    blk_e, row_tok, row_w, pos = _route_tables(ri[:, :TOP_K_INNER], rw[:, :TOP_K_INNER], rb)
    yw = _experts(blk_e, row_tok, row_w, xn, w_gate, w_up, w_down, rb)
    return _combine(pos, h1, yw, norm_f_w, _pick(m, (256, 128)))


def kernel(x_prompt, x_sample, cache_k, cache_v, cache_kidx, state_conv, state_ssm, page_table, rel_bias,
           norm1_w, w_in, conv_w, conv_b, dt_bias, a_log, d_skip, ssd_norm_w, w_out, norm2_w,
           w_router_group, b_router_group, w_router_expert, b_router_expert, w_gate, w_up, w_down, norm_f_w):
    bp, tp_len, d = x_prompt.shape
    bs, ts, _ = x_sample.shape
    depth = w_in.shape[0]
    assert depth == 1 and ts == 4 and tp_len % SSD_CHUNK == 0
    l = 0

    w_perm = _permute_w_in(w_in[l])
    w_out_b = w_out[l].astype(BF16)
    n_r = N_EXPERT_GROUPS + N_EXPERTS
    w_router = jnp.concatenate([w_router_group[l], w_router_expert[l],
                                jnp.zeros((d, LANE - n_r), F32)], axis=1).astype(BF16)
    b_router = jnp.concatenate([b_router_group[l], b_router_expert[l], jnp.zeros((LANE - n_r,), F32)]).reshape(1, LANE)
    wg_b, wu_b, wd_b = w_gate[l].astype(BF16), w_up[l].astype(BF16), w_down[l].astype(BF16)
    rel_t = rel_bias.astype(F32).T

    mp = bp * tp_len
    xp = x_prompt.reshape(mp, d)
    q_p, k_p, v_p, qi_p, misc_p, z_p, xbc_p = _inproj(xp, norm1_w[l], w_perm, _pick(mp, (512, 256, 128)))
    conv0 = jnp.zeros((bp, CONV_W - 1, CONV_DIM), F32)
    ssm0 = jnp.zeros((bp, SSD_HEADS, SSD_HEAD_DIM, D_STATE), F32)
    ssd_p, ssm_p = _ssd(z_p, xbc_p, misc_p, conv0, ssm0, conv_w[l], conv_b[l], dt_bias[l], a_log[l], d_skip[l],
                        ssd_norm_w[l], b=bp, nc=tp_len // SSD_CHUNK, cl=SSD_CHUNK, t_valid=SSD_CHUNK)
    tq = 128
    kc = 256 if tp_len % 256 == 0 else 128
    jj = jnp.arange(tq, dtype=I32)[:, None]
    qq = jnp.arange(tq, dtype=I32)[None, :]
    dist_p = jnp.stack([dd * tq + qq - jj for dd in range(3)])
    bias_p = rel_t[:, _rel_bucket(dist_p)] * LOG2E
    attn_p = _attn_prompt(q_p, qi_p, misc_p, k_p, v_p, bias_p, b=bp, t=tp_len, tq=tq, kc=kc)
    y_p = _moe_and_final(xp, attn_p, ssd_p, w_out_b, norm2_w[l], w_router, b_router, wg_b, wu_b, wd_b, norm_f_w)

    ms = bs * ts
    xs = x_sample.reshape(ms, d)
    q_s, k_s, v_s, qi_s, misc_s, z_s, xbc_s = _inproj(xs, norm1_w[l], w_perm, _pick(ms, (512, 256, 128)))
    tpad = SUBLANE

    def pad_t(a):
        n = a.shape[1]
        return jnp.pad(a.reshape(bs, ts, n), ((0, 0), (0, tpad - ts), (0, 0))).reshape(bs * tpad, n)

    ssd_s8, ssm_s = _ssd(pad_t(z_s), pad_t(xbc_s), pad_t(misc_s), state_conv[l], state_ssm[l], conv_w[l], conv_b[l],
                         dt_bias[l], a_log[l], d_skip[l], ssd_norm_w[l], b=bs, nc=1, cl=tpad, t_valid=ts)
    ssd_s = ssd_s8.reshape(bs, tpad, D_SSD)[:, :ts].reshape(ms, D_SSD)

    n_pages = page_table.shape[1]
    past = n_pages * PAGE_SIZE
    padq = ((0, 0), (0, 0), (0, tpad - ts), (0, 0))
    qi4 = jnp.pad(qi_s.reshape(bs, ts, IDX_HEADS, IDX_DIM).transpose(0, 2, 1, 3), padq)
    qi4 = qi4.reshape(bs, IDX_HEADS * tpad, IDX_DIM)
    w8 = jnp.pad(misc_s.reshape(bs, ts, LANE)[:, :, MISC_W:MISC_W + IDX_HEADS],
                 ((0, 0), (0, tpad - ts), (0, LANE - IDX_HEADS)))
    qh = jnp.pad(q_s.reshape(bs, ts, N_HEADS, HEAD_DIM).transpose(0, 2, 1, 3), padq)
    qg = qh.reshape(bs, N_KV_HEADS, REP * tpad, HEAD_DIM)
    tt = jnp.tile(jnp.arange(tpad, dtype=I32), N_HEADS)[:, None]
    hh = jnp.repeat(jnp.arange(N_HEADS, dtype=I32), tpad)[:, None]
    jl = jnp.arange(PAGE_SIZE, dtype=I32)[None, :]
    dist_s = jnp.stack([2 * PAGE_SIZE + tt - jl + PAGE_SIZE, PAGE_SIZE + tt - jl, tt - jl])
    bias_s = rel_t[hh[None], _rel_bucket(dist_s)] * LOG2E

    def new_page_t(a, lead):
        a = a.reshape(bs, ts, lead, HEAD_DIM).transpose(0, 2, 3, 1)
        return jnp.pad(a, ((0, 0), (0, 0), (0, 0), (0, PAGE_SIZE - ts)))

    kidx_new_t = new_page_t(misc_s[:, :IDX_DIM], 1)[:, 0]
    o_s = _attn_sample(page_table, qi4, w8, qg, bias_s, kidx_new_t,
                       new_page_t(k_s, N_KV_HEADS), new_page_t(v_s, N_KV_HEADS),
                       cache_kidx[l].transpose(0, 2, 1), cache_k[l].transpose(0, 2, 3, 1),
                       cache_v[l].transpose(0, 2, 3, 1), pps=_pick(n_pages, (16, 8)))
    o_s = o_s.reshape(bs, N_KV_HEADS, REP, tpad, HEAD_DIM)[:, :, :, :ts]
    attn_s = o_s.transpose(0, 3, 1, 2, 4).reshape(ms, N_HEADS * HEAD_DIM)
    y_s = _moe_and_final(xs, attn_s, ssd_s, w_out_b, norm2_w[l], w_router, b_router, wg_b, wu_b, wd_b, norm_f_w)

    def cache_out(k_, b_, t_):
        return k_.reshape(1, b_, t_, N_KV_HEADS, HEAD_DIM)

    return (
        y_p.reshape(bp, tp_len, d),
        y_s.reshape(bs, ts, d),
        cache_out(k_p, bp, tp_len), cache_out(v_p, bp, tp_len),
        misc_p[:, :IDX_DIM].reshape(1, bp, tp_len, IDX_DIM),
        xbc_p.reshape(bp, tp_len, CONV_DIM)[:, tp_len - (CONV_W - 1):][None],
        ssm_p[None],
        cache_out(k_s, bs, ts), cache_out(v_s, bs, ts),
        misc_s[:, :IDX_DIM].reshape(1, bs, ts, IDX_DIM),
        xbc_s.reshape(bs, ts, CONV_DIM)[:, ts - (CONV_W - 1):][None],
        ssm_s[None],
    )
```

```python
import functools
import math

import jax
import jax.numpy as jnp
from jax import lax
from jax.experimental import pallas as pl
from jax.experimental.pallas import tpu as pltpu

F32 = jnp.float32
BF16 = jnp.bfloat16
I32 = jnp.int32
I16 = jnp.int16
I16_MIN, I16_MAX = -(2 ** 15), 2 ** 15 - 1
HIGHEST = lax.Precision.HIGHEST

HEAD_DIM = 64
N_HEADS = 8
N_KV_HEADS = 2
REP = N_HEADS // N_KV_HEADS
IDX_HEADS = 4
IDX_DIM = 64
TOPK_MAX = 256
N_BUCKETS = 32
MAX_DISTANCE = 128
D_SSD = 512
SSD_HEADS = 8
SSD_HEAD_DIM = 64
SSD_GROUPS = 2
D_STATE = 64
CONV_W = 4
CONV_DIM = D_SSD + 2 * SSD_GROUPS * D_STATE
SSD_CHUNK = 128
N_EXPERT_GROUPS = 4
EXPERTS_PER_GROUP = 8
N_EXPERTS = N_EXPERT_GROUPS * EXPERTS_PER_GROUP
TOP_K_INNER = 2
PAGE_SIZE = 128
EPS = 1e-6

LANE = 128
SUBLANE = 8
PACKED_SUBLANE = 16
VMEM_LIMIT = 56 * 1024 * 1024
NEG = -1e30
INT_MIN = -(2 ** 31)
LOG2E = math.log2(math.e)

SPLITS = (512, 128, 128, 256, 128, 512, 768)
MISC_W = 64
MISC_DT = 68
VAUG_W = N_KV_HEADS * 2 * HEAD_DIM

NT = (((1,), (1,)), ((), ()))
TN = (((0,), (0,)), ((), ()))


def _params(sem):
    return pltpu.CompilerParams(dimension_semantics=sem, vmem_limit_bytes=VMEM_LIMIT)


def _sigmoid(x):
    return 1.0 / (1.0 + jnp.exp(-x))


def _paired_loop(body, lo, hi, carry):
    odd = lax.rem(hi - lo, 2)
    carry = lax.fori_loop(lo, lo + odd, body, carry)
    first = lo + odd

    def two(j, cr):
        c = first + 2 * j
        return body(c + 1, body(c, cr))

    return lax.fori_loop(0, (hi - first) // 2, two, carry)


def _sort_key(x):
    bits = lax.bitcast_convert_type(x, I32)
    return bits ^ ((bits >> 31) & 0x7FFFFFFF)


def _inproj_body(x_ref, nw_ref, w_ref, *out_refs):
    x = x_ref[...]
    ms = jnp.mean(x * x, axis=-1, keepdims=True)
    xb = ((x * lax.rsqrt(ms + EPS)) * nw_ref[...]).astype(BF16)
    off = 0
    for o_ref, n in zip(out_refs, SPLITS):
        o_ref[...] = jnp.dot(xb, w_ref[:, off:off + n], preferred_element_type=F32)
        off += n
    v = out_refs[2][...]
    ones = jnp.ones((v.shape[0], HEAD_DIM), F32)
    out_refs[len(SPLITS)][...] = jnp.concatenate(
        [piece for g in range(N_KV_HEADS) for piece in (v[:, g * HEAD_DIM:(g + 1) * HEAD_DIM], ones)],
        axis=1).astype(BF16)


def _inproj(x, norm_w, w_perm, tm):
    m, d = x.shape
    n_tot = sum(SPLITS)
    return pl.pallas_call(
        _inproj_body,
        grid=(m // tm,),
        in_specs=[
            pl.BlockSpec((tm, d), lambda i: (i, 0)),
            pl.BlockSpec((1, d), lambda i: (0, 0)),
            pl.BlockSpec((d, n_tot), lambda i: (0, 0)),
        ],
        out_specs=[pl.BlockSpec((tm, n), lambda i: (i, 0)) for n in SPLITS + (VAUG_W,)],
        out_shape=([jax.ShapeDtypeStruct((m, n), F32) for n in SPLITS]
                   + [jax.ShapeDtypeStruct((m, VAUG_W), BF16)]),
        compiler_params=_params(("parallel",)),
        name="inproj",
    )(x, norm_w.reshape(1, d), w_perm)


def _ssd_body(z_ref, xbc_ref, misc_ref, cs_ref, h0_ref, cw_ref, cb_ref, dtb_ref, alog_ref, dsk_ref, nw_ref,
              y_ref, hl_ref, xp_ref, h_ref, yb_ref, *, cl, t_valid, nc):
    c = pl.program_id(1)

    @pl.when(c == 0)
    def _():
        xp_ref[5:8, :] = cs_ref[0]
        h_ref[...] = h0_ref[0]

    xp_ref[8:8 + cl, :] = xbc_ref[...]
    cw = cw_ref[...]
    yc = cb_ref[...]
    for j in range(CONV_W):
        yc = yc + xp_ref[5 + j:5 + j + cl, :] * cw[j:j + 1, :]
    act = yc * _sigmoid(yc)
    xp_ref[5:8, :] = xbc_ref[cl - 3:cl, :]

    xs = act[:, :D_SSD]
    bm = act[:, D_SSD:D_SSD + SSD_GROUPS * D_STATE]
    cm = act[:, D_SSD + SSD_GROUPS * D_STATE:]
    xb = xs.astype(BF16)
    cb16 = cm.astype(BF16)
    bb16 = bm.astype(BF16)

    dtr = misc_ref[:, MISC_DT:MISC_DT + SSD_HEADS] + dtb_ref[...]
    dt = jnp.maximum(dtr, 0.0) + jnp.log1p(jnp.exp(-jnp.abs(dtr)))
    if t_valid < cl:
        dt = jnp.where(lax.broadcasted_iota(I32, (cl, SSD_HEADS), 0) < t_valid, dt, 0.0)
    a_neg = -jnp.exp(alog_ref[...])
    a = dt * a_neg
    ri = lax.broadcasted_iota(I32, (cl, cl), 0)
    ci = lax.broadcasted_iota(I32, (cl, cl), 1)
    tril = ci <= ri
    acs = jnp.dot(jnp.where(tril, 1.0, 0.0), a, precision=HIGHEST, preferred_element_type=F32)
    eye = jnp.where(lax.broadcasted_iota(I32, (SSD_HEADS, SSD_HEADS), 0)
                    == lax.broadcasted_iota(I32, (SSD_HEADS, SSD_HEADS), 1), 1.0, 0.0)
    dt_t = lax.dot_general(eye, dt, NT, precision=HIGHEST, preferred_element_type=F32)
    acs_t = lax.dot_general(eye, acs, NT, precision=HIGHEST, preferred_element_type=F32)
    acs_last = acs[cl - 1:cl, :]
    w_end = jnp.exp(acs_last - acs) * dt
    e_acs = jnp.exp(acs)
    c_dec = jnp.exp(acs_last)
    dsk = dsk_ref[...]

    for g in range(SSD_GROUPS):
        gs = slice(g * D_STATE, (g + 1) * D_STATE)
        cg = cb16[:, gs]
        cbm = lax.dot_general(cg, bb16[:, gs], NT, preferred_element_type=F32)
        for r in range(SSD_HEADS // SSD_GROUPS):
            h = g * (SSD_HEADS // SSD_GROUPS) + r
            hs = slice(h * SSD_HEAD_DIM, (h + 1) * SSD_HEAD_DIM)
            seg = acs[:, h:h + 1] - acs_t[h:h + 1, :]
            decay = jnp.exp(jnp.where(tril, seg, -jnp.inf))
            sc = cbm * decay * dt_t[h:h + 1, :]
            xh = xs[:, hs]
            xhb = xb[:, hs]
            y_diag = jnp.dot(sc.astype(BF16), xhb, preferred_element_type=F32)
            bw = (bm[:, gs] * w_end[:, h:h + 1]).astype(BF16)
            st = lax.dot_general(xhb, bw, TN, preferred_element_type=F32)
            h_in = h_ref[h]
            y_off = lax.dot_general(cg, h_in.astype(BF16), NT, preferred_element_type=F32) * e_acs[:, h:h + 1]
            h_ref[h] = h_in * c_dec[:, h:h + 1] + st
            yb_ref[:, hs] = (y_diag + y_off) + dsk[:, h:h + 1] * xh

    zz = z_ref[...]
    y = yb_ref[...] * (zz * _sigmoid(zz))
    gw = D_SSD // SSD_GROUPS
    for g in range(SSD_GROUPS):
        yg = y[:, g * gw:(g + 1) * gw]
        yg = yg * lax.rsqrt(jnp.mean(yg * yg, axis=-1, keepdims=True) + EPS)
        y_ref[:, g * gw:(g + 1) * gw] = yg * nw_ref[:, g * gw:(g + 1) * gw]

    @pl.when(c == nc - 1)
    def _():
        hl_ref[0] = h_ref[...]


def _ssd(z, xbc, misc, conv_state, h0, conv_w, conv_b, dt_bias, a_log, d_skip, norm_w, *, b, nc, cl, t_valid):
    m = z.shape[0]
    row = lambda bi, ci: (bi * nc + ci, 0)
    const2 = lambda bi, ci: (0, 0)
    body = functools.partial(_ssd_body, cl=cl, t_valid=t_valid, nc=nc)
    return pl.pallas_call(
        body,
        grid=(b, nc),
        in_specs=[
            pl.BlockSpec((cl, D_SSD), row),
            pl.BlockSpec((cl, CONV_DIM), row),
            pl.BlockSpec((cl, LANE), row),
            pl.BlockSpec((1, CONV_W - 1, CONV_DIM), lambda bi, ci: (bi, 0, 0)),
            pl.BlockSpec((1, SSD_HEADS, SSD_HEAD_DIM, D_STATE), lambda bi, ci: (bi, 0, 0, 0)),
            pl.BlockSpec((CONV_W, CONV_DIM), const2),
            pl.BlockSpec((1, CONV_DIM), const2),
            pl.BlockSpec((1, SSD_HEADS), const2),
            pl.BlockSpec((1, SSD_HEADS), const2),
            pl.BlockSpec((1, SSD_HEADS), const2),
            pl.BlockSpec((1, D_SSD), const2),
        ],
        out_specs=[
            pl.BlockSpec((cl, D_SSD), row),
            pl.BlockSpec((1, SSD_HEADS, SSD_HEAD_DIM, D_STATE), lambda bi, ci: (bi, 0, 0, 0)),
        ],
        out_shape=[
            jax.ShapeDtypeStruct((m, D_SSD), F32),
            jax.ShapeDtypeStruct((b, SSD_HEADS, SSD_HEAD_DIM, D_STATE), F32),
        ],
        scratch_shapes=[
            pltpu.VMEM((SUBLANE + cl, CONV_DIM), F32),
            pltpu.VMEM((SSD_HEADS, SSD_HEAD_DIM, D_STATE), F32),
            pltpu.VMEM((cl, D_SSD), F32),
        ],
        compiler_params=_params(("parallel", "arbitrary")),
        name="ssd",
    )(z, xbc, misc, conv_state, h0, conv_w, conv_b.reshape(1, -1), dt_bias.reshape(1, -1),
      a_log.reshape(1, -1), d_skip.reshape(1, -1), norm_w.reshape(1, -1))


def _rel_bucket(dist):
    n = jnp.maximum(dist, 0)
    max_exact = N_BUCKETS // 2
    nf = jnp.maximum(n, 1).astype(F32)
    log_part = jnp.log(nf / max_exact) / math.log(MAX_DISTANCE / max_exact) * (N_BUCKETS - max_exact)
    large = jnp.minimum(max_exact + log_part.astype(I32), N_BUCKETS - 1)
    return jnp.where(n < max_exact, n, large)


def _attn_prompt_body(q_ref, qi_ref, mq_ref, k_ref, v_ref, mk_ref, bias_ref, o_ref,
                      keys_ref, hi_ref, lo_ref, qit_ref, qbd_ref, tri_ref, acc_ref, *, tq, kc, sc_rows, topk):
    i = pl.program_id(1)
    nch = (i * tq + tq + kc - 1) // kc
    nsel = (i * tq + tq + sc_rows - 1) // sc_rows
    nsub = kc // tq
    n_far = jnp.maximum((i - 1) // nsub, 0)
    qpos = i * tq + lax.broadcasted_iota(I32, (1, tq), 1)

    qi_t = qi_ref[...].T
    qit_ref[...] = jnp.concatenate(
        [qi_t[h * IDX_DIM:(h + 1) * IDX_DIM, :] for h in range(IDX_HEADS)], axis=1).astype(BF16)
    w_rows = mq_ref[...].T[MISC_W:MISC_W + IDX_HEADS, :] * (IDX_DIM ** -0.5 * IDX_HEADS ** -0.5)
    q_t = q_ref[...].T * (HEAD_DIM ** -0.5 * LOG2E)
    zeros = jnp.zeros((HEAD_DIM, tq), F32)
    cols = []
    for g in range(N_KV_HEADS):
        for r in range(REP):
            h = g * REP + r
            blk = q_t[h * HEAD_DIM:(h + 1) * HEAD_DIM, :]
            cols.append(jnp.concatenate([blk, zeros] if g == 0 else [zeros, blk], axis=0))
    qbd_ref[...] = jnp.concatenate(cols, axis=1).astype(BF16)
    tri_ref[...] = jnp.where(lax.broadcasted_iota(I32, (kc, kc), 1) < lax.broadcasted_iota(I32, (kc, kc), 0),
                             1.0, 0.0).astype(BF16)

    def idx_chunk(c, carry):
        k0 = pl.multiple_of(c * kc, kc)
        kidx = mk_ref[pl.ds(k0, kc), 0:IDX_DIM].astype(BF16)
        s = jnp.dot(kidx, qit_ref[...], preferred_element_type=F32)
        sc = jnp.maximum(s[:, 0:tq], 0.0) * w_rows[0:1, :]
        for h in range(1, IDX_HEADS):
            sc = sc + jnp.maximum(s[:, h * tq:(h + 1) * tq], 0.0) * w_rows[h:h + 1, :]
        kpos = k0 + lax.broadcasted_iota(I32, (kc, 1), 0)
        sc = jnp.where(kpos <= qpos, sc, -jnp.inf)
        key = _sort_key(sc)
        keys_ref[pl.ds(k0, kc), :] = key
        hi_ref[pl.ds(k0, kc), :] = (key >> 16).astype(I16)
        lo_ref[pl.ds(k0, kc), :] = ((key & 0xFFFF) + I16_MIN).astype(I16)
        return carry

    _paired_loop(idx_chunk, 0, nch, 0)

    def fill_chunk(c, carry):
        rows = pl.ds(pl.multiple_of(c * kc, kc), kc)
        hi_ref[rows, :] = jnp.full((kc, tq), I16_MIN, I16)
        lo_ref[rows, :] = jnp.full((kc, tq), I16_MIN, I16)
        return carry

    lax.fori_loop(nch, nsel * (sc_rows // kc), fill_chunk, 0)

    acc_rows = 8 * PACKED_SUBLANE
    n_part = sc_rows // acc_rows

    def count_ge(src_ref, cand):
        c16 = cand.astype(I16)

        def ch(c, part):
            k0 = pl.multiple_of(c * sc_rows, sc_rows)
            hit = jnp.where(src_ref[pl.ds(k0, sc_rows), :] >= c16, jnp.bfloat16(1), jnp.bfloat16(0))
            hit = hit.reshape(n_part, acc_rows, tq)
            for u in range(n_part):
                part = part + hit[u]
            return part

        part = lax.fori_loop(0, nsel, ch, jnp.zeros((acc_rows, tq), BF16))
        return jnp.sum(part.astype(F32), axis=0, keepdims=True)

    def count_gt(src_ref, t):
        return jnp.where(t >= I16_MAX, 0.0, count_ge(src_ref, jnp.minimum(t + 1, I16_MAX)))

    def search(src_ref, k_need):
        def bit_pass(bi, t):
            cand = t + lax.shift_left(jnp.int32(1), 15 - bi)
            return jnp.where(count_ge(src_ref, cand) >= k_need, cand, t)
        return lax.fori_loop(0, 16, bit_pass, jnp.full((1, tq), I16_MIN, I32))

    thr_hi = search(hi_ref, topk)
    k_low = topk - count_gt(hi_ref, thr_hi)
    thr_hi16 = thr_hi.astype(I16)

    def keep_bucket(c, carry):
        rows = pl.ds(pl.multiple_of(c * sc_rows, sc_rows), sc_rows)
        lo_ref[rows, :] = jnp.where(hi_ref[rows, :] == thr_hi16, lo_ref[rows, :], jnp.int16(I16_MIN))
        return carry

    lax.fori_loop(0, nsel, keep_bucket, 0)
    thr_lo = search(lo_ref, k_low)
    need = k_low - count_gt(lo_ref, thr_lo)
    thr = thr_hi * 65536 + (thr_lo - I16_MIN)

    acc_ref[...] = jnp.zeros(acc_ref.shape, F32)

    def att_logits(far, c, ties_before):
        k0 = pl.multiple_of(c * kc, kc)
        kk = keys_ref[pl.ds(k0, kc), :]
        eq = kk == thr
        eqf = jnp.where(eq, 1.0, 0.0)
        rank = jnp.dot(tri_ref[...], eqf.astype(BF16), preferred_element_type=F32) + ties_before
        sel = (kk > thr) | (eq & (rank < need))
        if not far:
            sel = sel & (k0 + lax.broadcasted_iota(I32, (kc, 1), 0) <= qpos)
        kb = k_ref[pl.ds(k0, kc), :].astype(BF16)
        logits = jnp.dot(kb, qbd_ref[...], preferred_element_type=F32)
        return ties_before + jnp.sum(eqf, axis=0, keepdims=True), sel, logits

    def att_softmax(far, c, sel, logits, ms, ls):
        k0 = pl.multiple_of(c * kc, kc)
        ms_new, ls_new = [], []
        for g in range(N_KV_HEADS):
            ps, alphas = [], []
            for r in range(REP):
                h = g * REP + r
                cs = slice(h * tq, (h + 1) * tq)
                if far:
                    shift = bias_ref[h, 2, 0:1, :]
                    lg = jnp.where(sel, logits[:, cs], NEG)
                    m_new = jnp.maximum(ms[h], jnp.max(lg, axis=0, keepdims=True) + shift)
                    p = jnp.exp2(lg - (m_new - shift))
                else:
                    bias = jnp.concatenate(
                        [bias_ref[h, jnp.clip(i - (c * nsub + s), 0, 2)] for s in range(nsub)], axis=0)
                    lg = jnp.where(sel, logits[:, cs] + bias, NEG)
                    m_new = jnp.maximum(ms[h], jnp.max(lg, axis=0, keepdims=True))
                    p = jnp.exp2(lg - m_new)
                alpha = jnp.exp2(ms[h] - m_new)
                ms_new.append(m_new)
                ps.append(p.astype(BF16))
                alphas.append(alpha)
            vb = v_ref[pl.ds(k0, kc), g * 2 * HEAD_DIM:(g + 1) * 2 * HEAD_DIM]
            pv = lax.dot_general(vb, jnp.concatenate(ps, axis=1), TN, preferred_element_type=F32)
            acc_ref[g] = jnp.concatenate(alphas, axis=1) * acc_ref[g] + pv[0:HEAD_DIM]
            for r in range(REP):
                ls_new.append(alphas[r] * ls[g * REP + r] + pv[HEAD_DIM:HEAD_DIM + 1, r * tq:(r + 1) * tq])
        return tuple(ms_new), tuple(ls_new)

    def att_trips(far, cs, carry):
        ties, ms, ls = carry
        staged = []
        for c in cs:
            ties, sel, logits = att_logits(far, c, ties)
            staged.append((c, sel, logits))
        for c, sel, logits in staged:
            ms, ls = att_softmax(far, c, sel, logits, ms, ls)
        return ties, ms, ls

    def att_loop(far, lo, hi, carry):
        odd = lax.rem(hi - lo, 2)
        carry = lax.fori_loop(lo, lo + odd, lambda c, cr: att_trips(far, [c], cr), carry)
        first = lo + odd
        return lax.fori_loop(0, (hi - first) // 2,
                             lambda j, cr: att_trips(far, [first + 2 * j, first + 2 * j + 1], cr), carry)

    carry = (jnp.zeros((1, tq), F32),
             tuple(jnp.full((1, tq), NEG, F32) for _ in range(N_HEADS)),
             tuple(jnp.zeros((1, tq), F32) for _ in range(N_HEADS)))
    carry = att_loop(True, 0, n_far, carry)
    _, _, ls = att_loop(False, n_far, nch, carry)

    blocks = []
    for g in range(N_KV_HEADS):
        for r in range(REP):
            blocks.append(acc_ref[g][:, r * tq:(r + 1) * tq] * (1.0 / ls[g * REP + r]))
    o_ref[...] = jnp.concatenate(blocks, axis=0).T


def _attn_prompt(q, qidx, misc, k, vaug, bias_t, *, b, t, tq, kc):
    m = q.shape[0]
    nq = t // tq
    topk = min(TOPK_MAX, t // 4)
    qrow = lambda bi, qi: (bi * nq + qi, 0)
    brow = lambda bi, qi: (bi, 0)
    sc_rows = _pick(t, (1024, 512, 256))
    body = functools.partial(_attn_prompt_body, tq=tq, kc=kc, sc_rows=sc_rows, topk=topk)
    return pl.pallas_call(
        body,
        grid=(b, nq),
        in_specs=[
            pl.BlockSpec((tq, N_HEADS * HEAD_DIM), qrow),
            pl.BlockSpec((tq, IDX_HEADS * IDX_DIM), qrow),
            pl.BlockSpec((tq, LANE), qrow),
            pl.BlockSpec((t, LANE), brow),
            pl.BlockSpec((t, VAUG_W), brow),
            pl.BlockSpec((t, LANE), brow),
            pl.BlockSpec((N_HEADS, 3, tq, tq), lambda bi, qi: (0, 0, 0, 0)),
        ],
        out_specs=pl.BlockSpec((tq, N_HEADS * HEAD_DIM), qrow),
        out_shape=jax.ShapeDtypeStruct((m, N_HEADS * HEAD_DIM), F32),
        scratch_shapes=[
            pltpu.VMEM((t, tq), I32),
            pltpu.VMEM((t, tq), I16),
            pltpu.VMEM((t, tq), I16),
            pltpu.VMEM((IDX_DIM, IDX_HEADS * tq), BF16),
            pltpu.VMEM((N_KV_HEADS * HEAD_DIM, N_HEADS * tq), BF16),
            pltpu.VMEM((kc, kc), BF16),
            pltpu.VMEM((N_KV_HEADS, HEAD_DIM, REP * tq), F32),
        ],
        compiler_params=_params(("parallel", "arbitrary")),
        name="attn_prompt",
    )(q, qidx, misc, k, vaug, misc, bias_t)


def _attn_sample_body(pt_ref, qi_ref, w_ref, qg_ref, bias_ref, kinew_ref, knew_ref, vnew_ref, *rest,
                      pps, n_pages, past, topk):
    kidx_refs = rest[0:pps]
    k_refs = rest[pps:2 * pps]
    v_refs = rest[2 * pps:3 * pps]
    o_ref = rest[3 * pps]
    keys_ref, thr_ref, need_ref, ties_ref, m_ref, l_ref, acc_ref = rest[3 * pps + 1:]
    del pt_ref
    j = pl.program_id(1)
    ng = n_pages // pps
    tp = SUBLANE
    grows = REP * tp
    qpos = past + lax.broadcasted_iota(I32, (tp, 1), 0)
    lane = lax.broadcasted_iota(I32, (1, PAGE_SIZE), 1)

    def score_page(kidx_t, slot, kpos0):
        s = jnp.dot(qi_ref[0].astype(BF16), kidx_t.astype(BF16), preferred_element_type=F32)
        w = w_ref[0] * (IDX_DIM ** -0.5 * IDX_HEADS ** -0.5)
        sc = jnp.maximum(s[0:tp], 0.0) * w[:, 0:1]
        for h in range(1, IDX_HEADS):
            sc = sc + jnp.maximum(s[h * tp:(h + 1) * tp], 0.0) * w[:, h:h + 1]
        sc = jnp.where(kpos0 + lane <= qpos, sc, -jnp.inf)
        keys_ref[slot] = _sort_key(sc)

    @pl.when(j < ng)
    def _():
        for i in range(pps):
            page = j * pps + i
            score_page(kidx_refs[i][0], page, page * PAGE_SIZE)

    @pl.when(j == ng - 1)
    def _():
        score_page(kinew_ref[0], n_pages, past)
        for s in range(n_pages + 1, n_pages + pps):
            keys_ref[s] = jnp.full((tp, PAGE_SIZE), INT_MIN, I32)

        n_acc = 8
        n_slots = keys_ref.shape[0]

        def count(pred_fn):
            hit = jnp.where(pred_fn(keys_ref[...]), 1.0, 0.0)
            part = jnp.sum(hit.reshape(n_slots // n_acc, n_acc, tp, PAGE_SIZE), axis=0)
            return jnp.sum(jnp.sum(part, axis=0), axis=1, keepdims=True)

        def bit_pass(bi, thr):
            cand = thr + lax.shift_left(jnp.int32(1), 31 - bi)
            return jnp.where(count(lambda kk: kk >= cand) >= topk, cand, thr)

        thr = lax.fori_loop(0, 32, bit_pass, jnp.full((tp, 1), INT_MIN, I32))
        thr_ref[...] = thr
        need_ref[...] = topk - count(lambda kk: kk > thr)
        ties_ref[...] = jnp.zeros(ties_ref.shape, F32)
        m_ref[...] = jnp.full(m_ref.shape, NEG, F32)
        l_ref[...] = jnp.zeros(l_ref.shape, F32)
        acc_ref[...] = jnp.zeros(acc_ref.shape, F32)

    def attend(pages):
        thr = thr_ref[...]
        need = need_ref[...]
        tri = jnp.where(lax.broadcasted_iota(I32, (PAGE_SIZE, PAGE_SIZE), 0)
                        < lax.broadcasted_iota(I32, (PAGE_SIZE, PAGE_SIZE), 1), 1.0, 0.0).astype(BF16)
        qg = (qg_ref[0] * (HEAD_DIM ** -0.5 * LOG2E)).astype(BF16)
        ties = ties_ref[...]
        lgs, vbs = [], []
        for kp, vp, slot, kpos0, bias in pages:
            kk = keys_ref[slot]
            eq = kk == thr
            eqf = jnp.where(eq, 1.0, 0.0)
            rank = jnp.dot(eqf.astype(BF16), tri, preferred_element_type=F32) + ties
            sel = ((kk > thr) | (eq & (rank < need))) & (kpos0 + lane <= qpos)
            ties = ties + jnp.sum(eqf, axis=1, keepdims=True)
            self = jnp.where(sel, 1.0, 0.0)
            sel_rows = jnp.concatenate([self] * N_HEADS, axis=0) > 0.5
            kb = kp.astype(BF16)
            lg = jnp.concatenate([jnp.dot(qg[g], kb[g], preferred_element_type=F32) for g in range(N_KV_HEADS)],
                                 axis=0)
            lgs.append(jnp.where(sel_rows, lg + bias, NEG))
            vbs.append(vp.astype(BF16))
        ties_ref[...] = ties
        lg = jnp.concatenate(lgs, axis=1)
        m_old = m_ref[...]
        m_new = jnp.maximum(m_old, jnp.max(lg, axis=1, keepdims=True))
        p = jnp.exp2(lg - m_new)
        alpha = jnp.exp2(m_old - m_new)
        l_ref[...] = alpha * l_ref[...] + jnp.sum(p, axis=1, keepdims=True)
        m_ref[...] = m_new
        pb = p.astype(BF16)
        pvs = []
        for g in range(N_KV_HEADS):
            pg = pb[g * grows:(g + 1) * grows]
            pv = None
            for n in range(len(pages)):
                d = lax.dot_general(pg[:, n * PAGE_SIZE:(n + 1) * PAGE_SIZE], vbs[n][g], NT,
                                    preferred_element_type=F32)
                pv = d if pv is None else pv + d
            pvs.append(pv)
        acc_ref[...] = alpha * acc_ref[...] + jnp.concatenate(pvs, axis=0)

    @pl.when(j >= ng)
    def _():
        pages = []
        for i in range(pps):
            page = (j - ng) * pps + i
            bias = jnp.where(page == n_pages - 1, bias_ref[1], bias_ref[0])
            pages.append((k_refs[i][0], v_refs[i][0], page, page * PAGE_SIZE, bias))
        attend(pages)

    @pl.when(j == 2 * ng - 1)
    def _():
        attend([(knew_ref[0], vnew_ref[0], n_pages, past, bias_ref[2])])
        o_ref[0] = acc_ref[...] / l_ref[...]


def _attn_sample(page_table, qi4, w8, qg, bias_s, kidx_new_t, k_new_t, v_new_t, pool_kidx_t, pool_k_t, pool_v_t,
                 *, pps):
    b, n_pages = page_table.shape
    past = n_pages * PAGE_SIZE
    t_new = 4
    topk = min(TOPK_MAX, (past + t_new) // 4)
    ng = n_pages // pps
    tp = SUBLANE
    rows = N_HEADS * tp
    per_b = lambda bi, j, pt: (bi, 0, 0)
    per_b4 = lambda bi, j, pt: (bi, 0, 0, 0)

    def kidx_map(i):
        return lambda bi, j, pt: (pt[bi * n_pages + jnp.minimum(j, ng - 1) * pps + i], 0, 0)

    def kv_map(i):
        return lambda bi, j, pt: (pt[bi * n_pages + jnp.maximum(j - ng, 0) * pps + i], 0, 0, 0)

    kv_block = (1, N_KV_HEADS, HEAD_DIM, PAGE_SIZE)
    in_specs = [
        pl.BlockSpec((1, IDX_HEADS * tp, IDX_DIM), per_b),
        pl.BlockSpec((1, tp, LANE), per_b),
        pl.BlockSpec((1, N_KV_HEADS, REP * tp, HEAD_DIM), per_b4),
        pl.BlockSpec((3, rows, PAGE_SIZE), lambda bi, j, pt: (0, 0, 0)),
        pl.BlockSpec((1, IDX_DIM, PAGE_SIZE), per_b),
        pl.BlockSpec(kv_block, per_b4),
        pl.BlockSpec(kv_block, per_b4),
    ]
    in_specs += [pl.BlockSpec((1, IDX_DIM, PAGE_SIZE), kidx_map(i)) for i in range(pps)]
    in_specs += [pl.BlockSpec(kv_block, kv_map(i)) for i in range(pps)]
    in_specs += [pl.BlockSpec(kv_block, kv_map(i)) for i in range(pps)]
    body = functools.partial(_attn_sample_body, pps=pps, n_pages=n_pages, past=past, topk=topk)
    grid_spec = pltpu.PrefetchScalarGridSpec(
        num_scalar_prefetch=1,
        grid=(b, 2 * ng),
        in_specs=in_specs,
        out_specs=pl.BlockSpec((1, rows, HEAD_DIM), per_b),
        scratch_shapes=[
            pltpu.VMEM((n_pages + pps, tp, PAGE_SIZE), I32),
            pltpu.VMEM((tp, 1), I32),
            pltpu.VMEM((tp, 1), F32),
            pltpu.VMEM((tp, 1), F32),
            pltpu.VMEM((rows, 1), F32),
            pltpu.VMEM((rows, 1), F32),
            pltpu.VMEM((rows, HEAD_DIM), F32),
        ],
    )
    return pl.pallas_call(
        body,
        grid_spec=grid_spec,
        out_shape=jax.ShapeDtypeStruct((b, rows, HEAD_DIM), F32),
        compiler_params=_params(("parallel", "arbitrary")),
        name="attn_sample",
    )(page_table.reshape(-1), qi4, w8, qg, bias_s, kidx_new_t, k_new_t, v_new_t,
      *([pool_kidx_t] * pps), *([pool_k_t] * pps), *([pool_v_t] * pps))


def _outproj_body(h_ref, a_ref, s_ref, wo_ref, n2_ref, wr_ref, br_ref, h1_ref, xn_ref, ri_ref, rw_ref):
    d_att = a_ref.shape[1]
    mix = (jnp.dot(a_ref[...].astype(BF16), wo_ref[0:d_att, :], preferred_element_type=F32)
           + jnp.dot(s_ref[...].astype(BF16), wo_ref[d_att:, :], preferred_element_type=F32))
    h1 = h_ref[...] + mix
    h1_ref[...] = h1
    ms = jnp.mean(h1 * h1, axis=-1, keepdims=True)
    xn = (h1 * lax.rsqrt(ms + EPS)) * n2_ref[...]
    xn_ref[...] = xn
    logits = jnp.dot(xn.astype(BF16), wr_ref[...], preferred_element_type=F32) + br_ref[...]
    tm = logits.shape[0]
    lane = lax.broadcasted_iota(I32, (tm, LANE), 1).astype(F32)
    ninf = -jnp.inf
    gl = jnp.where(lane < N_EXPERT_GROUPS, logits, ninf)
    gmax = jnp.max(gl, axis=-1, keepdims=True)
    grp = jnp.min(jnp.where(gl == gmax, lane, float(LANE)), axis=-1, keepdims=True)
    g_w = 1.0 / jnp.sum(jnp.exp(gl - gmax), axis=-1, keepdims=True)
    lo = N_EXPERT_GROUPS + grp * EXPERTS_PER_GROUP
    el = jnp.where((lane >= lo) & (lane < lo + EXPERTS_PER_GROUP), logits, ninf)
    v1 = jnp.max(el, axis=-1, keepdims=True)
    i1 = jnp.min(jnp.where(el == v1, lane, float(LANE)), axis=-1, keepdims=True)
    el2 = jnp.where(lane == i1, ninf, el)
    v2 = jnp.max(el2, axis=-1, keepdims=True)
    i2 = jnp.min(jnp.where(el2 == v2, lane, float(LANE)), axis=-1, keepdims=True)
    e2 = jnp.exp(v2 - v1)
    den = 1.0 + e2
    ids = jnp.where(lane == 0.0, i1 - N_EXPERT_GROUPS, jnp.where(lane == 1.0, i2 - N_EXPERT_GROUPS, 0.0))
    ri_ref[...] = ids.astype(I32)
    rw_ref[...] = jnp.where(lane == 0.0, g_w * (1.0 / den), jnp.where(lane == 1.0, g_w * (e2 / den), 0.0))


def _outproj(h, attn, ssd, w_out, norm2_w, w_router, b_router, tm):
    m, d = h.shape
    row = lambda i: (i, 0)
    const = lambda i: (0, 0)
    return pl.pallas_call(
        _outproj_body,
        grid=(m // tm,),
        in_specs=[
            pl.BlockSpec((tm, d), row),
            pl.BlockSpec((tm, attn.shape[1]), row),
            pl.BlockSpec((tm, ssd.shape[1]), row),
            pl.BlockSpec(w_out.shape, const),
            pl.BlockSpec((1, d), const),
            pl.BlockSpec((d, LANE), const),
            pl.BlockSpec((1, LANE), const),
        ],
        out_specs=[
            pl.BlockSpec((tm, d), row),
            pl.BlockSpec((tm, d), row),
            pl.BlockSpec((tm, LANE), row),
            pl.BlockSpec((tm, LANE), row),
        ],
        out_shape=[
            jax.ShapeDtypeStruct((m, d), F32),
            jax.ShapeDtypeStruct((m, d), F32),
            jax.ShapeDtypeStruct((m, LANE), I32),
            jax.ShapeDtypeStruct((m, LANE), F32),
        ],
        compiler_params=_params(("parallel",)),
        name="outproj_router",
    )(h, attn, ssd, w_out, norm2_w.reshape(1, d), w_router, b_router)


def _start_rows(idx_ref, n, src_hbm, dst, sem, idx_of):
    for r in range(n):
        pltpu.make_async_copy(src_hbm.at[pl.ds(idx_of(idx_ref, r), 1), :], dst.at[pl.ds(r, 1), :], sem).start()


def _wait_rows(n, src_hbm, dst, sem):
    pltpu.make_async_copy(src_hbm.at[pl.ds(0, n), :], dst, sem).wait()


def _expert_body(blk_e_ref, tok_ref, tok_next_ref, roww_ref, x_hbm, wg_ref, wu_ref, wd_ref, o_ref, xbuf, sem):
    del blk_e_ref
    i = pl.program_id(0)
    last = pl.num_programs(0) - 1
    slot = lax.rem(i, 2)
    other = 1 - slot
    rb = xbuf.shape[1]
    tok_of = lambda ref, r: ref[0, 0, r]

    @pl.when(i == 0)
    def _():
        _start_rows(tok_ref, rb, x_hbm, xbuf.at[0], sem.at[0], tok_of)

    _start_rows(tok_next_ref, rb, x_hbm, xbuf.at[other], sem.at[other], tok_of)
    _wait_rows(rb, x_hbm, xbuf.at[slot], sem.at[slot])
    xb = xbuf[slot].astype(BF16)
    hg = jnp.dot(xb, wg_ref[0], preferred_element_type=F32)
    hu = jnp.dot(xb, wu_ref[0], preferred_element_type=F32)
    hd = ((hg * _sigmoid(hg)) * hu).astype(BF16)
    y = jnp.dot(hd, wd_ref[0], preferred_element_type=F32)
    o_ref[...] = y * roww_ref[...]

    @pl.when(i == last)
    def _():
        _wait_rows(rb, x_hbm, xbuf.at[other], sem.at[other])


def _experts(blk_e, row_tok, row_w, xn, w_gate, w_up, w_down, rb):
    cap = row_tok.shape[0]
    n_blk = cap // rb
    d = xn.shape[1]
    de = w_gate.shape[2]
    tok3 = row_tok.reshape(n_blk, 1, rb)
    grid_spec = pltpu.PrefetchScalarGridSpec(
        num_scalar_prefetch=1,
        grid=(n_blk,),
        in_specs=[
            pl.BlockSpec((1, 1, rb), lambda i, be: (i, 0, 0), memory_space=pltpu.SMEM),
            pl.BlockSpec((1, 1, rb), lambda i, be: (jnp.minimum(i + 1, n_blk - 1), 0, 0), memory_space=pltpu.SMEM),
            pl.BlockSpec((rb, 1), lambda i, be: (i, 0)),
            pl.BlockSpec(memory_space=pl.ANY),
            pl.BlockSpec((1, d, de), lambda i, be: (be[i], 0, 0)),
            pl.BlockSpec((1, d, de), lambda i, be: (be[i], 0, 0)),
            pl.BlockSpec((1, de, d), lambda i, be: (be[i], 0, 0)),
        ],
        out_specs=pl.BlockSpec((rb, d), lambda i, be: (i, 0)),
        scratch_shapes=[pltpu.VMEM((2, rb, d), F32), pltpu.SemaphoreType.DMA((2,))],
    )
    return pl.pallas_call(
        _expert_body,
        grid_spec=grid_spec,
        out_shape=jax.ShapeDtypeStruct((cap, d), F32),
        compiler_params=_params(("arbitrary",)),
        name="experts",
    )(blk_e, tok3, tok3, row_w.reshape(cap, 1), xn, w_gate, w_up, w_down)


def _combine_body(pos_ref, pos_next_ref, h1_ref, yw_hbm, nf_ref, o_ref, buf, sem):
    i = pl.program_id(0)
    last = pl.num_programs(0) - 1
    slot = lax.rem(i, 2)
    other = 1 - slot
    tm = h1_ref.shape[0]

    def start(ref, s):
        for kk in range(TOP_K_INNER):
            _start_rows(ref, tm, yw_hbm, buf.at[s, kk], sem.at[s, kk],
                        lambda rf, r, kk=kk: rf[0, 0, TOP_K_INNER * r + kk])

    def wait(s):
        for kk in range(TOP_K_INNER):
            _wait_rows(tm, yw_hbm, buf.at[s, kk], sem.at[s, kk])

    @pl.when(i == 0)
    def _():
        start(pos_ref, 0)

    start(pos_next_ref, other)
    wait(slot)
    h = h1_ref[...] + (buf[slot, 0] + buf[slot, 1])
    ms = jnp.mean(h * h, axis=-1, keepdims=True)
    o_ref[...] = (h * lax.rsqrt(ms + EPS)) * nf_ref[...]

    @pl.when(i == last)
    def _():
        wait(other)


def _combine(pos, h1, yw, norm_f_w, tm):
    m, d = h1.shape
    nt = m // tm
    pos3 = pos.reshape(nt, 1, TOP_K_INNER * tm)
    return pl.pallas_call(
        _combine_body,
        grid=(nt,),
        in_specs=[
            pl.BlockSpec((1, 1, TOP_K_INNER * tm), lambda i: (i, 0, 0), memory_space=pltpu.SMEM),
            pl.BlockSpec((1, 1, TOP_K_INNER * tm), lambda i: (jnp.minimum(i + 1, nt - 1), 0, 0),
                         memory_space=pltpu.SMEM),
            pl.BlockSpec((tm, d), lambda i: (i, 0)),
            pl.BlockSpec(memory_space=pl.ANY),
            pl.BlockSpec((1, d), lambda i: (0, 0)),
        ],
        out_specs=pl.BlockSpec((tm, d), lambda i: (i, 0)),
        out_shape=jax.ShapeDtypeStruct((m, d), F32),
        scratch_shapes=[pltpu.VMEM((2, TOP_K_INNER, tm, d), F32), pltpu.SemaphoreType.DMA((2, TOP_K_INNER))],
        compiler_params=_params(("arbitrary",)),
        name="combine",
    )(pos3, pos3, h1, yw, norm_f_w.reshape(1, d))


def _permute_w_in(w_in):
    d = w_in.shape[0]
    c = 0
    parts = {}
    for name, n in (("q", 512), ("k", 128), ("v", 128), ("qi", 256), ("ki", 64), ("wi", 4),
                    ("z", 512), ("xbc", 768), ("dt", 8)):
        parts[name] = w_in[:, c:c + n]
        c += n
    pad = jnp.zeros((d, LANE - 64 - 4 - 8), w_in.dtype)
    return jnp.concatenate([parts["q"], parts["k"], parts["v"], parts["qi"], parts["ki"], parts["wi"],
                            parts["dt"], pad, parts["z"], parts["xbc"]], axis=1).astype(BF16)


def _route_tables(expert, gate, rb):
    a = expert.shape[0] * TOP_K_INNER
    e_flat = expert.reshape(-1)
    pid = jnp.arange(a, dtype=I32)
    _, order, gw_s = lax.sort((e_flat, pid, gate.reshape(-1)), num_keys=1, is_stable=True)
    eids = jnp.arange(N_EXPERTS, dtype=I32)
    sizes = jnp.sum((e_flat[:, None] == eids[None, :]).astype(I32), axis=0)
    gends = jnp.cumsum(sizes)
    padded = (sizes + rb - 1) // rb * rb
    pends = jnp.cumsum(padded)
    gap = padded - sizes
    cap = -(-(a + N_EXPERTS * (rb - 1)) // rb) * rb
    n_blk = cap // rb
    dest = pid + jnp.sum(jnp.where(gends[None, :] <= pid[:, None], gap[None, :], 0), axis=1)
    r = jnp.arange(cap, dtype=I32)
    done = pends[None, :] <= r[:, None]
    src = r - jnp.sum(jnp.where(done, gap[None, :], 0), axis=1)
    e_r = jnp.minimum(jnp.sum(done.astype(I32), axis=1), N_EXPERTS - 1)
    valid = src < jnp.sum(jnp.where(eids[None, :] == e_r[:, None], gends[None, :], 0), axis=1)
    src = jnp.minimum(src, a - 1)
    row_tok = jnp.where(valid, order[src] // TOP_K_INNER, 0)
    row_w = jnp.where(valid, gw_s[src], 0.0)
    _, pos = lax.sort((order, dest), num_keys=1)
    blk_e = jnp.minimum(jnp.sum((pends[None, :] <= (jnp.arange(n_blk, dtype=I32) * rb)[:, None]).astype(I32), axis=1),
                        N_EXPERTS - 1)
    return blk_e, row_tok, row_w, pos


def _pick(n, prefs):
    for p in prefs:
        if n % p == 0:
            return p
    return n


def _moe_and_final(h, attn, ssd, w_out, norm2_w, w_router, b_router, w_gate, w_up, w_down, norm_f_w):
    m = h.shape[0]
    tm = _pick(m, (512, 256, 128))
    h1, xn, ri, rw = _outproj(h, attn, ssd, w_out, norm2_w, w_router, b_router, tm)
    rb = 256 if m >= 4096 else 128
    blk_e, row_tok, row_w, pos = _route_tables(ri[:, :TOP_K_INNER], rw[:, :TOP_K_INNER], rb)
    yw = _experts(blk_e, row_tok, row_w, xn, w_gate, w_up, w_down, rb)
    return _combine(pos, h1, yw, norm_f_w, _pick(m, (256, 128)))


def kernel(x_prompt, x_sample, cache_k, cache_v, cache_kidx, state_conv, state_ssm, page_table, rel_bias,
           norm1_w, w_in, conv_w, conv_b, dt_bias, a_log, d_skip, ssd_norm_w, w_out, norm2_w,
           w_router_group, b_router_group, w_router_expert, b_router_expert, w_gate, w_up, w_down, norm_f_w):
    bp, tp_len, d = x_prompt.shape
    bs, ts, _ = x_sample.shape
    depth = w_in.shape[0]
    assert depth == 1 and ts == 4 and tp_len % SSD_CHUNK == 0
    l = 0

    w_perm = _permute_w_in(w_in[l])
    w_out_b = w_out[l].astype(BF16)
    n_r = N_EXPERT_GROUPS + N_EXPERTS
    w_router = jnp.concatenate([w_router_group[l], w_router_expert[l],
                                jnp.zeros((d, LANE - n_r), F32)], axis=1).astype(BF16)
    b_router = jnp.concatenate([b_router_group[l], b_router_expert[l], jnp.zeros((LANE - n_r,), F32)]).reshape(1, LANE)
    wg_b, wu_b, wd_b = w_gate[l].astype(BF16), w_up[l].astype(BF16), w_down[l].astype(BF16)
    rel_t = rel_bias.astype(F32).T

    mp = bp * tp_len
    xp = x_prompt.reshape(mp, d)
    q_p, k_p, v_p, qi_p, misc_p, z_p, xbc_p, vaug_p = _inproj(xp, norm1_w[l], w_perm, _pick(mp, (512, 256, 128)))
    conv0 = jnp.zeros((bp, CONV_W - 1, CONV_DIM), F32)
    ssm0 = jnp.zeros((bp, SSD_HEADS, SSD_HEAD_DIM, D_STATE), F32)
    ssd_p, ssm_p = _ssd(z_p, xbc_p, misc_p, conv0, ssm0, conv_w[l], conv_b[l], dt_bias[l], a_log[l], d_skip[l],
                        ssd_norm_w[l], b=bp, nc=tp_len // SSD_CHUNK, cl=SSD_CHUNK, t_valid=SSD_CHUNK)
    tq = 128
    kc = 256 if tp_len % 256 == 0 else 128
    jj = jnp.arange(tq, dtype=I32)[:, None]
    qq = jnp.arange(tq, dtype=I32)[None, :]
    dist_p = jnp.stack([dd * tq + qq - jj for dd in range(3)])
    bias_p = rel_t[:, _rel_bucket(dist_p)] * LOG2E
    attn_p = _attn_prompt(q_p, qi_p, misc_p, k_p, vaug_p, bias_p, b=bp, t=tp_len, tq=tq, kc=kc)
    y_p = _moe_and_final(xp, attn_p, ssd_p, w_out_b, norm2_w[l], w_router, b_router, wg_b, wu_b, wd_b, norm_f_w)

    ms = bs * ts
    xs = x_sample.reshape(ms, d)
    q_s, k_s, v_s, qi_s, misc_s, z_s, xbc_s, _ = _inproj(xs, norm1_w[l], w_perm, _pick(ms, (512, 256, 128)))
    tpad = SUBLANE

    def pad_t(a):
        n = a.shape[1]
        return jnp.pad(a.reshape(bs, ts, n), ((0, 0), (0, tpad - ts), (0, 0))).reshape(bs * tpad, n)

    ssd_s8, ssm_s = _ssd(pad_t(z_s), pad_t(xbc_s), pad_t(misc_s), state_conv[l], state_ssm[l], conv_w[l], conv_b[l],
                         dt_bias[l], a_log[l], d_skip[l], ssd_norm_w[l], b=bs, nc=1, cl=tpad, t_valid=ts)
    ssd_s = ssd_s8.reshape(bs, tpad, D_SSD)[:, :ts].reshape(ms, D_SSD)

    n_pages = page_table.shape[1]
    past = n_pages * PAGE_SIZE
    padq = ((0, 0), (0, 0), (0, tpad - ts), (0, 0))
    qi4 = jnp.pad(qi_s.reshape(bs, ts, IDX_HEADS, IDX_DIM).transpose(0, 2, 1, 3), padq)
    qi4 = qi4.reshape(bs, IDX_HEADS * tpad, IDX_DIM)
    w8 = jnp.pad(misc_s.reshape(bs, ts, LANE)[:, :, MISC_W:MISC_W + IDX_HEADS],
                 ((0, 0), (0, tpad - ts), (0, LANE - IDX_HEADS)))
    qh = jnp.pad(q_s.reshape(bs, ts, N_HEADS, HEAD_DIM).transpose(0, 2, 1, 3), padq)
    qg = qh.reshape(bs, N_KV_HEADS, REP * tpad, HEAD_DIM)
    tt = jnp.tile(jnp.arange(tpad, dtype=I32), N_HEADS)[:, None]
    hh = jnp.repeat(jnp.arange(N_HEADS, dtype=I32), tpad)[:, None]
    jl = jnp.arange(PAGE_SIZE, dtype=I32)[None, :]
    dist_s = jnp.stack([2 * PAGE_SIZE + tt - jl + PAGE_SIZE, PAGE_SIZE + tt - jl, tt - jl])
    bias_s = rel_t[hh[None], _rel_bucket(dist_s)] * LOG2E

    def new_page_t(a, lead):
        a = a.reshape(bs, ts, lead, HEAD_DIM).transpose(0, 2, 3, 1)
        return jnp.pad(a, ((0, 0), (0, 0), (0, 0), (0, PAGE_SIZE - ts)))

    kidx_new_t = new_page_t(misc_s[:, :IDX_DIM], 1)[:, 0]
    o_s = _attn_sample(page_table, qi4, w8, qg, bias_s, kidx_new_t,
                       new_page_t(k_s, N_KV_HEADS), new_page_t(v_s, N_KV_HEADS),
                       cache_kidx[l].transpose(0, 2, 1), cache_k[l].transpose(0, 2, 3, 1),
                       cache_v[l].transpose(0, 2, 3, 1), pps=_pick(n_pages, (16, 8)))
    o_s = o_s.reshape(bs, N_KV_HEADS, REP, tpad, HEAD_DIM)[:, :, :, :ts]
    attn_s = o_s.transpose(0, 3, 1, 2, 4).reshape(ms, N_HEADS * HEAD_DIM)
    y_s = _moe_and_final(xs, attn_s, ssd_s, w_out_b, norm2_w[l], w_router, b_router, wg_b, wu_b, wd_b, norm_f_w)

    def cache_out(k_, b_, t_):
        return k_.reshape(1, b_, t_, N_KV_HEADS, HEAD_DIM)

    return (
        y_p.reshape(bp, tp_len, d),
        y_s.reshape(bs, ts, d),
        cache_out(k_p, bp, tp_len), cache_out(v_p, bp, tp_len),
        misc_p[:, :IDX_DIM].reshape(1, bp, tp_len, IDX_DIM),
        xbc_p.reshape(bp, tp_len, CONV_DIM)[:, tp_len - (CONV_W - 1):][None],
        ssm_p[None],
        cache_out(k_s, bs, ts), cache_out(v_s, bs, ts),
        misc_s[:, :IDX_DIM].reshape(1, bs, ts, IDX_DIM),
        xbc_s.reshape(bs, ts, CONV_DIM)[:, ts - (CONV_W - 1):][None],
        ssm_s[None],
    )
```

```python
import functools
import math

import jax
import jax.numpy as jnp
from jax import lax
from jax.experimental import pallas as pl
from jax.experimental.pallas import tpu as pltpu

F32 = jnp.float32
BF16 = jnp.bfloat16
I32 = jnp.int32
I16 = jnp.int16
I16_MIN, I16_MAX = -(2 ** 15), 2 ** 15 - 1
HIGHEST = lax.Precision.HIGHEST

HEAD_DIM = 64
N_HEADS = 8
N_KV_HEADS = 2
REP = N_HEADS // N_KV_HEADS
IDX_HEADS = 4
IDX_DIM = 64
TOPK_MAX = 256
N_BUCKETS = 32
MAX_DISTANCE = 128
D_SSD = 512
SSD_HEADS = 8
SSD_HEAD_DIM = 64
SSD_GROUPS = 2
D_STATE = 64
CONV_W = 4
CONV_DIM = D_SSD + 2 * SSD_GROUPS * D_STATE
SSD_CHUNK = 128
N_EXPERT_GROUPS = 4
EXPERTS_PER_GROUP = 8
N_EXPERTS = N_EXPERT_GROUPS * EXPERTS_PER_GROUP
TOP_K_INNER = 2
PAGE_SIZE = 128
EPS = 1e-6

LANE = 128
SUBLANE = 8
PACKED_SUBLANE = 16
VMEM_LIMIT = 56 * 1024 * 1024
NEG = -1e30
INT_MIN = -(2 ** 31)
LOG2E = math.log2(math.e)
DIAG_ONES, DIAG_PAIR_ATT, DIAG_PAIR_IDX = False, False, False

SPLITS = (512, 128, 128, 256, 128, 512, 768)
MISC_W = 64
MISC_DT = 68
VAUG_W = N_KV_HEADS * 2 * HEAD_DIM

NT = (((1,), (1,)), ((), ()))
TN = (((0,), (0,)), ((), ()))


def _params(sem):
    return pltpu.CompilerParams(dimension_semantics=sem, vmem_limit_bytes=VMEM_LIMIT)


def _sigmoid(x):
    return 1.0 / (1.0 + jnp.exp(-x))


def _paired_loop(body, lo, hi, carry):
    odd = lax.rem(hi - lo, 2)
    carry = lax.fori_loop(lo, lo + odd, body, carry)
    first = lo + odd

    def two(j, cr):
        c = first + 2 * j
        return body(c + 1, body(c, cr))

    return lax.fori_loop(0, (hi - first) // 2, two, carry)


def _sort_key(x):
    bits = lax.bitcast_convert_type(x, I32)
    return bits ^ ((bits >> 31) & 0x7FFFFFFF)


def _inproj_body(x_ref, nw_ref, w_ref, *out_refs):
    x = x_ref[...]
    ms = jnp.mean(x * x, axis=-1, keepdims=True)
    xb = ((x * lax.rsqrt(ms + EPS)) * nw_ref[...]).astype(BF16)
    off = 0
    for o_ref, n in zip(out_refs, SPLITS):
        o_ref[...] = jnp.dot(xb, w_ref[:, off:off + n], preferred_element_type=F32)
        off += n
    v = out_refs[2][...]
    ones = jnp.ones((v.shape[0], HEAD_DIM), F32)
    out_refs[len(SPLITS)][...] = jnp.concatenate(
        [piece for g in range(N_KV_HEADS) for piece in (v[:, g * HEAD_DIM:(g + 1) * HEAD_DIM], ones)],
        axis=1).astype(BF16)


def _inproj(x, norm_w, w_perm, tm):
    m, d = x.shape
    n_tot = sum(SPLITS)
    return pl.pallas_call(
        _inproj_body,
        grid=(m // tm,),
        in_specs=[
            pl.BlockSpec((tm, d), lambda i: (i, 0)),
            pl.BlockSpec((1, d), lambda i: (0, 0)),
            pl.BlockSpec((d, n_tot), lambda i: (0, 0)),
        ],
        out_specs=[pl.BlockSpec((tm, n), lambda i: (i, 0)) for n in SPLITS + (VAUG_W,)],
        out_shape=([jax.ShapeDtypeStruct((m, n), F32) for n in SPLITS]
                   + [jax.ShapeDtypeStruct((m, VAUG_W), BF16)]),
        compiler_params=_params(("parallel",)),
        name="inproj",
    )(x, norm_w.reshape(1, d), w_perm)


def _ssd_body(z_ref, xbc_ref, misc_ref, cs_ref, h0_ref, cw_ref, cb_ref, dtb_ref, alog_ref, dsk_ref, nw_ref,
              y_ref, hl_ref, xp_ref, h_ref, yb_ref, *, cl, t_valid, nc):
    c = pl.program_id(1)

    @pl.when(c == 0)
    def _():
        xp_ref[5:8, :] = cs_ref[0]
        h_ref[...] = h0_ref[0]

    xp_ref[8:8 + cl, :] = xbc_ref[...]
    cw = cw_ref[...]
    yc = cb_ref[...]
    for j in range(CONV_W):
        yc = yc + xp_ref[5 + j:5 + j + cl, :] * cw[j:j + 1, :]
    act = yc * _sigmoid(yc)
    xp_ref[5:8, :] = xbc_ref[cl - 3:cl, :]

    xs = act[:, :D_SSD]
    bm = act[:, D_SSD:D_SSD + SSD_GROUPS * D_STATE]
    cm = act[:, D_SSD + SSD_GROUPS * D_STATE:]
    xb = xs.astype(BF16)
    cb16 = cm.astype(BF16)
    bb16 = bm.astype(BF16)

    dtr = misc_ref[:, MISC_DT:MISC_DT + SSD_HEADS] + dtb_ref[...]
    dt = jnp.maximum(dtr, 0.0) + jnp.log1p(jnp.exp(-jnp.abs(dtr)))
    if t_valid < cl:
        dt = jnp.where(lax.broadcasted_iota(I32, (cl, SSD_HEADS), 0) < t_valid, dt, 0.0)
    a_neg = -jnp.exp(alog_ref[...])
    a = dt * a_neg
    ri = lax.broadcasted_iota(I32, (cl, cl), 0)
    ci = lax.broadcasted_iota(I32, (cl, cl), 1)
    tril = ci <= ri
    acs = jnp.dot(jnp.where(tril, 1.0, 0.0), a, precision=HIGHEST, preferred_element_type=F32)
    eye = jnp.where(lax.broadcasted_iota(I32, (SSD_HEADS, SSD_HEADS), 0)
                    == lax.broadcasted_iota(I32, (SSD_HEADS, SSD_HEADS), 1), 1.0, 0.0)
    dt_t = lax.dot_general(eye, dt, NT, precision=HIGHEST, preferred_element_type=F32)
    acs_t = lax.dot_general(eye, acs, NT, precision=HIGHEST, preferred_element_type=F32)
    acs_last = acs[cl - 1:cl, :]
    w_end = jnp.exp(acs_last - acs) * dt
    e_acs = jnp.exp(acs)
    c_dec = jnp.exp(acs_last)
    dsk = dsk_ref[...]

    for g in range(SSD_GROUPS):
        gs = slice(g * D_STATE, (g + 1) * D_STATE)
        cg = cb16[:, gs]
        cbm = lax.dot_general(cg, bb16[:, gs], NT, preferred_element_type=F32)
        for r in range(SSD_HEADS // SSD_GROUPS):
            h = g * (SSD_HEADS // SSD_GROUPS) + r
            hs = slice(h * SSD_HEAD_DIM, (h + 1) * SSD_HEAD_DIM)
            seg = acs[:, h:h + 1] - acs_t[h:h + 1, :]
            decay = jnp.exp(jnp.where(tril, seg, -jnp.inf))
            sc = cbm * decay * dt_t[h:h + 1, :]
            xh = xs[:, hs]
            xhb = xb[:, hs]
            y_diag = jnp.dot(sc.astype(BF16), xhb, preferred_element_type=F32)
            bw = (bm[:, gs] * w_end[:, h:h + 1]).astype(BF16)
            st = lax.dot_general(xhb, bw, TN, preferred_element_type=F32)
            h_in = h_ref[h]
            y_off = lax.dot_general(cg, h_in.astype(BF16), NT, preferred_element_type=F32) * e_acs[:, h:h + 1]
            h_ref[h] = h_in * c_dec[:, h:h + 1] + st
            yb_ref[:, hs] = (y_diag + y_off) + dsk[:, h:h + 1] * xh

    zz = z_ref[...]
    y = yb_ref[...] * (zz * _sigmoid(zz))
    gw = D_SSD // SSD_GROUPS
    for g in range(SSD_GROUPS):
        yg = y[:, g * gw:(g + 1) * gw]
        yg = yg * lax.rsqrt(jnp.mean(yg * yg, axis=-1, keepdims=True) + EPS)
        y_ref[:, g * gw:(g + 1) * gw] = yg * nw_ref[:, g * gw:(g + 1) * gw]

    @pl.when(c == nc - 1)
    def _():
        hl_ref[0] = h_ref[...]


def _ssd(z, xbc, misc, conv_state, h0, conv_w, conv_b, dt_bias, a_log, d_skip, norm_w, *, b, nc, cl, t_valid):
    m = z.shape[0]
    row = lambda bi, ci: (bi * nc + ci, 0)
    const2 = lambda bi, ci: (0, 0)
    body = functools.partial(_ssd_body, cl=cl, t_valid=t_valid, nc=nc)
    return pl.pallas_call(
        body,
        grid=(b, nc),
        in_specs=[
            pl.BlockSpec((cl, D_SSD), row),
            pl.BlockSpec((cl, CONV_DIM), row),
            pl.BlockSpec((cl, LANE), row),
            pl.BlockSpec((1, CONV_W - 1, CONV_DIM), lambda bi, ci: (bi, 0, 0)),
            pl.BlockSpec((1, SSD_HEADS, SSD_HEAD_DIM, D_STATE), lambda bi, ci: (bi, 0, 0, 0)),
            pl.BlockSpec((CONV_W, CONV_DIM), const2),
            pl.BlockSpec((1, CONV_DIM), const2),
            pl.BlockSpec((1, SSD_HEADS), const2),
            pl.BlockSpec((1, SSD_HEADS), const2),
            pl.BlockSpec((1, SSD_HEADS), const2),
            pl.BlockSpec((1, D_SSD), const2),
        ],
        out_specs=[
            pl.BlockSpec((cl, D_SSD), row),
            pl.BlockSpec((1, SSD_HEADS, SSD_HEAD_DIM, D_STATE), lambda bi, ci: (bi, 0, 0, 0)),
        ],
        out_shape=[
            jax.ShapeDtypeStruct((m, D_SSD), F32),
            jax.ShapeDtypeStruct((b, SSD_HEADS, SSD_HEAD_DIM, D_STATE), F32),
        ],
        scratch_shapes=[
            pltpu.VMEM((SUBLANE + cl, CONV_DIM), F32),
            pltpu.VMEM((SSD_HEADS, SSD_HEAD_DIM, D_STATE), F32),
            pltpu.VMEM((cl, D_SSD), F32),
        ],
        compiler_params=_params(("parallel", "arbitrary")),
        name="ssd",
    )(z, xbc, misc, conv_state, h0, conv_w, conv_b.reshape(1, -1), dt_bias.reshape(1, -1),
      a_log.reshape(1, -1), d_skip.reshape(1, -1), norm_w.reshape(1, -1))


def _rel_bucket(dist):
    n = jnp.maximum(dist, 0)
    max_exact = N_BUCKETS // 2
    nf = jnp.maximum(n, 1).astype(F32)
    log_part = jnp.log(nf / max_exact) / math.log(MAX_DISTANCE / max_exact) * (N_BUCKETS - max_exact)
    large = jnp.minimum(max_exact + log_part.astype(I32), N_BUCKETS - 1)
    return jnp.where(n < max_exact, n, large)


def _attn_prompt_body(q_ref, qi_ref, mq_ref, k_ref, v_ref, mk_ref, bias_ref, o_ref,
                      keys_ref, hi_ref, lo_ref, qit_ref, qbd_ref, tri_ref, acc_ref, *, tq, kc, sc_rows, topk):
    i = pl.program_id(1)
    nch = (i * tq + tq + kc - 1) // kc
    nsel = (i * tq + tq + sc_rows - 1) // sc_rows
    nsub = kc // tq
    n_far = jnp.maximum((i - 1) // nsub, 0)
    qpos = i * tq + lax.broadcasted_iota(I32, (1, tq), 1)

    qi_t = qi_ref[...].T
    qit_ref[...] = jnp.concatenate(
        [qi_t[h * IDX_DIM:(h + 1) * IDX_DIM, :] for h in range(IDX_HEADS)], axis=1).astype(BF16)
    w_rows = mq_ref[...].T[MISC_W:MISC_W + IDX_HEADS, :] * (IDX_DIM ** -0.5 * IDX_HEADS ** -0.5)
    q_t = q_ref[...].T * (HEAD_DIM ** -0.5 * LOG2E)
    zeros = jnp.zeros((HEAD_DIM, tq), F32)
    cols = []
    for g in range(N_KV_HEADS):
        for r in range(REP):
            h = g * REP + r
            blk = q_t[h * HEAD_DIM:(h + 1) * HEAD_DIM, :]
            cols.append(jnp.concatenate([blk, zeros] if g == 0 else [zeros, blk], axis=0))
    qbd_ref[...] = jnp.concatenate(cols, axis=1).astype(BF16)
    tri_ref[...] = jnp.where(lax.broadcasted_iota(I32, (kc, kc), 1) < lax.broadcasted_iota(I32, (kc, kc), 0),
                             1.0, 0.0).astype(BF16)

    def idx_chunk(c, carry):
        k0 = pl.multiple_of(c * kc, kc)
        kidx = mk_ref[pl.ds(k0, kc), 0:IDX_DIM].astype(BF16)
        s = jnp.dot(kidx, qit_ref[...], preferred_element_type=F32)
        sc = jnp.maximum(s[:, 0:tq], 0.0) * w_rows[0:1, :]
        for h in range(1, IDX_HEADS):
            sc = sc + jnp.maximum(s[:, h * tq:(h + 1) * tq], 0.0) * w_rows[h:h + 1, :]
        kpos = k0 + lax.broadcasted_iota(I32, (kc, 1), 0)
        sc = jnp.where(kpos <= qpos, sc, -jnp.inf)
        key = _sort_key(sc)
        keys_ref[pl.ds(k0, kc), :] = key
        hi_ref[pl.ds(k0, kc), :] = (key >> 16).astype(I16)
        lo_ref[pl.ds(k0, kc), :] = ((key & 0xFFFF) + I16_MIN).astype(I16)
        return carry

    if DIAG_PAIR_IDX:
        _paired_loop(idx_chunk, 0, nch, 0)
    else:
        lax.fori_loop(0, nch, idx_chunk, 0)

    def fill_chunk(c, carry):
        rows = pl.ds(pl.multiple_of(c * kc, kc), kc)
        hi_ref[rows, :] = jnp.full((kc, tq), I16_MIN, I16)
        lo_ref[rows, :] = jnp.full((kc, tq), I16_MIN, I16)
        return carry

    lax.fori_loop(nch, nsel * (sc_rows // kc), fill_chunk, 0)

    acc_rows = 8 * PACKED_SUBLANE
    n_part = sc_rows // acc_rows

    def count_ge(src_ref, cand):
        c16 = cand.astype(I16)

        def ch(c, part):
            k0 = pl.multiple_of(c * sc_rows, sc_rows)
            hit = jnp.where(src_ref[pl.ds(k0, sc_rows), :] >= c16, jnp.bfloat16(1), jnp.bfloat16(0))
            hit = hit.reshape(n_part, acc_rows, tq)
            for u in range(n_part):
                part = part + hit[u]
            return part

        part = lax.fori_loop(0, nsel, ch, jnp.zeros((acc_rows, tq), BF16))
        return jnp.sum(part.astype(F32), axis=0, keepdims=True)

    def count_gt(src_ref, t):
        return jnp.where(t >= I16_MAX, 0.0, count_ge(src_ref, jnp.minimum(t + 1, I16_MAX)))

    def search(src_ref, k_need):
        def bit_pass(bi, t):
            cand = t + lax.shift_left(jnp.int32(1), 15 - bi)
            return jnp.where(count_ge(src_ref, cand) >= k_need, cand, t)
        return lax.fori_loop(0, 16, bit_pass, jnp.full((1, tq), I16_MIN, I32))

    thr_hi = search(hi_ref, topk)
    k_low = topk - count_gt(hi_ref, thr_hi)
    thr_hi16 = thr_hi.astype(I16)

    def keep_bucket(c, carry):
        rows = pl.ds(pl.multiple_of(c * sc_rows, sc_rows), sc_rows)
        lo_ref[rows, :] = jnp.where(hi_ref[rows, :] == thr_hi16, lo_ref[rows, :], jnp.int16(I16_MIN))
        return carry

    lax.fori_loop(0, nsel, keep_bucket, 0)
    thr_lo = search(lo_ref, k_low)
    need = k_low - count_gt(lo_ref, thr_lo)
    thr = thr_hi * 65536 + (thr_lo - I16_MIN)

    acc_ref[...] = jnp.zeros(acc_ref.shape, F32)

    def att_logits(far, c, ties_before):
        k0 = pl.multiple_of(c * kc, kc)
        kk = keys_ref[pl.ds(k0, kc), :]
        eq = kk == thr
        eqf = jnp.where(eq, 1.0, 0.0)
        rank = jnp.dot(tri_ref[...], eqf.astype(BF16), preferred_element_type=F32) + ties_before
        sel = (kk > thr) | (eq & (rank < need))
        if not far:
            sel = sel & (k0 + lax.broadcasted_iota(I32, (kc, 1), 0) <= qpos)
        kb = k_ref[pl.ds(k0, kc), :].astype(BF16)
        logits = jnp.dot(kb, qbd_ref[...], preferred_element_type=F32)
        return ties_before + jnp.sum(eqf, axis=0, keepdims=True), sel, logits

    def att_softmax(far, c, sel, logits, ms, ls):
        k0 = pl.multiple_of(c * kc, kc)
        ms_new, ls_new = [], []
        for g in range(N_KV_HEADS):
            ps, alphas = [], []
            for r in range(REP):
                h = g * REP + r
                cs = slice(h * tq, (h + 1) * tq)
                if far:
                    shift = bias_ref[h, 2, 0:1, :]
                    lg = jnp.where(sel, logits[:, cs], NEG)
                    m_new = jnp.maximum(ms[h], jnp.max(lg, axis=0, keepdims=True) + shift)
                    p = jnp.exp2(lg - (m_new - shift))
                else:
                    bias = jnp.concatenate(
                        [bias_ref[h, jnp.clip(i - (c * nsub + s), 0, 2)] for s in range(nsub)], axis=0)
                    lg = jnp.where(sel, logits[:, cs] + bias, NEG)
                    m_new = jnp.maximum(ms[h], jnp.max(lg, axis=0, keepdims=True))
                    p = jnp.exp2(lg - m_new)
                alpha = jnp.exp2(ms[h] - m_new)
                ms_new.append(m_new)
                ps.append(p.astype(BF16))
                alphas.append(alpha)
                if not DIAG_ONES:
                    ls_new.append(alpha * ls[h] + jnp.sum(p, axis=0, keepdims=True))
            if DIAG_ONES:
                vb = v_ref[pl.ds(k0, kc), g * 2 * HEAD_DIM:(g + 1) * 2 * HEAD_DIM]
            else:
                vb = v_ref[pl.ds(k0, kc), g * 2 * HEAD_DIM:g * 2 * HEAD_DIM + HEAD_DIM]
            pv = lax.dot_general(vb, jnp.concatenate(ps, axis=1), TN, preferred_element_type=F32)
            acc_ref[g] = jnp.concatenate(alphas, axis=1) * acc_ref[g] + pv[0:HEAD_DIM]
            if DIAG_ONES:
                for r in range(REP):
                    ls_new.append(alphas[r] * ls[g * REP + r] + pv[HEAD_DIM:HEAD_DIM + 1, r * tq:(r + 1) * tq])
        return tuple(ms_new), tuple(ls_new)

    def att_trips(far, cs, carry):
        ties, ms, ls = carry
        staged = []
        for c in cs:
            ties, sel, logits = att_logits(far, c, ties)
            staged.append((c, sel, logits))
        for c, sel, logits in staged:
            ms, ls = att_softmax(far, c, sel, logits, ms, ls)
        return ties, ms, ls

    def att_loop(far, lo, hi, carry):
        if not DIAG_PAIR_ATT:
            return lax.fori_loop(lo, hi, lambda c, cr: att_trips(far, [c], cr), carry)
        odd = lax.rem(hi - lo, 2)
        carry = lax.fori_loop(lo, lo + odd, lambda c, cr: att_trips(far, [c], cr), carry)
        first = lo + odd
        return lax.fori_loop(0, (hi - first) // 2,
                             lambda j, cr: att_trips(far, [first + 2 * j, first + 2 * j + 1], cr), carry)

    carry = (jnp.zeros((1, tq), F32),
             tuple(jnp.full((1, tq), NEG, F32) for _ in range(N_HEADS)),
             tuple(jnp.zeros((1, tq), F32) for _ in range(N_HEADS)))
    carry = att_loop(True, 0, n_far, carry)
    _, _, ls = att_loop(False, n_far, nch, carry)

    blocks = []
    for g in range(N_KV_HEADS):
        for r in range(REP):
            blocks.append(acc_ref[g][:, r * tq:(r + 1) * tq] * (1.0 / ls[g * REP + r]))
    o_ref[...] = jnp.concatenate(blocks, axis=0).T


def _attn_prompt(q, qidx, misc, k, vaug, bias_t, *, b, t, tq, kc):
    m = q.shape[0]
    nq = t // tq
    topk = min(TOPK_MAX, t // 4)
    qrow = lambda bi, qi: (bi * nq + qi, 0)
    brow = lambda bi, qi: (bi, 0)
    sc_rows = _pick(t, (1024, 512, 256))
    body = functools.partial(_attn_prompt_body, tq=tq, kc=kc, sc_rows=sc_rows, topk=topk)
    return pl.pallas_call(
        body,
        grid=(b, nq),
        in_specs=[
            pl.BlockSpec((tq, N_HEADS * HEAD_DIM), qrow),
            pl.BlockSpec((tq, IDX_HEADS * IDX_DIM), qrow),
            pl.BlockSpec((tq, LANE), qrow),
            pl.BlockSpec((t, LANE), brow),
            pl.BlockSpec((t, VAUG_W), brow),
            pl.BlockSpec((t, LANE), brow),
            pl.BlockSpec((N_HEADS, 3, tq, tq), lambda bi, qi: (0, 0, 0, 0)),
        ],
        out_specs=pl.BlockSpec((tq, N_HEADS * HEAD_DIM), qrow),
        out_shape=jax.ShapeDtypeStruct((m, N_HEADS * HEAD_DIM), F32),
        scratch_shapes=[
            pltpu.VMEM((t, tq), I32),
            pltpu.VMEM((t, tq), I16),
            pltpu.VMEM((t, tq), I16),
            pltpu.VMEM((IDX_DIM, IDX_HEADS * tq), BF16),
            pltpu.VMEM((N_KV_HEADS * HEAD_DIM, N_HEADS * tq), BF16),
            pltpu.VMEM((kc, kc), BF16),
            pltpu.VMEM((N_KV_HEADS, HEAD_DIM, REP * tq), F32),
        ],
        compiler_params=_params(("parallel", "arbitrary")),
        name="attn_prompt",
    )(q, qidx, misc, k, vaug, misc, bias_t)


def _attn_sample_body(pt_ref, qi_ref, w_ref, qg_ref, bias_ref, kinew_ref, knew_ref, vnew_ref, *rest,
                      pps, n_pages, past, topk):
    kidx_refs = rest[0:pps]
    k_refs = rest[pps:2 * pps]
    v_refs = rest[2 * pps:3 * pps]
    o_ref = rest[3 * pps]
    keys_ref, thr_ref, need_ref, ties_ref, m_ref, l_ref, acc_ref = rest[3 * pps + 1:]
    del pt_ref
    j = pl.program_id(1)
    ng = n_pages // pps
    tp = SUBLANE
    grows = REP * tp
    qpos = past + lax.broadcasted_iota(I32, (tp, 1), 0)
    lane = lax.broadcasted_iota(I32, (1, PAGE_SIZE), 1)

    def score_page(kidx_t, slot, kpos0):
        s = jnp.dot(qi_ref[0].astype(BF16), kidx_t.astype(BF16), preferred_element_type=F32)
        w = w_ref[0] * (IDX_DIM ** -0.5 * IDX_HEADS ** -0.5)
        sc = jnp.maximum(s[0:tp], 0.0) * w[:, 0:1]
        for h in range(1, IDX_HEADS):
            sc = sc + jnp.maximum(s[h * tp:(h + 1) * tp], 0.0) * w[:, h:h + 1]
        sc = jnp.where(kpos0 + lane <= qpos, sc, -jnp.inf)
        keys_ref[slot] = _sort_key(sc)

    @pl.when(j < ng)
    def _():
        for i in range(pps):
            page = j * pps + i
            score_page(kidx_refs[i][0], page, page * PAGE_SIZE)

    @pl.when(j == ng - 1)
    def _():
        score_page(kinew_ref[0], n_pages, past)
        for s in range(n_pages + 1, n_pages + pps):
            keys_ref[s] = jnp.full((tp, PAGE_SIZE), INT_MIN, I32)

        n_acc = 8
        n_slots = keys_ref.shape[0]

        def count(pred_fn):
            hit = jnp.where(pred_fn(keys_ref[...]), 1.0, 0.0)
            part = jnp.sum(hit.reshape(n_slots // n_acc, n_acc, tp, PAGE_SIZE), axis=0)
            return jnp.sum(jnp.sum(part, axis=0), axis=1, keepdims=True)

        def bit_pass(bi, thr):
            cand = thr + lax.shift_left(jnp.int32(1), 31 - bi)
            return jnp.where(count(lambda kk: kk >= cand) >= topk, cand, thr)

        thr = lax.fori_loop(0, 32, bit_pass, jnp.full((tp, 1), INT_MIN, I32))
        thr_ref[...] = thr
        need_ref[...] = topk - count(lambda kk: kk > thr)
        ties_ref[...] = jnp.zeros(ties_ref.shape, F32)
        m_ref[...] = jnp.full(m_ref.shape, NEG, F32)
        l_ref[...] = jnp.zeros(l_ref.shape, F32)
        acc_ref[...] = jnp.zeros(acc_ref.shape, F32)

    def attend(pages):
        thr = thr_ref[...]
        need = need_ref[...]
        tri = jnp.where(lax.broadcasted_iota(I32, (PAGE_SIZE, PAGE_SIZE), 0)
                        < lax.broadcasted_iota(I32, (PAGE_SIZE, PAGE_SIZE), 1), 1.0, 0.0).astype(BF16)
        qg = (qg_ref[0] * (HEAD_DIM ** -0.5 * LOG2E)).astype(BF16)
        ties = ties_ref[...]
        lgs, vbs = [], []
        for kp, vp, slot, kpos0, bias in pages:
            kk = keys_ref[slot]
            eq = kk == thr
            eqf = jnp.where(eq, 1.0, 0.0)
            rank = jnp.dot(eqf.astype(BF16), tri, preferred_element_type=F32) + ties
            sel = ((kk > thr) | (eq & (rank < need))) & (kpos0 + lane <= qpos)
            ties = ties + jnp.sum(eqf, axis=1, keepdims=True)
            self = jnp.where(sel, 1.0, 0.0)
            sel_rows = jnp.concatenate([self] * N_HEADS, axis=0) > 0.5
            kb = kp.astype(BF16)
            lg = jnp.concatenate([jnp.dot(qg[g], kb[g], preferred_element_type=F32) for g in range(N_KV_HEADS)],
                                 axis=0)
            lgs.append(jnp.where(sel_rows, lg + bias, NEG))
            vbs.append(vp.astype(BF16))
        ties_ref[...] = ties
        lg = jnp.concatenate(lgs, axis=1)
        m_old = m_ref[...]
        m_new = jnp.maximum(m_old, jnp.max(lg, axis=1, keepdims=True))
        p = jnp.exp2(lg - m_new)
        alpha = jnp.exp2(m_old - m_new)
        l_ref[...] = alpha * l_ref[...] + jnp.sum(p, axis=1, keepdims=True)
        m_ref[...] = m_new
        pb = p.astype(BF16)
        pvs = []
        for g in range(N_KV_HEADS):
            pg = pb[g * grows:(g + 1) * grows]
            pv = None
            for n in range(len(pages)):
                d = lax.dot_general(pg[:, n * PAGE_SIZE:(n + 1) * PAGE_SIZE], vbs[n][g], NT,
                                    preferred_element_type=F32)
                pv = d if pv is None else pv + d
            pvs.append(pv)
        acc_ref[...] = alpha * acc_ref[...] + jnp.concatenate(pvs, axis=0)

    @pl.when(j >= ng)
    def _():
        pages = []
        for i in range(pps):
            page = (j - ng) * pps + i
            bias = jnp.where(page == n_pages - 1, bias_ref[1], bias_ref[0])
            pages.append((k_refs[i][0], v_refs[i][0], page, page * PAGE_SIZE, bias))
        attend(pages)

    @pl.when(j == 2 * ng - 1)
    def _():
        attend([(knew_ref[0], vnew_ref[0], n_pages, past, bias_ref[2])])
        o_ref[0] = acc_ref[...] / l_ref[...]


def _attn_sample(page_table, qi4, w8, qg, bias_s, kidx_new_t, k_new_t, v_new_t, pool_kidx_t, pool_k_t, pool_v_t,
                 *, pps):
    b, n_pages = page_table.shape
    past = n_pages * PAGE_SIZE
    t_new = 4
    topk = min(TOPK_MAX, (past + t_new) // 4)
    ng = n_pages // pps
    tp = SUBLANE
    rows = N_HEADS * tp
    per_b = lambda bi, j, pt: (bi, 0, 0)
    per_b4 = lambda bi, j, pt: (bi, 0, 0, 0)

    def kidx_map(i):
        return lambda bi, j, pt: (pt[bi * n_pages + jnp.minimum(j, ng - 1) * pps + i], 0, 0)

    def kv_map(i):
        return lambda bi, j, pt: (pt[bi * n_pages + jnp.maximum(j - ng, 0) * pps + i], 0, 0, 0)

    kv_block = (1, N_KV_HEADS, HEAD_DIM, PAGE_SIZE)
    in_specs = [
        pl.BlockSpec((1, IDX_HEADS * tp, IDX_DIM), per_b),
        pl.BlockSpec((1, tp, LANE), per_b),
        pl.BlockSpec((1, N_KV_HEADS, REP * tp, HEAD_DIM), per_b4),
        pl.BlockSpec((3, rows, PAGE_SIZE), lambda bi, j, pt: (0, 0, 0)),
        pl.BlockSpec((1, IDX_DIM, PAGE_SIZE), per_b),
        pl.BlockSpec(kv_block, per_b4),
        pl.BlockSpec(kv_block, per_b4),
    ]
    in_specs += [pl.BlockSpec((1, IDX_DIM, PAGE_SIZE), kidx_map(i)) for i in range(pps)]
    in_specs += [pl.BlockSpec(kv_block, kv_map(i)) for i in range(pps)]
    in_specs += [pl.BlockSpec(kv_block, kv_map(i)) for i in range(pps)]
    body = functools.partial(_attn_sample_body, pps=pps, n_pages=n_pages, past=past, topk=topk)
    grid_spec = pltpu.PrefetchScalarGridSpec(
        num_scalar_prefetch=1,
        grid=(b, 2 * ng),
        in_specs=in_specs,
        out_specs=pl.BlockSpec((1, rows, HEAD_DIM), per_b),
        scratch_shapes=[
            pltpu.VMEM((n_pages + pps, tp, PAGE_SIZE), I32),
            pltpu.VMEM((tp, 1), I32),
            pltpu.VMEM((tp, 1), F32),
            pltpu.VMEM((tp, 1), F32),
            pltpu.VMEM((rows, 1), F32),
            pltpu.VMEM((rows, 1), F32),
            pltpu.VMEM((rows, HEAD_DIM), F32),
        ],
    )
    return pl.pallas_call(
        body,
        grid_spec=grid_spec,
        out_shape=jax.ShapeDtypeStruct((b, rows, HEAD_DIM), F32),
        compiler_params=_params(("parallel", "arbitrary")),
        name="attn_sample",
    )(page_table.reshape(-1), qi4, w8, qg, bias_s, kidx_new_t, k_new_t, v_new_t,
      *([pool_kidx_t] * pps), *([pool_k_t] * pps), *([pool_v_t] * pps))


def _outproj_body(h_ref, a_ref, s_ref, wo_ref, n2_ref, wr_ref, br_ref, h1_ref, xn_ref, ri_ref, rw_ref):
    d_att = a_ref.shape[1]
    mix = (jnp.dot(a_ref[...].astype(BF16), wo_ref[0:d_att, :], preferred_element_type=F32)
           + jnp.dot(s_ref[...].astype(BF16), wo_ref[d_att:, :], preferred_element_type=F32))
    h1 = h_ref[...] + mix
    h1_ref[...] = h1
    ms = jnp.mean(h1 * h1, axis=-1, keepdims=True)
    xn = (h1 * lax.rsqrt(ms + EPS)) * n2_ref[...]
    xn_ref[...] = xn
    logits = jnp.dot(xn.astype(BF16), wr_ref[...], preferred_element_type=F32) + br_ref[...]
    tm = logits.shape[0]
    lane = lax.broadcasted_iota(I32, (tm, LANE), 1).astype(F32)
    ninf = -jnp.inf
    gl = jnp.where(lane < N_EXPERT_GROUPS, logits, ninf)
    gmax = jnp.max(gl, axis=-1, keepdims=True)
    grp = jnp.min(jnp.where(gl == gmax, lane, float(LANE)), axis=-1, keepdims=True)
    g_w = 1.0 / jnp.sum(jnp.exp(gl - gmax), axis=-1, keepdims=True)
    lo = N_EXPERT_GROUPS + grp * EXPERTS_PER_GROUP
    el = jnp.where((lane >= lo) & (lane < lo + EXPERTS_PER_GROUP), logits, ninf)
    v1 = jnp.max(el, axis=-1, keepdims=True)
    i1 = jnp.min(jnp.where(el == v1, lane, float(LANE)), axis=-1, keepdims=True)
    el2 = jnp.where(lane == i1, ninf, el)
    v2 = jnp.max(el2, axis=-1, keepdims=True)
    i2 = jnp.min(jnp.where(el2 == v2, lane, float(LANE)), axis=-1, keepdims=True)
    e2 = jnp.exp(v2 - v1)
    den = 1.0 + e2
    ids = jnp.where(lane == 0.0, i1 - N_EXPERT_GROUPS, jnp.where(lane == 1.0, i2 - N_EXPERT_GROUPS, 0.0))
    ri_ref[...] = ids.astype(I32)
    rw_ref[...] = jnp.where(lane == 0.0, g_w * (1.0 / den), jnp.where(lane == 1.0, g_w * (e2 / den), 0.0))


def _outproj(h, attn, ssd, w_out, norm2_w, w_router, b_router, tm):
    m, d = h.shape
    row = lambda i: (i, 0)
    const = lambda i: (0, 0)
    return pl.pallas_call(
        _outproj_body,
        grid=(m // tm,),
        in_specs=[
            pl.BlockSpec((tm, d), row),
            pl.BlockSpec((tm, attn.shape[1]), row),
            pl.BlockSpec((tm, ssd.shape[1]), row),
            pl.BlockSpec(w_out.shape, const),
            pl.BlockSpec((1, d), const),
            pl.BlockSpec((d, LANE), const),
            pl.BlockSpec((1, LANE), const),
        ],
        out_specs=[
            pl.BlockSpec((tm, d), row),
            pl.BlockSpec((tm, d), row),
            pl.BlockSpec((tm, LANE), row),
            pl.BlockSpec((tm, LANE), row),
        ],
        out_shape=[
            jax.ShapeDtypeStruct((m, d), F32),
            jax.ShapeDtypeStruct((m, d), F32),
            jax.ShapeDtypeStruct((m, LANE), I32),
            jax.ShapeDtypeStruct((m, LANE), F32),
        ],
        compiler_params=_params(("parallel",)),
        name="outproj_router",
    )(h, attn, ssd, w_out, norm2_w.reshape(1, d), w_router, b_router)


def _start_rows(idx_ref, n, src_hbm, dst, sem, idx_of):
    for r in range(n):
        pltpu.make_async_copy(src_hbm.at[pl.ds(idx_of(idx_ref, r), 1), :], dst.at[pl.ds(r, 1), :], sem).start()


def _wait_rows(n, src_hbm, dst, sem):
    pltpu.make_async_copy(src_hbm.at[pl.ds(0, n), :], dst, sem).wait()


def _expert_body(blk_e_ref, tok_ref, tok_next_ref, roww_ref, x_hbm, wg_ref, wu_ref, wd_ref, o_ref, xbuf, sem):
    del blk_e_ref
    i = pl.program_id(0)
    last = pl.num_programs(0) - 1
    slot = lax.rem(i, 2)
    other = 1 - slot
    rb = xbuf.shape[1]
    tok_of = lambda ref, r: ref[0, 0, r]

    @pl.when(i == 0)
    def _():
        _start_rows(tok_ref, rb, x_hbm, xbuf.at[0], sem.at[0], tok_of)

    _start_rows(tok_next_ref, rb, x_hbm, xbuf.at[other], sem.at[other], tok_of)
    _wait_rows(rb, x_hbm, xbuf.at[slot], sem.at[slot])
    xb = xbuf[slot].astype(BF16)
    hg = jnp.dot(xb, wg_ref[0], preferred_element_type=F32)
    hu = jnp.dot(xb, wu_ref[0], preferred_element_type=F32)
    hd = ((hg * _sigmoid(hg)) * hu).astype(BF16)
    y = jnp.dot(hd, wd_ref[0], preferred_element_type=F32)
    o_ref[...] = y * roww_ref[...]

    @pl.when(i == last)
    def _():
        _wait_rows(rb, x_hbm, xbuf.at[other], sem.at[other])


def _experts(blk_e, row_tok, row_w, xn, w_gate, w_up, w_down, rb):
    cap = row_tok.shape[0]
    n_blk = cap // rb
    d = xn.shape[1]
    de = w_gate.shape[2]
    tok3 = row_tok.reshape(n_blk, 1, rb)
    grid_spec = pltpu.PrefetchScalarGridSpec(
        num_scalar_prefetch=1,
        grid=(n_blk,),
        in_specs=[
            pl.BlockSpec((1, 1, rb), lambda i, be: (i, 0, 0), memory_space=pltpu.SMEM),
            pl.BlockSpec((1, 1, rb), lambda i, be: (jnp.minimum(i + 1, n_blk - 1), 0, 0), memory_space=pltpu.SMEM),
            pl.BlockSpec((rb, 1), lambda i, be: (i, 0)),
            pl.BlockSpec(memory_space=pl.ANY),
            pl.BlockSpec((1, d, de), lambda i, be: (be[i], 0, 0)),
            pl.BlockSpec((1, d, de), lambda i, be: (be[i], 0, 0)),
            pl.BlockSpec((1, de, d), lambda i, be: (be[i], 0, 0)),
        ],
        out_specs=pl.BlockSpec((rb, d), lambda i, be: (i, 0)),
        scratch_shapes=[pltpu.VMEM((2, rb, d), F32), pltpu.SemaphoreType.DMA((2,))],
    )
    return pl.pallas_call(
        _expert_body,
        grid_spec=grid_spec,
        out_shape=jax.ShapeDtypeStruct((cap, d), F32),
        compiler_params=_params(("arbitrary",)),
        name="experts",
    )(blk_e, tok3, tok3, row_w.reshape(cap, 1), xn, w_gate, w_up, w_down)


def _combine_body(pos_ref, pos_next_ref, h1_ref, yw_hbm, nf_ref, o_ref, buf, sem):
    i = pl.program_id(0)
    last = pl.num_programs(0) - 1
    slot = lax.rem(i, 2)
    other = 1 - slot
    tm = h1_ref.shape[0]

    def start(ref, s):
        for kk in range(TOP_K_INNER):
            _start_rows(ref, tm, yw_hbm, buf.at[s, kk], sem.at[s, kk],
                        lambda rf, r, kk=kk: rf[0, 0, TOP_K_INNER * r + kk])

    def wait(s):
        for kk in range(TOP_K_INNER):
            _wait_rows(tm, yw_hbm, buf.at[s, kk], sem.at[s, kk])

    @pl.when(i == 0)
    def _():
        start(pos_ref, 0)

    start(pos_next_ref, other)
    wait(slot)
    h = h1_ref[...] + (buf[slot, 0] + buf[slot, 1])
    ms = jnp.mean(h * h, axis=-1, keepdims=True)
    o_ref[...] = (h * lax.rsqrt(ms + EPS)) * nf_ref[...]

    @pl.when(i == last)
    def _():
        wait(other)


def _combine(pos, h1, yw, norm_f_w, tm):
    m, d = h1.shape
    nt = m // tm
    pos3 = pos.reshape(nt, 1, TOP_K_INNER * tm)
    return pl.pallas_call(
        _combine_body,
        grid=(nt,),
        in_specs=[
            pl.BlockSpec((1, 1, TOP_K_INNER * tm), lambda i: (i, 0, 0), memory_space=pltpu.SMEM),
            pl.BlockSpec((1, 1, TOP_K_INNER * tm), lambda i: (jnp.minimum(i + 1, nt - 1), 0, 0),
                         memory_space=pltpu.SMEM),
            pl.BlockSpec((tm, d), lambda i: (i, 0)),
            pl.BlockSpec(memory_space=pl.ANY),
            pl.BlockSpec((1, d), lambda i: (0, 0)),
        ],
        out_specs=pl.BlockSpec((tm, d), lambda i: (i, 0)),
        out_shape=jax.ShapeDtypeStruct((m, d), F32),
        scratch_shapes=[pltpu.VMEM((2, TOP_K_INNER, tm, d), F32), pltpu.SemaphoreType.DMA((2, TOP_K_INNER))],
        compiler_params=_params(("arbitrary",)),
        name="combine",
    )(pos3, pos3, h1, yw, norm_f_w.reshape(1, d))


def _permute_w_in(w_in):
    d = w_in.shape[0]
    c = 0
    parts = {}
    for name, n in (("q", 512), ("k", 128), ("v", 128), ("qi", 256), ("ki", 64), ("wi", 4),
                    ("z", 512), ("xbc", 768), ("dt", 8)):
        parts[name] = w_in[:, c:c + n]
        c += n
    pad = jnp.zeros((d, LANE - 64 - 4 - 8), w_in.dtype)
    return jnp.concatenate([parts["q"], parts["k"], parts["v"], parts["qi"], parts["ki"], parts["wi"],
                            parts["dt"], pad, parts["z"], parts["xbc"]], axis=1).astype(BF16)


def _route_tables(expert, gate, rb):
    a = expert.shape[0] * TOP_K_INNER
    e_flat = expert.reshape(-1)
    pid = jnp.arange(a, dtype=I32)
    _, order, gw_s = lax.sort((e_flat, pid, gate.reshape(-1)), num_keys=1, is_stable=True)
    eids = jnp.arange(N_EXPERTS, dtype=I32)
    sizes = jnp.sum((e_flat[:, None] == eids[None, :]).astype(I32), axis=0)
    gends = jnp.cumsum(sizes)
    padded = (sizes + rb - 1) // rb * rb
    pends = jnp.cumsum(padded)
    gap = padded - sizes
    cap = -(-(a + N_EXPERTS * (rb - 1)) // rb) * rb
    n_blk = cap // rb
    dest = pid + jnp.sum(jnp.where(gends[None, :] <= pid[:, None], gap[None, :], 0), axis=1)
    r = jnp.arange(cap, dtype=I32)
    done = pends[None, :] <= r[:, None]
    src = r - jnp.sum(jnp.where(done, gap[None, :], 0), axis=1)
    e_r = jnp.minimum(jnp.sum(done.astype(I32), axis=1), N_EXPERTS - 1)
    valid = src < jnp.sum(jnp.where(eids[None, :] == e_r[:, None], gends[None, :], 0), axis=1)
    src = jnp.minimum(src, a - 1)
    row_tok = jnp.where(valid, order[src] // TOP_K_INNER, 0)
    row_w = jnp.where(valid, gw_s[src], 0.0)
    _, pos = lax.sort((order, dest), num_keys=1)
    blk_e = jnp.minimum(jnp.sum((pends[None, :] <= (jnp.arange(n_blk, dtype=I32) * rb)[:, None]).astype(I32), axis=1),
                        N_EXPERTS - 1)
    return blk_e, row_tok, row_w, pos


def _pick(n, prefs):
    for p in prefs:
        if n % p == 0:
            return p
    return n


def _moe_and_final(h, attn, ssd, w_out, norm2_w, w_router, b_router, w_gate, w_up, w_down, norm_f_w):
    m = h.shape[0]
    tm = _pick(m, (512, 256, 128))
    h1, xn, ri, rw = _outproj(h, attn, ssd, w_out, norm2_w, w_router, b_router, tm)
    rb = 256 if m >= 4096 else 128
    blk_e, row_tok, row_w, pos = _route_tables(ri[:, :TOP_K_INNER], rw[:, :TOP_K_INNER], rb)
    yw = _experts(blk_e, row_tok, row_w, xn, w_gate, w_up, w_down, rb)
    return _combine(pos, h1, yw, norm_f_w, _pick(m, (256, 128)))


def kernel(x_prompt, x_sample, cache_k, cache_v, cache_kidx, state_conv, state_ssm, page_table, rel_bias,
           norm1_w, w_in, conv_w, conv_b, dt_bias, a_log, d_skip, ssd_norm_w, w_out, norm2_w,
           w_router_group, b_router_group, w_router_expert, b_router_expert, w_gate, w_up, w_down, norm_f_w):
    bp, tp_len, d = x_prompt.shape
    bs, ts, _ = x_sample.shape
    depth = w_in.shape[0]
    assert depth == 1 and ts == 4 and tp_len % SSD_CHUNK == 0
    l = 0

    w_perm = _permute_w_in(w_in[l])
    w_out_b = w_out[l].astype(BF16)
    n_r = N_EXPERT_GROUPS + N_EXPERTS
    w_router = jnp.concatenate([w_router_group[l], w_router_expert[l],
                                jnp.zeros((d, LANE - n_r), F32)], axis=1).astype(BF16)
    b_router = jnp.concatenate([b_router_group[l], b_router_expert[l], jnp.zeros((LANE - n_r,), F32)]).reshape(1, LANE)
    wg_b, wu_b, wd_b = w_gate[l].astype(BF16), w_up[l].astype(BF16), w_down[l].astype(BF16)
    rel_t = rel_bias.astype(F32).T

    mp = bp * tp_len
    xp = x_prompt.reshape(mp, d)
    q_p, k_p, v_p, qi_p, misc_p, z_p, xbc_p, vaug_p = _inproj(xp, norm1_w[l], w_perm, _pick(mp, (512, 256, 128)))
    conv0 = jnp.zeros((bp, CONV_W - 1, CONV_DIM), F32)
    ssm0 = jnp.zeros((bp, SSD_HEADS, SSD_HEAD_DIM, D_STATE), F32)
    ssd_p, ssm_p = _ssd(z_p, xbc_p, misc_p, conv0, ssm0, conv_w[l], conv_b[l], dt_bias[l], a_log[l], d_skip[l],
                        ssd_norm_w[l], b=bp, nc=tp_len // SSD_CHUNK, cl=SSD_CHUNK, t_valid=SSD_CHUNK)
    tq = 128
    kc = 256 if tp_len % 256 == 0 else 128
    jj = jnp.arange(tq, dtype=I32)[:, None]
    qq = jnp.arange(tq, dtype=I32)[None, :]
    dist_p = jnp.stack([dd * tq + qq - jj for dd in range(3)])
    bias_p = rel_t[:, _rel_bucket(dist_p)] * LOG2E
    attn_p = _attn_prompt(q_p, qi_p, misc_p, k_p, vaug_p, bias_p, b=bp, t=tp_len, tq=tq, kc=kc)
    y_p = _moe_and_final(xp, attn_p, ssd_p, w_out_b, norm2_w[l], w_router, b_router, wg_b, wu_b, wd_b, norm_f_w)

    ms = bs * ts
    xs = x_sample.reshape(ms, d)
    q_s, k_s, v_s, qi_s, misc_s, z_s, xbc_s, _ = _inproj(xs, norm1_w[l], w_perm, _pick(ms, (512, 256, 128)))
    tpad = SUBLANE

    def pad_t(a):
        n = a.shape[1]
        return jnp.pad(a.reshape(bs, ts, n), ((0, 0), (0, tpad - ts), (0, 0))).reshape(bs * tpad, n)

    ssd_s8, ssm_s = _ssd(pad_t(z_s), pad_t(xbc_s), pad_t(misc_s), state_conv[l], state_ssm[l], conv_w[l], conv_b[l],
                         dt_bias[l], a_log[l], d_skip[l], ssd_norm_w[l], b=bs, nc=1, cl=tpad, t_valid=ts)
    ssd_s = ssd_s8.reshape(bs, tpad, D_SSD)[:, :ts].reshape(ms, D_SSD)

    n_pages = page_table.shape[1]
    past = n_pages * PAGE_SIZE
    padq = ((0, 0), (0, 0), (0, tpad - ts), (0, 0))
    qi4 = jnp.pad(qi_s.reshape(bs, ts, IDX_HEADS, IDX_DIM).transpose(0, 2, 1, 3), padq)
    qi4 = qi4.reshape(bs, IDX_HEADS * tpad, IDX_DIM)
    w8 = jnp.pad(misc_s.reshape(bs, ts, LANE)[:, :, MISC_W:MISC_W + IDX_HEADS],
                 ((0, 0), (0, tpad - ts), (0, LANE - IDX_HEADS)))
    qh = jnp.pad(q_s.reshape(bs, ts, N_HEADS, HEAD_DIM).transpose(0, 2, 1, 3), padq)
    qg = qh.reshape(bs, N_KV_HEADS, REP * tpad, HEAD_DIM)
    tt = jnp.tile(jnp.arange(tpad, dtype=I32), N_HEADS)[:, None]
    hh = jnp.repeat(jnp.arange(N_HEADS, dtype=I32), tpad)[:, None]
    jl = jnp.arange(PAGE_SIZE, dtype=I32)[None, :]
    dist_s = jnp.stack([2 * PAGE_SIZE + tt - jl + PAGE_SIZE, PAGE_SIZE + tt - jl, tt - jl])
    bias_s = rel_t[hh[None], _rel_bucket(dist_s)] * LOG2E

    def new_page_t(a, lead):
        a = a.reshape(bs, ts, lead, HEAD_DIM).transpose(0, 2, 3, 1)
        return jnp.pad(a, ((0, 0), (0, 0), (0, 0), (0, PAGE_SIZE - ts)))

    kidx_new_t = new_page_t(misc_s[:, :IDX_DIM], 1)[:, 0]
    o_s = _attn_sample(page_table, qi4, w8, qg, bias_s, kidx_new_t,
                       new_page_t(k_s, N_KV_HEADS), new_page_t(v_s, N_KV_HEADS),
                       cache_kidx[l].transpose(0, 2, 1), cache_k[l].transpose(0, 2, 3, 1),
                       cache_v[l].transpose(0, 2, 3, 1), pps=_pick(n_pages, (16, 8)))
    o_s = o_s.reshape(bs, N_KV_HEADS, REP, tpad, HEAD_DIM)[:, :, :, :ts]
    attn_s = o_s.transpose(0, 3, 1, 2, 4).reshape(ms, N_HEADS * HEAD_DIM)
    y_s = _moe_and_final(xs, attn_s, ssd_s, w_out_b, norm2_w[l], w_router, b_router, wg_b, wu_b, wd_b, norm_f_w)

    def cache_out(k_, b_, t_):
        return k_.reshape(1, b_, t_, N_KV_HEADS, HEAD_DIM)

    return (
        y_p.reshape(bp, tp_len, d),
        y_s.reshape(bs, ts, d),
        cache_out(k_p, bp, tp_len), cache_out(v_p, bp, tp_len),
        misc_p[:, :IDX_DIM].reshape(1, bp, tp_len, IDX_DIM),
        xbc_p.reshape(bp, tp_len, CONV_DIM)[:, tp_len - (CONV_W - 1):][None],
        ssm_p[None],
        cache_out(k_s, bs, ts), cache_out(v_s, bs, ts),
        misc_s[:, :IDX_DIM].reshape(1, bs, ts, IDX_DIM),
        xbc_s.reshape(bs, ts, CONV_DIM)[:, ts - (CONV_W - 1):][None],
        ssm_s[None],
    )
```

```python
import functools
import math

import jax
import jax.numpy as jnp
from jax import lax
from jax.experimental import pallas as pl
from jax.experimental.pallas import tpu as pltpu

F32 = jnp.float32
BF16 = jnp.bfloat16
I32 = jnp.int32
HIGHEST = lax.Precision.HIGHEST

HEAD_DIM = 64
N_HEADS = 8
N_KV_HEADS = 2
REP = N_HEADS // N_KV_HEADS
IDX_HEADS = 4
IDX_DIM = 64
TOPK_MAX = 256
N_BUCKETS = 32
MAX_DISTANCE = 128
D_SSD = 512
SSD_HEADS = 8
SSD_HEAD_DIM = 64
SSD_GROUPS = 2
D_STATE = 64
CONV_W = 4
CONV_DIM = D_SSD + 2 * SSD_GROUPS * D_STATE
SSD_CHUNK = 128
N_EXPERT_GROUPS = 4
EXPERTS_PER_GROUP = 8
N_EXPERTS = N_EXPERT_GROUPS * EXPERTS_PER_GROUP
TOP_K_INNER = 2
PAGE_SIZE = 128
EPS = 1e-6

LANE = 128
SUBLANE = 8
VMEM_LIMIT = 56 * 1024 * 1024
NEG = -1e30
INT_MIN = -(2 ** 31)
LOG2E = math.log2(math.e)
DIAG_ONES, DIAG_PAIR_ATT, DIAG_PAIR_IDX = True, True, True

SPLITS = (512, 128, 128, 256, 128, 512, 768)
MISC_W = 64
MISC_DT = 68
VAUG_W = N_KV_HEADS * 2 * HEAD_DIM

NT = (((1,), (1,)), ((), ()))
TN = (((0,), (0,)), ((), ()))


def _params(sem):
    return pltpu.CompilerParams(dimension_semantics=sem, vmem_limit_bytes=VMEM_LIMIT)


def _sigmoid(x):
    return 1.0 / (1.0 + jnp.exp(-x))


def _paired_loop(body, lo, hi, carry):
    odd = lax.rem(hi - lo, 2)
    carry = lax.fori_loop(lo, lo + odd, body, carry)
    first = lo + odd

    def two(j, cr):
        c = first + 2 * j
        return body(c + 1, body(c, cr))

    return lax.fori_loop(0, (hi - first) // 2, two, carry)


def _sort_key(x):
    bits = lax.bitcast_convert_type(x, I32)
    return bits ^ ((bits >> 31) & 0x7FFFFFFF)


def _inproj_body(x_ref, nw_ref, w_ref, *out_refs):
    x = x_ref[...]
    ms = jnp.mean(x * x, axis=-1, keepdims=True)
    xb = ((x * lax.rsqrt(ms + EPS)) * nw_ref[...]).astype(BF16)
    off = 0
    for o_ref, n in zip(out_refs, SPLITS):
        o_ref[...] = jnp.dot(xb, w_ref[:, off:off + n], preferred_element_type=F32)
        off += n
    v = out_refs[2][...]
    ones = jnp.ones((v.shape[0], HEAD_DIM), F32)
    out_refs[len(SPLITS)][...] = jnp.concatenate(
        [piece for g in range(N_KV_HEADS) for piece in (v[:, g * HEAD_DIM:(g + 1) * HEAD_DIM], ones)],
        axis=1).astype(BF16)


def _inproj(x, norm_w, w_perm, tm):
    m, d = x.shape
    n_tot = sum(SPLITS)
    return pl.pallas_call(
        _inproj_body,
        grid=(m // tm,),
        in_specs=[
            pl.BlockSpec((tm, d), lambda i: (i, 0)),
            pl.BlockSpec((1, d), lambda i: (0, 0)),
            pl.BlockSpec((d, n_tot), lambda i: (0, 0)),
        ],
        out_specs=[pl.BlockSpec((tm, n), lambda i: (i, 0)) for n in SPLITS + (VAUG_W,)],
        out_shape=([jax.ShapeDtypeStruct((m, n), F32) for n in SPLITS]
                   + [jax.ShapeDtypeStruct((m, VAUG_W), BF16)]),
        compiler_params=_params(("parallel",)),
        name="inproj",
    )(x, norm_w.reshape(1, d), w_perm)


def _ssd_body(z_ref, xbc_ref, misc_ref, cs_ref, h0_ref, cw_ref, cb_ref, dtb_ref, alog_ref, dsk_ref, nw_ref,
              y_ref, hl_ref, xp_ref, h_ref, yb_ref, *, cl, t_valid, nc):
    c = pl.program_id(1)

    @pl.when(c == 0)
    def _():
        xp_ref[5:8, :] = cs_ref[0]
        h_ref[...] = h0_ref[0]

    xp_ref[8:8 + cl, :] = xbc_ref[...]
    cw = cw_ref[...]
    yc = cb_ref[...]
    for j in range(CONV_W):
        yc = yc + xp_ref[5 + j:5 + j + cl, :] * cw[j:j + 1, :]
    act = yc * _sigmoid(yc)
    xp_ref[5:8, :] = xbc_ref[cl - 3:cl, :]

    xs = act[:, :D_SSD]
    bm = act[:, D_SSD:D_SSD + SSD_GROUPS * D_STATE]
    cm = act[:, D_SSD + SSD_GROUPS * D_STATE:]
    xb = xs.astype(BF16)
    cb16 = cm.astype(BF16)
    bb16 = bm.astype(BF16)

    dtr = misc_ref[:, MISC_DT:MISC_DT + SSD_HEADS] + dtb_ref[...]
    dt = jnp.maximum(dtr, 0.0) + jnp.log1p(jnp.exp(-jnp.abs(dtr)))
    if t_valid < cl:
        dt = jnp.where(lax.broadcasted_iota(I32, (cl, SSD_HEADS), 0) < t_valid, dt, 0.0)
    a_neg = -jnp.exp(alog_ref[...])
    a = dt * a_neg
    ri = lax.broadcasted_iota(I32, (cl, cl), 0)
    ci = lax.broadcasted_iota(I32, (cl, cl), 1)
    tril = ci <= ri
    acs = jnp.dot(jnp.where(tril, 1.0, 0.0), a, precision=HIGHEST, preferred_element_type=F32)
    eye = jnp.where(lax.broadcasted_iota(I32, (SSD_HEADS, SSD_HEADS), 0)
                    == lax.broadcasted_iota(I32, (SSD_HEADS, SSD_HEADS), 1), 1.0, 0.0)
    dt_t = lax.dot_general(eye, dt, NT, precision=HIGHEST, preferred_element_type=F32)
    acs_t = lax.dot_general(eye, acs, NT, precision=HIGHEST, preferred_element_type=F32)
    acs_last = acs[cl - 1:cl, :]
    w_end = jnp.exp(acs_last - acs) * dt
    e_acs = jnp.exp(acs)
    c_dec = jnp.exp(acs_last)
    dsk = dsk_ref[...]

    for g in range(SSD_GROUPS):
        gs = slice(g * D_STATE, (g + 1) * D_STATE)
        cg = cb16[:, gs]
        cbm = lax.dot_general(cg, bb16[:, gs], NT, preferred_element_type=F32)
        for r in range(SSD_HEADS // SSD_GROUPS):
            h = g * (SSD_HEADS // SSD_GROUPS) + r
            hs = slice(h * SSD_HEAD_DIM, (h + 1) * SSD_HEAD_DIM)
            seg = acs[:, h:h + 1] - acs_t[h:h + 1, :]
            decay = jnp.exp(jnp.where(tril, seg, -jnp.inf))
            sc = cbm * decay * dt_t[h:h + 1, :]
            xh = xs[:, hs]
            xhb = xb[:, hs]
            y_diag = jnp.dot(sc.astype(BF16), xhb, preferred_element_type=F32)
            bw = (bm[:, gs] * w_end[:, h:h + 1]).astype(BF16)
            st = lax.dot_general(xhb, bw, TN, preferred_element_type=F32)
            h_in = h_ref[h]
            y_off = lax.dot_general(cg, h_in.astype(BF16), NT, preferred_element_type=F32) * e_acs[:, h:h + 1]
            h_ref[h] = h_in * c_dec[:, h:h + 1] + st
            yb_ref[:, hs] = (y_diag + y_off) + dsk[:, h:h + 1] * xh

    zz = z_ref[...]
    y = yb_ref[...] * (zz * _sigmoid(zz))
    gw = D_SSD // SSD_GROUPS
    for g in range(SSD_GROUPS):
        yg = y[:, g * gw:(g + 1) * gw]
        yg = yg * lax.rsqrt(jnp.mean(yg * yg, axis=-1, keepdims=True) + EPS)
        y_ref[:, g * gw:(g + 1) * gw] = yg * nw_ref[:, g * gw:(g + 1) * gw]

    @pl.when(c == nc - 1)
    def _():
        hl_ref[0] = h_ref[...]


def _ssd(z, xbc, misc, conv_state, h0, conv_w, conv_b, dt_bias, a_log, d_skip, norm_w, *, b, nc, cl, t_valid):
    m = z.shape[0]
    row = lambda bi, ci: (bi * nc + ci, 0)
    const2 = lambda bi, ci: (0, 0)
    body = functools.partial(_ssd_body, cl=cl, t_valid=t_valid, nc=nc)
    return pl.pallas_call(
        body,
        grid=(b, nc),
        in_specs=[
            pl.BlockSpec((cl, D_SSD), row),
            pl.BlockSpec((cl, CONV_DIM), row),
            pl.BlockSpec((cl, LANE), row),
            pl.BlockSpec((1, CONV_W - 1, CONV_DIM), lambda bi, ci: (bi, 0, 0)),
            pl.BlockSpec((1, SSD_HEADS, SSD_HEAD_DIM, D_STATE), lambda bi, ci: (bi, 0, 0, 0)),
            pl.BlockSpec((CONV_W, CONV_DIM), const2),
            pl.BlockSpec((1, CONV_DIM), const2),
            pl.BlockSpec((1, SSD_HEADS), const2),
            pl.BlockSpec((1, SSD_HEADS), const2),
            pl.BlockSpec((1, SSD_HEADS), const2),
            pl.BlockSpec((1, D_SSD), const2),
        ],
        out_specs=[
            pl.BlockSpec((cl, D_SSD), row),
            pl.BlockSpec((1, SSD_HEADS, SSD_HEAD_DIM, D_STATE), lambda bi, ci: (bi, 0, 0, 0)),
        ],
        out_shape=[
            jax.ShapeDtypeStruct((m, D_SSD), F32),
            jax.ShapeDtypeStruct((b, SSD_HEADS, SSD_HEAD_DIM, D_STATE), F32),
        ],
        scratch_shapes=[
            pltpu.VMEM((SUBLANE + cl, CONV_DIM), F32),
            pltpu.VMEM((SSD_HEADS, SSD_HEAD_DIM, D_STATE), F32),
            pltpu.VMEM((cl, D_SSD), F32),
        ],
        compiler_params=_params(("parallel", "arbitrary")),
        name="ssd",
    )(z, xbc, misc, conv_state, h0, conv_w, conv_b.reshape(1, -1), dt_bias.reshape(1, -1),
      a_log.reshape(1, -1), d_skip.reshape(1, -1), norm_w.reshape(1, -1))


def _rel_bucket(dist):
    n = jnp.maximum(dist, 0)
    max_exact = N_BUCKETS // 2
    nf = jnp.maximum(n, 1).astype(F32)
    log_part = jnp.log(nf / max_exact) / math.log(MAX_DISTANCE / max_exact) * (N_BUCKETS - max_exact)
    large = jnp.minimum(max_exact + log_part.astype(I32), N_BUCKETS - 1)
    return jnp.where(n < max_exact, n, large)


def _attn_prompt_body(q_ref, qi_ref, mq_ref, k_ref, v_ref, mk_ref, bias_ref, o_ref,
                      keys_ref, qit_ref, qbd_ref, tri_ref, acc_ref, *, tq, kc, sc_rows, topk):
    i = pl.program_id(1)
    nch = (i * tq + tq + kc - 1) // kc
    nsel = (i * tq + tq + sc_rows - 1) // sc_rows
    nsub = kc // tq
    n_far = jnp.maximum((i - 1) // nsub, 0)
    qpos = i * tq + lax.broadcasted_iota(I32, (1, tq), 1)

    qi_t = qi_ref[...].T
    qit_ref[...] = jnp.concatenate(
        [qi_t[h * IDX_DIM:(h + 1) * IDX_DIM, :] for h in range(IDX_HEADS)], axis=1).astype(BF16)
    w_rows = mq_ref[...].T[MISC_W:MISC_W + IDX_HEADS, :] * (IDX_DIM ** -0.5 * IDX_HEADS ** -0.5)
    q_t = q_ref[...].T * (HEAD_DIM ** -0.5 * LOG2E)
    zeros = jnp.zeros((HEAD_DIM, tq), F32)
    cols = []
    for g in range(N_KV_HEADS):
        for r in range(REP):
            h = g * REP + r
            blk = q_t[h * HEAD_DIM:(h + 1) * HEAD_DIM, :]
            cols.append(jnp.concatenate([blk, zeros] if g == 0 else [zeros, blk], axis=0))
    qbd_ref[...] = jnp.concatenate(cols, axis=1).astype(BF16)
    tri_ref[...] = jnp.where(lax.broadcasted_iota(I32, (kc, kc), 1) < lax.broadcasted_iota(I32, (kc, kc), 0),
                             1.0, 0.0).astype(BF16)

    def idx_chunk(c, carry):
        k0 = pl.multiple_of(c * kc, kc)
        kidx = mk_ref[pl.ds(k0, kc), 0:IDX_DIM].astype(BF16)
        s = jnp.dot(kidx, qit_ref[...], preferred_element_type=F32)
        sc = jnp.maximum(s[:, 0:tq], 0.0) * w_rows[0:1, :]
        for h in range(1, IDX_HEADS):
            sc = sc + jnp.maximum(s[:, h * tq:(h + 1) * tq], 0.0) * w_rows[h:h + 1, :]
        kpos = k0 + lax.broadcasted_iota(I32, (kc, 1), 0)
        sc = jnp.where(kpos <= qpos, sc, -jnp.inf)
        keys_ref[pl.ds(k0, kc), :] = _sort_key(sc)
        return carry

    if DIAG_PAIR_IDX:
        _paired_loop(idx_chunk, 0, nch, 0)
    else:
        lax.fori_loop(0, nch, idx_chunk, 0)

    def fill_chunk(c, carry):
        keys_ref[pl.ds(pl.multiple_of(c * kc, kc), kc), :] = jnp.full((kc, tq), INT_MIN, I32)
        return carry

    lax.fori_loop(nch, nsel * (sc_rows // kc), fill_chunk, 0)

    n_acc = 8
    acc_rows = n_acc * SUBLANE

    def count(pred_fn):
        def ch(c, part):
            k0 = pl.multiple_of(c * sc_rows, sc_rows)
            hit = jnp.where(pred_fn(keys_ref[pl.ds(k0, sc_rows), :]), 1.0, 0.0)
            return part + jnp.sum(hit.reshape(sc_rows // acc_rows, acc_rows, tq), axis=0)
        part = lax.fori_loop(0, nsel, ch, jnp.zeros((acc_rows, tq), F32))
        return jnp.sum(part, axis=0, keepdims=True)

    def bit_pass(bi, thr):
        cand = thr + lax.shift_left(jnp.int32(1), 31 - bi)
        cnt = count(lambda kk: kk >= cand)
        return jnp.where(cnt >= topk, cand, thr)

    thr = lax.fori_loop(0, 32, bit_pass, jnp.full((1, tq), INT_MIN, I32))
    need = topk - count(lambda kk: kk > thr)

    acc_ref[...] = jnp.zeros(acc_ref.shape, F32)

    def att_logits(far, c, ties_before):
        k0 = pl.multiple_of(c * kc, kc)
        kk = keys_ref[pl.ds(k0, kc), :]
        eq = kk == thr
        eqf = jnp.where(eq, 1.0, 0.0)
        rank = jnp.dot(tri_ref[...], eqf.astype(BF16), preferred_element_type=F32) + ties_before
        sel = (kk > thr) | (eq & (rank < need))
        if not far:
            sel = sel & (k0 + lax.broadcasted_iota(I32, (kc, 1), 0) <= qpos)
        kb = k_ref[pl.ds(k0, kc), :].astype(BF16)
        logits = jnp.dot(kb, qbd_ref[...], preferred_element_type=F32)
        return ties_before + jnp.sum(eqf, axis=0, keepdims=True), sel, logits

    def att_softmax(far, c, sel, logits, ms, ls):
        k0 = pl.multiple_of(c * kc, kc)
        ms_new, ls_new = [], []
        for g in range(N_KV_HEADS):
            ps, alphas = [], []
            for r in range(REP):
                h = g * REP + r
                cs = slice(h * tq, (h + 1) * tq)
                if far:
                    shift = bias_ref[h, 2, 0:1, :]
                    lg = jnp.where(sel, logits[:, cs], NEG)
                    m_new = jnp.maximum(ms[h], jnp.max(lg, axis=0, keepdims=True) + shift)
                    p = jnp.exp2(lg - (m_new - shift))
                else:
                    bias = jnp.concatenate(
                        [bias_ref[h, jnp.clip(i - (c * nsub + s), 0, 2)] for s in range(nsub)], axis=0)
                    lg = jnp.where(sel, logits[:, cs] + bias, NEG)
                    m_new = jnp.maximum(ms[h], jnp.max(lg, axis=0, keepdims=True))
                    p = jnp.exp2(lg - m_new)
                alpha = jnp.exp2(ms[h] - m_new)
                ms_new.append(m_new)
                ps.append(p.astype(BF16))
                alphas.append(alpha)
                if not DIAG_ONES:
                    ls_new.append(alpha * ls[h] + jnp.sum(p, axis=0, keepdims=True))
            if DIAG_ONES:
                vb = v_ref[pl.ds(k0, kc), g * 2 * HEAD_DIM:(g + 1) * 2 * HEAD_DIM]
            else:
                vb = v_ref[pl.ds(k0, kc), g * 2 * HEAD_DIM:g * 2 * HEAD_DIM + HEAD_DIM]
            pv = lax.dot_general(vb, jnp.concatenate(ps, axis=1), TN, preferred_element_type=F32)
            acc_ref[g] = jnp.concatenate(alphas, axis=1) * acc_ref[g] + pv[0:HEAD_DIM]
            if DIAG_ONES:
                for r in range(REP):
                    ls_new.append(alphas[r] * ls[g * REP + r] + pv[HEAD_DIM:HEAD_DIM + 1, r * tq:(r + 1) * tq])
        return tuple(ms_new), tuple(ls_new)

    def att_trips(far, cs, carry):
        ties, ms, ls = carry
        staged = []
        for c in cs:
            ties, sel, logits = att_logits(far, c, ties)
            staged.append((c, sel, logits))
        for c, sel, logits in staged:
            ms, ls = att_softmax(far, c, sel, logits, ms, ls)
        return ties, ms, ls

    def att_loop(far, lo, hi, carry):
        if not DIAG_PAIR_ATT:
            return lax.fori_loop(lo, hi, lambda c, cr: att_trips(far, [c], cr), carry)
        odd = lax.rem(hi - lo, 2)
        carry = lax.fori_loop(lo, lo + odd, lambda c, cr: att_trips(far, [c], cr), carry)
        first = lo + odd
        return lax.fori_loop(0, (hi - first) // 2,
                             lambda j, cr: att_trips(far, [first + 2 * j, first + 2 * j + 1], cr), carry)

    carry = (jnp.zeros((1, tq), F32),
             tuple(jnp.full((1, tq), NEG, F32) for _ in range(N_HEADS)),
             tuple(jnp.zeros((1, tq), F32) for _ in range(N_HEADS)))
    carry = att_loop(True, 0, n_far, carry)
    _, _, ls = att_loop(False, n_far, nch, carry)

    blocks = []
    for g in range(N_KV_HEADS):
        for r in range(REP):
            blocks.append(acc_ref[g][:, r * tq:(r + 1) * tq] * (1.0 / ls[g * REP + r]))
    o_ref[...] = jnp.concatenate(blocks, axis=0).T


def _attn_prompt(q, qidx, misc, k, vaug, bias_t, *, b, t, tq, kc):
    m = q.shape[0]
    nq = t // tq
    topk = min(TOPK_MAX, t // 4)
    qrow = lambda bi, qi: (bi * nq + qi, 0)
    brow = lambda bi, qi: (bi, 0)
    sc_rows = _pick(t, (1024, 512, 256))
    body = functools.partial(_attn_prompt_body, tq=tq, kc=kc, sc_rows=sc_rows, topk=topk)
    return pl.pallas_call(
        body,
        grid=(b, nq),
        in_specs=[
            pl.BlockSpec((tq, N_HEADS * HEAD_DIM), qrow),
            pl.BlockSpec((tq, IDX_HEADS * IDX_DIM), qrow),
            pl.BlockSpec((tq, LANE), qrow),
            pl.BlockSpec((t, LANE), brow),
            pl.BlockSpec((t, VAUG_W), brow),
            pl.BlockSpec((t, LANE), brow),
            pl.BlockSpec((N_HEADS, 3, tq, tq), lambda bi, qi: (0, 0, 0, 0)),
        ],
        out_specs=pl.BlockSpec((tq, N_HEADS * HEAD_DIM), qrow),
        out_shape=jax.ShapeDtypeStruct((m, N_HEADS * HEAD_DIM), F32),
        scratch_shapes=[
            pltpu.VMEM((t, tq), I32),
            pltpu.VMEM((IDX_DIM, IDX_HEADS * tq), BF16),
            pltpu.VMEM((N_KV_HEADS * HEAD_DIM, N_HEADS * tq), BF16),
            pltpu.VMEM((kc, kc), BF16),
            pltpu.VMEM((N_KV_HEADS, HEAD_DIM, REP * tq), F32),
        ],
        compiler_params=_params(("parallel", "arbitrary")),
        name="attn_prompt",
    )(q, qidx, misc, k, vaug, misc, bias_t)


def _attn_sample_body(pt_ref, qi_ref, w_ref, qg_ref, bias_ref, kinew_ref, knew_ref, vnew_ref, *rest,
                      pps, n_pages, past, topk):
    kidx_refs = rest[0:pps]
    k_refs = rest[pps:2 * pps]
    v_refs = rest[2 * pps:3 * pps]
    o_ref = rest[3 * pps]
    keys_ref, thr_ref, need_ref, ties_ref, m_ref, l_ref, acc_ref = rest[3 * pps + 1:]
    del pt_ref
    j = pl.program_id(1)
    ng = n_pages // pps
    tp = SUBLANE
    grows = REP * tp
    qpos = past + lax.broadcasted_iota(I32, (tp, 1), 0)
    lane = lax.broadcasted_iota(I32, (1, PAGE_SIZE), 1)

    def score_page(kidx_t, slot, kpos0):
        s = jnp.dot(qi_ref[0].astype(BF16), kidx_t.astype(BF16), preferred_element_type=F32)
        w = w_ref[0] * (IDX_DIM ** -0.5 * IDX_HEADS ** -0.5)
        sc = jnp.maximum(s[0:tp], 0.0) * w[:, 0:1]
        for h in range(1, IDX_HEADS):
            sc = sc + jnp.maximum(s[h * tp:(h + 1) * tp], 0.0) * w[:, h:h + 1]
        sc = jnp.where(kpos0 + lane <= qpos, sc, -jnp.inf)
        keys_ref[slot] = _sort_key(sc)

    @pl.when(j < ng)
    def _():
        for i in range(pps):
            page = j * pps + i
            score_page(kidx_refs[i][0], page, page * PAGE_SIZE)

    @pl.when(j == ng - 1)
    def _():
        score_page(kinew_ref[0], n_pages, past)
        for s in range(n_pages + 1, n_pages + pps):
            keys_ref[s] = jnp.full((tp, PAGE_SIZE), INT_MIN, I32)

        n_acc = 8
        n_slots = keys_ref.shape[0]

        def count(pred_fn):
            hit = jnp.where(pred_fn(keys_ref[...]), 1.0, 0.0)
            part = jnp.sum(hit.reshape(n_slots // n_acc, n_acc, tp, PAGE_SIZE), axis=0)
            return jnp.sum(jnp.sum(part, axis=0), axis=1, keepdims=True)

        def bit_pass(bi, thr):
            cand = thr + lax.shift_left(jnp.int32(1), 31 - bi)
            return jnp.where(count(lambda kk: kk >= cand) >= topk, cand, thr)

        thr = lax.fori_loop(0, 32, bit_pass, jnp.full((tp, 1), INT_MIN, I32))
        thr_ref[...] = thr
        need_ref[...] = topk - count(lambda kk: kk > thr)
        ties_ref[...] = jnp.zeros(ties_ref.shape, F32)
        m_ref[...] = jnp.full(m_ref.shape, NEG, F32)
        l_ref[...] = jnp.zeros(l_ref.shape, F32)
        acc_ref[...] = jnp.zeros(acc_ref.shape, F32)

    def attend(pages):
        thr = thr_ref[...]
        need = need_ref[...]
        tri = jnp.where(lax.broadcasted_iota(I32, (PAGE_SIZE, PAGE_SIZE), 0)
                        < lax.broadcasted_iota(I32, (PAGE_SIZE, PAGE_SIZE), 1), 1.0, 0.0).astype(BF16)
        qg = (qg_ref[0] * (HEAD_DIM ** -0.5 * LOG2E)).astype(BF16)
        ties = ties_ref[...]
        lgs, vbs = [], []
        for kp, vp, slot, kpos0, bias in pages:
            kk = keys_ref[slot]
            eq = kk == thr
            eqf = jnp.where(eq, 1.0, 0.0)
            rank = jnp.dot(eqf.astype(BF16), tri, preferred_element_type=F32) + ties
            sel = ((kk > thr) | (eq & (rank < need))) & (kpos0 + lane <= qpos)
            ties = ties + jnp.sum(eqf, axis=1, keepdims=True)
            self = jnp.where(sel, 1.0, 0.0)
            sel_rows = jnp.concatenate([self] * N_HEADS, axis=0) > 0.5
            kb = kp.astype(BF16)
            lg = jnp.concatenate([jnp.dot(qg[g], kb[g], preferred_element_type=F32) for g in range(N_KV_HEADS)],
                                 axis=0)
            lgs.append(jnp.where(sel_rows, lg + bias, NEG))
            vbs.append(vp.astype(BF16))
        ties_ref[...] = ties
        lg = jnp.concatenate(lgs, axis=1)
        m_old = m_ref[...]
        m_new = jnp.maximum(m_old, jnp.max(lg, axis=1, keepdims=True))
        p = jnp.exp2(lg - m_new)
        alpha = jnp.exp2(m_old - m_new)
        l_ref[...] = alpha * l_ref[...] + jnp.sum(p, axis=1, keepdims=True)
        m_ref[...] = m_new
        pb = p.astype(BF16)
        pvs = []
        for g in range(N_KV_HEADS):
            pg = pb[g * grows:(g + 1) * grows]
            pv = None
            for n in range(len(pages)):
                d = lax.dot_general(pg[:, n * PAGE_SIZE:(n + 1) * PAGE_SIZE], vbs[n][g], NT,
                                    preferred_element_type=F32)
                pv = d if pv is None else pv + d
            pvs.append(pv)
        acc_ref[...] = alpha * acc_ref[...] + jnp.concatenate(pvs, axis=0)

    @pl.when(j >= ng)
    def _():
        pages = []
        for i in range(pps):
            page = (j - ng) * pps + i
            bias = jnp.where(page == n_pages - 1, bias_ref[1], bias_ref[0])
            pages.append((k_refs[i][0], v_refs[i][0], page, page * PAGE_SIZE, bias))
        attend(pages)

    @pl.when(j == 2 * ng - 1)
    def _():
        attend([(knew_ref[0], vnew_ref[0], n_pages, past, bias_ref[2])])
        o_ref[0] = acc_ref[...] / l_ref[...]


def _attn_sample(page_table, qi4, w8, qg, bias_s, kidx_new_t, k_new_t, v_new_t, pool_kidx_t, pool_k_t, pool_v_t,
                 *, pps):
    b, n_pages = page_table.shape
    past = n_pages * PAGE_SIZE
    t_new = 4
    topk = min(TOPK_MAX, (past + t_new) // 4)
    ng = n_pages // pps
    tp = SUBLANE
    rows = N_HEADS * tp
    per_b = lambda bi, j, pt: (bi, 0, 0)
    per_b4 = lambda bi, j, pt: (bi, 0, 0, 0)

    def kidx_map(i):
        return lambda bi, j, pt: (pt[bi * n_pages + jnp.minimum(j, ng - 1) * pps + i], 0, 0)

    def kv_map(i):
        return lambda bi, j, pt: (pt[bi * n_pages + jnp.maximum(j - ng, 0) * pps + i], 0, 0, 0)

    kv_block = (1, N_KV_HEADS, HEAD_DIM, PAGE_SIZE)
    in_specs = [
        pl.BlockSpec((1, IDX_HEADS * tp, IDX_DIM), per_b),
        pl.BlockSpec((1, tp, LANE), per_b),
        pl.BlockSpec((1, N_KV_HEADS, REP * tp, HEAD_DIM), per_b4),
        pl.BlockSpec((3, rows, PAGE_SIZE), lambda bi, j, pt: (0, 0, 0)),
        pl.BlockSpec((1, IDX_DIM, PAGE_SIZE), per_b),
        pl.BlockSpec(kv_block, per_b4),
        pl.BlockSpec(kv_block, per_b4),
    ]
    in_specs += [pl.BlockSpec((1, IDX_DIM, PAGE_SIZE), kidx_map(i)) for i in range(pps)]
    in_specs += [pl.BlockSpec(kv_block, kv_map(i)) for i in range(pps)]
    in_specs += [pl.BlockSpec(kv_block, kv_map(i)) for i in range(pps)]
    body = functools.partial(_attn_sample_body, pps=pps, n_pages=n_pages, past=past, topk=topk)
    grid_spec = pltpu.PrefetchScalarGridSpec(
        num_scalar_prefetch=1,
        grid=(b, 2 * ng),
        in_specs=in_specs,
        out_specs=pl.BlockSpec((1, rows, HEAD_DIM), per_b),
        scratch_shapes=[
            pltpu.VMEM((n_pages + pps, tp, PAGE_SIZE), I32),
            pltpu.VMEM((tp, 1), I32),
            pltpu.VMEM((tp, 1), F32),
            pltpu.VMEM((tp, 1), F32),
            pltpu.VMEM((rows, 1), F32),
            pltpu.VMEM((rows, 1), F32),
            pltpu.VMEM((rows, HEAD_DIM), F32),
        ],
    )
    return pl.pallas_call(
        body,
        grid_spec=grid_spec,
        out_shape=jax.ShapeDtypeStruct((b, rows, HEAD_DIM), F32),
        compiler_params=_params(("parallel", "arbitrary")),
        name="attn_sample",
    )(page_table.reshape(-1), qi4, w8, qg, bias_s, kidx_new_t, k_new_t, v_new_t,
      *([pool_kidx_t] * pps), *([pool_k_t] * pps), *([pool_v_t] * pps))


def _outproj_body(h_ref, a_ref, s_ref, wo_ref, n2_ref, wr_ref, br_ref, h1_ref, xn_ref, ri_ref, rw_ref):
    d_att = a_ref.shape[1]
    mix = (jnp.dot(a_ref[...].astype(BF16), wo_ref[0:d_att, :], preferred_element_type=F32)
           + jnp.dot(s_ref[...].astype(BF16), wo_ref[d_att:, :], preferred_element_type=F32))
    h1 = h_ref[...] + mix
    h1_ref[...] = h1
    ms = jnp.mean(h1 * h1, axis=-1, keepdims=True)
    xn = (h1 * lax.rsqrt(ms + EPS)) * n2_ref[...]
    xn_ref[...] = xn
    logits = jnp.dot(xn.astype(BF16), wr_ref[...], preferred_element_type=F32) + br_ref[...]
    tm = logits.shape[0]
    lane = lax.broadcasted_iota(I32, (tm, LANE), 1).astype(F32)
    ninf = -jnp.inf
    gl = jnp.where(lane < N_EXPERT_GROUPS, logits, ninf)
    gmax = jnp.max(gl, axis=-1, keepdims=True)
    grp = jnp.min(jnp.where(gl == gmax, lane, float(LANE)), axis=-1, keepdims=True)
    g_w = 1.0 / jnp.sum(jnp.exp(gl - gmax), axis=-1, keepdims=True)
    lo = N_EXPERT_GROUPS + grp * EXPERTS_PER_GROUP
    el = jnp.where((lane >= lo) & (lane < lo + EXPERTS_PER_GROUP), logits, ninf)
    v1 = jnp.max(el, axis=-1, keepdims=True)
    i1 = jnp.min(jnp.where(el == v1, lane, float(LANE)), axis=-1, keepdims=True)
    el2 = jnp.where(lane == i1, ninf, el)
    v2 = jnp.max(el2, axis=-1, keepdims=True)
    i2 = jnp.min(jnp.where(el2 == v2, lane, float(LANE)), axis=-1, keepdims=True)
    e2 = jnp.exp(v2 - v1)
    den = 1.0 + e2
    ids = jnp.where(lane == 0.0, i1 - N_EXPERT_GROUPS, jnp.where(lane == 1.0, i2 - N_EXPERT_GROUPS, 0.0))
    ri_ref[...] = ids.astype(I32)
    rw_ref[...] = jnp.where(lane == 0.0, g_w * (1.0 / den), jnp.where(lane == 1.0, g_w * (e2 / den), 0.0))


def _outproj(h, attn, ssd, w_out, norm2_w, w_router, b_router, tm):
    m, d = h.shape
    row = lambda i: (i, 0)
    const = lambda i: (0, 0)
    return pl.pallas_call(
        _outproj_body,
        grid=(m // tm,),
        in_specs=[
            pl.BlockSpec((tm, d), row),
            pl.BlockSpec((tm, attn.shape[1]), row),
            pl.BlockSpec((tm, ssd.shape[1]), row),
            pl.BlockSpec(w_out.shape, const),
            pl.BlockSpec((1, d), const),
            pl.BlockSpec((d, LANE), const),
            pl.BlockSpec((1, LANE), const),
        ],
        out_specs=[
            pl.BlockSpec((tm, d), row),
            pl.BlockSpec((tm, d), row),
            pl.BlockSpec((tm, LANE), row),
            pl.BlockSpec((tm, LANE), row),
        ],
        out_shape=[
            jax.ShapeDtypeStruct((m, d), F32),
            jax.ShapeDtypeStruct((m, d), F32),
            jax.ShapeDtypeStruct((m, LANE), I32),
            jax.ShapeDtypeStruct((m, LANE), F32),
        ],
        compiler_params=_params(("parallel",)),
        name="outproj_router",
    )(h, attn, ssd, w_out, norm2_w.reshape(1, d), w_router, b_router)


def _start_rows(idx_ref, n, src_hbm, dst, sem, idx_of):
    for r in range(n):
        pltpu.make_async_copy(src_hbm.at[pl.ds(idx_of(idx_ref, r), 1), :], dst.at[pl.ds(r, 1), :], sem).start()


def _wait_rows(n, src_hbm, dst, sem):
    pltpu.make_async_copy(src_hbm.at[pl.ds(0, n), :], dst, sem).wait()


def _expert_body(blk_e_ref, tok_ref, tok_next_ref, roww_ref, x_hbm, wg_ref, wu_ref, wd_ref, o_ref, xbuf, sem):
    del blk_e_ref
    i = pl.program_id(0)
    last = pl.num_programs(0) - 1
    slot = lax.rem(i, 2)
    other = 1 - slot
    rb = xbuf.shape[1]
    tok_of = lambda ref, r: ref[0, 0, r]

    @pl.when(i == 0)
    def _():
        _start_rows(tok_ref, rb, x_hbm, xbuf.at[0], sem.at[0], tok_of)

    _start_rows(tok_next_ref, rb, x_hbm, xbuf.at[other], sem.at[other], tok_of)
    _wait_rows(rb, x_hbm, xbuf.at[slot], sem.at[slot])
    xb = xbuf[slot].astype(BF16)
    hg = jnp.dot(xb, wg_ref[0], preferred_element_type=F32)
    hu = jnp.dot(xb, wu_ref[0], preferred_element_type=F32)
    hd = ((hg * _sigmoid(hg)) * hu).astype(BF16)
    y = jnp.dot(hd, wd_ref[0], preferred_element_type=F32)
    o_ref[...] = y * roww_ref[...]

    @pl.when(i == last)
    def _():
        _wait_rows(rb, x_hbm, xbuf.at[other], sem.at[other])


def _experts(blk_e, row_tok, row_w, xn, w_gate, w_up, w_down, rb):
    cap = row_tok.shape[0]
    n_blk = cap // rb
    d = xn.shape[1]
    de = w_gate.shape[2]
    tok3 = row_tok.reshape(n_blk, 1, rb)
    grid_spec = pltpu.PrefetchScalarGridSpec(
        num_scalar_prefetch=1,
        grid=(n_blk,),
        in_specs=[
            pl.BlockSpec((1, 1, rb), lambda i, be: (i, 0, 0), memory_space=pltpu.SMEM),
            pl.BlockSpec((1, 1, rb), lambda i, be: (jnp.minimum(i + 1, n_blk - 1), 0, 0), memory_space=pltpu.SMEM),
            pl.BlockSpec((rb, 1), lambda i, be: (i, 0)),
            pl.BlockSpec(memory_space=pl.ANY),
            pl.BlockSpec((1, d, de), lambda i, be: (be[i], 0, 0)),
            pl.BlockSpec((1, d, de), lambda i, be: (be[i], 0, 0)),
            pl.BlockSpec((1, de, d), lambda i, be: (be[i], 0, 0)),
        ],
        out_specs=pl.BlockSpec((rb, d), lambda i, be: (i, 0)),
        scratch_shapes=[pltpu.VMEM((2, rb, d), F32), pltpu.SemaphoreType.DMA((2,))],
    )
    return pl.pallas_call(
        _expert_body,
        grid_spec=grid_spec,
        out_shape=jax.ShapeDtypeStruct((cap, d), F32),
        compiler_params=_params(("arbitrary",)),
        name="experts",
    )(blk_e, tok3, tok3, row_w.reshape(cap, 1), xn, w_gate, w_up, w_down)


def _combine_body(pos_ref, pos_next_ref, h1_ref, yw_hbm, nf_ref, o_ref, buf, sem):
    i = pl.program_id(0)
    last = pl.num_programs(0) - 1
    slot = lax.rem(i, 2)
    other = 1 - slot
    tm = h1_ref.shape[0]

    def start(ref, s):
        for kk in range(TOP_K_INNER):
            _start_rows(ref, tm, yw_hbm, buf.at[s, kk], sem.at[s, kk],
                        lambda rf, r, kk=kk: rf[0, 0, TOP_K_INNER * r + kk])

    def wait(s):
        for kk in range(TOP_K_INNER):
            _wait_rows(tm, yw_hbm, buf.at[s, kk], sem.at[s, kk])

    @pl.when(i == 0)
    def _():
        start(pos_ref, 0)

    start(pos_next_ref, other)
    wait(slot)
    h = h1_ref[...] + (buf[slot, 0] + buf[slot, 1])
    ms = jnp.mean(h * h, axis=-1, keepdims=True)
    o_ref[...] = (h * lax.rsqrt(ms + EPS)) * nf_ref[...]

    @pl.when(i == last)
    def _():
        wait(other)


def _combine(pos, h1, yw, norm_f_w, tm):
    m, d = h1.shape
    nt = m // tm
    pos3 = pos.reshape(nt, 1, TOP_K_INNER * tm)
    return pl.pallas_call(
        _combine_body,
        grid=(nt,),
        in_specs=[
            pl.BlockSpec((1, 1, TOP_K_INNER * tm), lambda i: (i, 0, 0), memory_space=pltpu.SMEM),
            pl.BlockSpec((1, 1, TOP_K_INNER * tm), lambda i: (jnp.minimum(i + 1, nt - 1), 0, 0),
                         memory_space=pltpu.SMEM),
            pl.BlockSpec((tm, d), lambda i: (i, 0)),
            pl.BlockSpec(memory_space=pl.ANY),
            pl.BlockSpec((1, d), lambda i: (0, 0)),
        ],
        out_specs=pl.BlockSpec((tm, d), lambda i: (i, 0)),
        out_shape=jax.ShapeDtypeStruct((m, d), F32),
        scratch_shapes=[pltpu.VMEM((2, TOP_K_INNER, tm, d), F32), pltpu.SemaphoreType.DMA((2, TOP_K_INNER))],
        compiler_params=_params(("arbitrary",)),
        name="combine",
    )(pos3, pos3, h1, yw, norm_f_w.reshape(1, d))


def _permute_w_in(w_in):
    d = w_in.shape[0]
    c = 0
    parts = {}
    for name, n in (("q", 512), ("k", 128), ("v", 128), ("qi", 256), ("ki", 64), ("wi", 4),
                    ("z", 512), ("xbc", 768), ("dt", 8)):
        parts[name] = w_in[:, c:c + n]
        c += n
    pad = jnp.zeros((d, LANE - 64 - 4 - 8), w_in.dtype)
    return jnp.concatenate([parts["q"], parts["k"], parts["v"], parts["qi"], parts["ki"], parts["wi"],
                            parts["dt"], pad, parts["z"], parts["xbc"]], axis=1).astype(BF16)


def _route_tables(expert, gate, rb):
    a = expert.shape[0] * TOP_K_INNER
    e_flat = expert.reshape(-1)
    pid = jnp.arange(a, dtype=I32)
    _, order, gw_s = lax.sort((e_flat, pid, gate.reshape(-1)), num_keys=1, is_stable=True)
    eids = jnp.arange(N_EXPERTS, dtype=I32)
    sizes = jnp.sum((e_flat[:, None] == eids[None, :]).astype(I32), axis=0)
    gends = jnp.cumsum(sizes)
    padded = (sizes + rb - 1) // rb * rb
    pends = jnp.cumsum(padded)
    gap = padded - sizes
    cap = -(-(a + N_EXPERTS * (rb - 1)) // rb) * rb
    n_blk = cap // rb
    dest = pid + jnp.sum(jnp.where(gends[None, :] <= pid[:, None], gap[None, :], 0), axis=1)
    r = jnp.arange(cap, dtype=I32)
    done = pends[None, :] <= r[:, None]
    src = r - jnp.sum(jnp.where(done, gap[None, :], 0), axis=1)
    e_r = jnp.minimum(jnp.sum(done.astype(I32), axis=1), N_EXPERTS - 1)
    valid = src < jnp.sum(jnp.where(eids[None, :] == e_r[:, None], gends[None, :], 0), axis=1)
    src = jnp.minimum(src, a - 1)
    row_tok = jnp.where(valid, order[src] // TOP_K_INNER, 0)
    row_w = jnp.where(valid, gw_s[src], 0.0)
    _, pos = lax.sort((order, dest), num_keys=1)
    blk_e = jnp.minimum(jnp.sum((pends[None, :] <= (jnp.arange(n_blk, dtype=I32) * rb)[:, None]).astype(I32), axis=1),
                        N_EXPERTS - 1)
    return blk_e, row_tok, row_w, pos


def _pick(n, prefs):
    for p in prefs:
        if n % p == 0:
            return p
    return n


def _moe_and_final(h, attn, ssd, w_out, norm2_w, w_router, b_router, w_gate, w_up, w_down, norm_f_w):
    m = h.shape[0]
    tm = _pick(m, (512, 256, 128))
    h1, xn, ri, rw = _outproj(h, attn, ssd, w_out, norm2_w, w_router, b_router, tm)
    rb = 256 if m >= 4096 else 128
    blk_e, row_tok, row_w, pos = _route_tables(ri[:, :TOP_K_INNER], rw[:, :TOP_K_INNER], rb)
    yw = _experts(blk_e, row_tok, row_w, xn, w_gate, w_up, w_down, rb)
    return _combine(pos, h1, yw, norm_f_w, _pick(m, (256, 128)))


def kernel(x_prompt, x_sample, cache_k, cache_v, cache_kidx, state_conv, state_ssm, page_table, rel_bias,
           norm1_w, w_in, conv_w, conv_b, dt_bias, a_log, d_skip, ssd_norm_w, w_out, norm2_w,
           w_router_group, b_router_group, w_router_expert, b_router_expert, w_gate, w_up, w_down, norm_f_w):
    bp, tp_len, d = x_prompt.shape
    bs, ts, _ = x_sample.shape
    depth = w_in.shape[0]
    assert depth == 1 and ts == 4 and tp_len % SSD_CHUNK == 0
    l = 0

    w_perm = _permute_w_in(w_in[l])
    w_out_b = w_out[l].astype(BF16)
    n_r = N_EXPERT_GROUPS + N_EXPERTS
    w_router = jnp.concatenate([w_router_group[l], w_router_expert[l],
                                jnp.zeros((d, LANE - n_r), F32)], axis=1).astype(BF16)
    b_router = jnp.concatenate([b_router_group[l], b_router_expert[l], jnp.zeros((LANE - n_r,), F32)]).reshape(1, LANE)
    wg_b, wu_b, wd_b = w_gate[l].astype(BF16), w_up[l].astype(BF16), w_down[l].astype(BF16)
    rel_t = rel_bias.astype(F32).T

    mp = bp * tp_len
    xp = x_prompt.reshape(mp, d)
    q_p, k_p, v_p, qi_p, misc_p, z_p, xbc_p, vaug_p = _inproj(xp, norm1_w[l], w_perm, _pick(mp, (512, 256, 128)))
    conv0 = jnp.zeros((bp, CONV_W - 1, CONV_DIM), F32)
    ssm0 = jnp.zeros((bp, SSD_HEADS, SSD_HEAD_DIM, D_STATE), F32)
    ssd_p, ssm_p = _ssd(z_p, xbc_p, misc_p, conv0, ssm0, conv_w[l], conv_b[l], dt_bias[l], a_log[l], d_skip[l],
                        ssd_norm_w[l], b=bp, nc=tp_len // SSD_CHUNK, cl=SSD_CHUNK, t_valid=SSD_CHUNK)
    tq = 128
    kc = 256 if tp_len % 256 == 0 else 128
    jj = jnp.arange(tq, dtype=I32)[:, None]
    qq = jnp.arange(tq, dtype=I32)[None, :]
    dist_p = jnp.stack([dd * tq + qq - jj for dd in range(3)])
    bias_p = rel_t[:, _rel_bucket(dist_p)] * LOG2E
    attn_p = _attn_prompt(q_p, qi_p, misc_p, k_p, vaug_p, bias_p, b=bp, t=tp_len, tq=tq, kc=kc)
    y_p = _moe_and_final(xp, attn_p, ssd_p, w_out_b, norm2_w[l], w_router, b_router, wg_b, wu_b, wd_b, norm_f_w)

    ms = bs * ts
    xs = x_sample.reshape(ms, d)
    q_s, k_s, v_s, qi_s, misc_s, z_s, xbc_s, _ = _inproj(xs, norm1_w[l], w_perm, _pick(ms, (512, 256, 128)))
    tpad = SUBLANE

    def pad_t(a):
        n = a.shape[1]
        return jnp.pad(a.reshape(bs, ts, n), ((0, 0), (0, tpad - ts), (0, 0))).reshape(bs * tpad, n)

    ssd_s8, ssm_s = _ssd(pad_t(z_s), pad_t(xbc_s), pad_t(misc_s), state_conv[l], state_ssm[l], conv_w[l], conv_b[l],
                         dt_bias[l], a_log[l], d_skip[l], ssd_norm_w[l], b=bs, nc=1, cl=tpad, t_valid=ts)
    ssd_s = ssd_s8.reshape(bs, tpad, D_SSD)[:, :ts].reshape(ms, D_SSD)

    n_pages = page_table.shape[1]
    past = n_pages * PAGE_SIZE
    padq = ((0, 0), (0, 0), (0, tpad - ts), (0, 0))
    qi4 = jnp.pad(qi_s.reshape(bs, ts, IDX_HEADS, IDX_DIM).transpose(0, 2, 1, 3), padq)
    qi4 = qi4.reshape(bs, IDX_HEADS * tpad, IDX_DIM)
    w8 = jnp.pad(misc_s.reshape(bs, ts, LANE)[:, :, MISC_W:MISC_W + IDX_HEADS],
                 ((0, 0), (0, tpad - ts), (0, LANE - IDX_HEADS)))
    qh = jnp.pad(q_s.reshape(bs, ts, N_HEADS, HEAD_DIM).transpose(0, 2, 1, 3), padq)
    qg = qh.reshape(bs, N_KV_HEADS, REP * tpad, HEAD_DIM)
    tt = jnp.tile(jnp.arange(tpad, dtype=I32), N_HEADS)[:, None]
    hh = jnp.repeat(jnp.arange(N_HEADS, dtype=I32), tpad)[:, None]
    jl = jnp.arange(PAGE_SIZE, dtype=I32)[None, :]
    dist_s = jnp.stack([2 * PAGE_SIZE + tt - jl + PAGE_SIZE, PAGE_SIZE + tt - jl, tt - jl])
    bias_s = rel_t[hh[None], _rel_bucket(dist_s)] * LOG2E

    def new_page_t(a, lead):
        a = a.reshape(bs, ts, lead, HEAD_DIM).transpose(0, 2, 3, 1)
        return jnp.pad(a, ((0, 0), (0, 0), (0, 0), (0, PAGE_SIZE - ts)))

    kidx_new_t = new_page_t(misc_s[:, :IDX_DIM], 1)[:, 0]
    o_s = _attn_sample(page_table, qi4, w8, qg, bias_s, kidx_new_t,
                       new_page_t(k_s, N_KV_HEADS), new_page_t(v_s, N_KV_HEADS),
                       cache_kidx[l].transpose(0, 2, 1), cache_k[l].transpose(0, 2, 3, 1),
                       cache_v[l].transpose(0, 2, 3, 1), pps=_pick(n_pages, (16, 8)))
    o_s = o_s.reshape(bs, N_KV_HEADS, REP, tpad, HEAD_DIM)[:, :, :, :ts]
    attn_s = o_s.transpose(0, 3, 1, 2, 4).reshape(ms, N_HEADS * HEAD_DIM)
    y_s = _moe_and_final(xs, attn_s, ssd_s, w_out_b, norm2_w[l], w_router, b_router, wg_b, wu_b, wd_b, norm_f_w)

    def cache_out(k_, b_, t_):
        return k_.reshape(1, b_, t_, N_KV_HEADS, HEAD_DIM)

    return (
        y_p.reshape(bp, tp_len, d),
        y_s.reshape(bs, ts, d),
        cache_out(k_p, bp, tp_len), cache_out(v_p, bp, tp_len),
        misc_p[:, :IDX_DIM].reshape(1, bp, tp_len, IDX_DIM),
        xbc_p.reshape(bp, tp_len, CONV_DIM)[:, tp_len - (CONV_W - 1):][None],
        ssm_p[None],
        cache_out(k_s, bs, ts), cache_out(v_s, bs, ts),
        misc_s[:, :IDX_DIM].reshape(1, bs, ts, IDX_DIM),
        xbc_s.reshape(bs, ts, CONV_DIM)[:, ts - (CONV_W - 1):][None],
        ssm_s[None],
    )
```

```python
import functools
import math

import jax
import jax.numpy as jnp
from jax import lax
from jax.experimental import pallas as pl
from jax.experimental.pallas import tpu as pltpu

F32 = jnp.float32
BF16 = jnp.bfloat16
I32 = jnp.int32
HIGHEST = lax.Precision.HIGHEST

HEAD_DIM = 64
N_HEADS = 8
N_KV_HEADS = 2
REP = N_HEADS // N_KV_HEADS
IDX_HEADS = 4
IDX_DIM = 64
TOPK_MAX = 256
N_BUCKETS = 32
MAX_DISTANCE = 128
D_SSD = 512
SSD_HEADS = 8
SSD_HEAD_DIM = 64
SSD_GROUPS = 2
D_STATE = 64
CONV_W = 4
CONV_DIM = D_SSD + 2 * SSD_GROUPS * D_STATE
SSD_CHUNK = 128
N_EXPERT_GROUPS = 4
EXPERTS_PER_GROUP = 8
N_EXPERTS = N_EXPERT_GROUPS * EXPERTS_PER_GROUP
TOP_K_INNER = 2
PAGE_SIZE = 128
EPS = 1e-6

LANE = 128
SUBLANE = 8
VMEM_LIMIT = 56 * 1024 * 1024
NEG = -1e30
INT_MIN = -(2 ** 31)
LOG2E = math.log2(math.e)
SPLITS = (512, 128, 128, 256, 128, 512, 768)
MISC_W = 64
MISC_DT = 68
VAUG_W = N_KV_HEADS * 2 * HEAD_DIM

NT = (((1,), (1,)), ((), ()))
TN = (((0,), (0,)), ((), ()))


def _params(sem):
    return pltpu.CompilerParams(dimension_semantics=sem, vmem_limit_bytes=VMEM_LIMIT)


def _sigmoid(x):
    return 1.0 / (1.0 + jnp.exp(-x))


def _paired_loop(body, lo, hi, carry):
    odd = lax.rem(hi - lo, 2)
    carry = lax.fori_loop(lo, lo + odd, body, carry)
    first = lo + odd

    def two(j, cr):
        c = first + 2 * j
        return body(c + 1, body(c, cr))

    return lax.fori_loop(0, (hi - first) // 2, two, carry)


def _sort_key(x):
    bits = lax.bitcast_convert_type(x, I32)
    return bits ^ ((bits >> 31) & 0x7FFFFFFF)


def _inproj_body(x_ref, nw_ref, w_ref, *out_refs):
    x = x_ref[...]
    ms = jnp.mean(x * x, axis=-1, keepdims=True)
    xb = ((x * lax.rsqrt(ms + EPS)) * nw_ref[...]).astype(BF16)
    off = 0
    for o_ref, n in zip(out_refs, SPLITS):
        o_ref[...] = jnp.dot(xb, w_ref[:, off:off + n], preferred_element_type=F32)
        off += n
    v = out_refs[2][...]
    ones = jnp.ones((v.shape[0], HEAD_DIM), F32)
    out_refs[len(SPLITS)][...] = jnp.concatenate(
        [piece for g in range(N_KV_HEADS) for piece in (v[:, g * HEAD_DIM:(g + 1) * HEAD_DIM], ones)],
        axis=1).astype(BF16)


def _inproj(x, norm_w, w_perm, tm):
    m, d = x.shape
    n_tot = sum(SPLITS)
    return pl.pallas_call(
        _inproj_body,
        grid=(m // tm,),
        in_specs=[
            pl.BlockSpec((tm, d), lambda i: (i, 0)),
            pl.BlockSpec((1, d), lambda i: (0, 0)),
            pl.BlockSpec((d, n_tot), lambda i: (0, 0)),
        ],
        out_specs=[pl.BlockSpec((tm, n), lambda i: (i, 0)) for n in SPLITS + (VAUG_W,)],
        out_shape=([jax.ShapeDtypeStruct((m, n), F32) for n in SPLITS]
                   + [jax.ShapeDtypeStruct((m, VAUG_W), BF16)]),
        compiler_params=_params(("parallel",)),
        name="inproj",
    )(x, norm_w.reshape(1, d), w_perm)


def _ssd_body(z_ref, xbc_ref, misc_ref, cs_ref, h0_ref, cw_ref, cb_ref, dtb_ref, alog_ref, dsk_ref, nw_ref,
              y_ref, hl_ref, xp_ref, h_ref, yb_ref, *, cl, t_valid, nc):
    c = pl.program_id(1)

    @pl.when(c == 0)
    def _():
        xp_ref[5:8, :] = cs_ref[0]
        h_ref[...] = h0_ref[0]

    xp_ref[8:8 + cl, :] = xbc_ref[...]
    cw = cw_ref[...]
    yc = cb_ref[...]
    for j in range(CONV_W):
        yc = yc + xp_ref[5 + j:5 + j + cl, :] * cw[j:j + 1, :]
    act = yc * _sigmoid(yc)
    xp_ref[5:8, :] = xbc_ref[cl - 3:cl, :]

    xs = act[:, :D_SSD]
    bm = act[:, D_SSD:D_SSD + SSD_GROUPS * D_STATE]
    cm = act[:, D_SSD + SSD_GROUPS * D_STATE:]
    xb = xs.astype(BF16)
    cb16 = cm.astype(BF16)
    bb16 = bm.astype(BF16)

    dtr = misc_ref[:, MISC_DT:MISC_DT + SSD_HEADS] + dtb_ref[...]
    dt = jnp.maximum(dtr, 0.0) + jnp.log1p(jnp.exp(-jnp.abs(dtr)))
    if t_valid < cl:
        dt = jnp.where(lax.broadcasted_iota(I32, (cl, SSD_HEADS), 0) < t_valid, dt, 0.0)
    a_neg = -jnp.exp(alog_ref[...])
    a = dt * a_neg
    ri = lax.broadcasted_iota(I32, (cl, cl), 0)
    ci = lax.broadcasted_iota(I32, (cl, cl), 1)
    tril = ci <= ri
    acs = jnp.dot(jnp.where(tril, 1.0, 0.0), a, precision=HIGHEST, preferred_element_type=F32)
    eye = jnp.where(lax.broadcasted_iota(I32, (SSD_HEADS, SSD_HEADS), 0)
                    == lax.broadcasted_iota(I32, (SSD_HEADS, SSD_HEADS), 1), 1.0, 0.0)
    dt_t = lax.dot_general(eye, dt, NT, precision=HIGHEST, preferred_element_type=F32)
    acs_t = lax.dot_general(eye, acs, NT, precision=HIGHEST, preferred_element_type=F32)
    acs_last = acs[cl - 1:cl, :]
    w_end = jnp.exp(acs_last - acs) * dt
    e_acs = jnp.exp(acs)
    c_dec = jnp.exp(acs_last)
    dsk = dsk_ref[...]

    for g in range(SSD_GROUPS):
        gs = slice(g * D_STATE, (g + 1) * D_STATE)
        cg = cb16[:, gs]
        cbm = lax.dot_general(cg, bb16[:, gs], NT, preferred_element_type=F32)
        for r in range(SSD_HEADS // SSD_GROUPS):
            h = g * (SSD_HEADS // SSD_GROUPS) + r
            hs = slice(h * SSD_HEAD_DIM, (h + 1) * SSD_HEAD_DIM)
            seg = acs[:, h:h + 1] - acs_t[h:h + 1, :]
            decay = jnp.exp(jnp.where(tril, seg, -jnp.inf))
            sc = cbm * decay * dt_t[h:h + 1, :]
            xh = xs[:, hs]
            xhb = xb[:, hs]
            y_diag = jnp.dot(sc.astype(BF16), xhb, preferred_element_type=F32)
            bw = (bm[:, gs] * w_end[:, h:h + 1]).astype(BF16)
            st = lax.dot_general(xhb, bw, TN, preferred_element_type=F32)
            h_in = h_ref[h]
            y_off = lax.dot_general(cg, h_in.astype(BF16), NT, preferred_element_type=F32) * e_acs[:, h:h + 1]
            h_ref[h] = h_in * c_dec[:, h:h + 1] + st
            yb_ref[:, hs] = (y_diag + y_off) + dsk[:, h:h + 1] * xh

    zz = z_ref[...]
    y = yb_ref[...] * (zz * _sigmoid(zz))
    gw = D_SSD // SSD_GROUPS
    for g in range(SSD_GROUPS):
        yg = y[:, g * gw:(g + 1) * gw]
        yg = yg * lax.rsqrt(jnp.mean(yg * yg, axis=-1, keepdims=True) + EPS)
        y_ref[:, g * gw:(g + 1) * gw] = yg * nw_ref[:, g * gw:(g + 1) * gw]

    @pl.when(c == nc - 1)
    def _():
        hl_ref[0] = h_ref[...]


def _ssd(z, xbc, misc, conv_state, h0, conv_w, conv_b, dt_bias, a_log, d_skip, norm_w, *, b, nc, cl, t_valid):
    m = z.shape[0]
    row = lambda bi, ci: (bi * nc + ci, 0)
    const2 = lambda bi, ci: (0, 0)
    body = functools.partial(_ssd_body, cl=cl, t_valid=t_valid, nc=nc)
    return pl.pallas_call(
        body,
        grid=(b, nc),
        in_specs=[
            pl.BlockSpec((cl, D_SSD), row),
            pl.BlockSpec((cl, CONV_DIM), row),
            pl.BlockSpec((cl, LANE), row),
            pl.BlockSpec((1, CONV_W - 1, CONV_DIM), lambda bi, ci: (bi, 0, 0)),
            pl.BlockSpec((1, SSD_HEADS, SSD_HEAD_DIM, D_STATE), lambda bi, ci: (bi, 0, 0, 0)),
            pl.BlockSpec((CONV_W, CONV_DIM), const2),
            pl.BlockSpec((1, CONV_DIM), const2),
            pl.BlockSpec((1, SSD_HEADS), const2),
            pl.BlockSpec((1, SSD_HEADS), const2),
            pl.BlockSpec((1, SSD_HEADS), const2),
            pl.BlockSpec((1, D_SSD), const2),
        ],
        out_specs=[
            pl.BlockSpec((cl, D_SSD), row),
            pl.BlockSpec((1, SSD_HEADS, SSD_HEAD_DIM, D_STATE), lambda bi, ci: (bi, 0, 0, 0)),
        ],
        out_shape=[
            jax.ShapeDtypeStruct((m, D_SSD), F32),
            jax.ShapeDtypeStruct((b, SSD_HEADS, SSD_HEAD_DIM, D_STATE), F32),
        ],
        scratch_shapes=[
            pltpu.VMEM((SUBLANE + cl, CONV_DIM), F32),
            pltpu.VMEM((SSD_HEADS, SSD_HEAD_DIM, D_STATE), F32),
            pltpu.VMEM((cl, D_SSD), F32),
        ],
        compiler_params=_params(("parallel", "arbitrary")),
        name="ssd",
    )(z, xbc, misc, conv_state, h0, conv_w, conv_b.reshape(1, -1), dt_bias.reshape(1, -1),
      a_log.reshape(1, -1), d_skip.reshape(1, -1), norm_w.reshape(1, -1))


def _rel_bucket(dist):
    n = jnp.maximum(dist, 0)
    max_exact = N_BUCKETS // 2
    nf = jnp.maximum(n, 1).astype(F32)
    log_part = jnp.log(nf / max_exact) / math.log(MAX_DISTANCE / max_exact) * (N_BUCKETS - max_exact)
    large = jnp.minimum(max_exact + log_part.astype(I32), N_BUCKETS - 1)
    return jnp.where(n < max_exact, n, large)


def _attn_prompt_body(q_ref, qi_ref, mq_ref, k_ref, v_ref, mk_ref, bias_ref, o_ref,
                      keys_ref, qit_ref, qbd_ref, tri_ref, acc_ref, *, tq, kc, ki, sc_rows, topk):
    i = pl.program_id(1)
    nch = (i * tq + tq + kc - 1) // kc
    nsel = (i * tq + tq + sc_rows - 1) // sc_rows
    n_idx = (i * tq + tq + ki - 1) // ki
    nsub = kc // tq
    n_far = jnp.maximum((i - 1) // nsub, 0)
    qpos = i * tq + lax.broadcasted_iota(I32, (1, tq), 1)

    qi_t = qi_ref[...].T
    qit_ref[...] = jnp.concatenate(
        [qi_t[h * IDX_DIM:(h + 1) * IDX_DIM, :] for h in range(IDX_HEADS)], axis=1).astype(BF16)
    w_rows = mq_ref[...].T[MISC_W:MISC_W + IDX_HEADS, :] * (IDX_DIM ** -0.5 * IDX_HEADS ** -0.5)
    q_t = q_ref[...].T * (HEAD_DIM ** -0.5 * LOG2E)
    zeros = jnp.zeros((HEAD_DIM, tq), F32)
    cols = []
    for g in range(N_KV_HEADS):
        for r in range(REP):
            h = g * REP + r
            blk = q_t[h * HEAD_DIM:(h + 1) * HEAD_DIM, :]
            cols.append(jnp.concatenate([blk, zeros] if g == 0 else [zeros, blk], axis=0))
    qbd_ref[...] = jnp.concatenate(cols, axis=1).astype(BF16)
    tri_ref[...] = jnp.where(lax.broadcasted_iota(I32, (kc, kc), 1) < lax.broadcasted_iota(I32, (kc, kc), 0),
                             1.0, 0.0).astype(BF16)

    def idx_chunk(c, carry):
        k0 = pl.multiple_of(c * ki, ki)
        kidx = mk_ref[pl.ds(k0, ki), 0:IDX_DIM].astype(BF16)
        s = jnp.dot(kidx, qit_ref[...], preferred_element_type=F32)
        sc = jnp.maximum(s[:, 0:tq], 0.0) * w_rows[0:1, :]
        for h in range(1, IDX_HEADS):
            sc = sc + jnp.maximum(s[:, h * tq:(h + 1) * tq], 0.0) * w_rows[h:h + 1, :]
        kpos = k0 + lax.broadcasted_iota(I32, (ki, 1), 0)
        sc = jnp.where(kpos <= qpos, sc, -jnp.inf)
        keys_ref[pl.ds(k0, ki), :] = _sort_key(sc)
        return carry

    _paired_loop(idx_chunk, 0, n_idx, 0)

    def fill_chunk(c, carry):
        keys_ref[pl.ds(pl.multiple_of(c * ki, ki), ki), :] = jnp.full((ki, tq), INT_MIN, I32)
        return carry

    lax.fori_loop(n_idx, nsel * (sc_rows // ki), fill_chunk, 0)

    n_acc = 4
    acc_rows = n_acc * SUBLANE

    def count(pred_fn):
        def ch(c, part):
            k0 = pl.multiple_of(c * sc_rows, sc_rows)
            hit = jnp.where(pred_fn(keys_ref[pl.ds(k0, sc_rows), :]), 1.0, 0.0)
            return part + jnp.sum(hit.reshape(sc_rows // acc_rows, acc_rows, tq), axis=0)
        part = lax.fori_loop(0, nsel, ch, jnp.zeros((acc_rows, tq), F32))
        return jnp.sum(part, axis=0, keepdims=True)

    def bit_pass(bi, thr):
        cand = thr + lax.shift_left(jnp.int32(1), 31 - bi)
        cnt = count(lambda kk: kk >= cand)
        return jnp.where(cnt >= topk, cand, thr)

    thr = lax.fori_loop(0, 32, bit_pass, jnp.full((1, tq), INT_MIN, I32))
    need = topk - count(lambda kk: kk > thr)

    acc_ref[...] = jnp.zeros(acc_ref.shape, F32)

    def att_logits(far, c, ties_before):
        k0 = pl.multiple_of(c * kc, kc)
        kk = keys_ref[pl.ds(k0, kc), :]
        eq = kk == thr
        eqf = jnp.where(eq, 1.0, 0.0)
        rank = jnp.dot(tri_ref[...], eqf.astype(BF16), preferred_element_type=F32) + ties_before
        sel = (kk > thr) | (eq & (rank < need))
        if not far:
            sel = sel & (k0 + lax.broadcasted_iota(I32, (kc, 1), 0) <= qpos)
        kb = k_ref[pl.ds(k0, kc), :].astype(BF16)
        logits = jnp.dot(kb, qbd_ref[...], preferred_element_type=F32)
        return ties_before + jnp.sum(eqf, axis=0, keepdims=True), sel, logits

    def att_softmax(far, c, sel, logits, ms, ls):
        k0 = pl.multiple_of(c * kc, kc)
        ms_new, ls_new = [], []
        for g in range(N_KV_HEADS):
            ps, alphas = [], []
            for r in range(REP):
                h = g * REP + r
                cs = slice(h * tq, (h + 1) * tq)
                if far:
                    shift = bias_ref[h, 2, 0:1, :]
                    lg = jnp.where(sel, logits[:, cs], NEG)
                    m_new = jnp.maximum(ms[h], jnp.max(lg, axis=0, keepdims=True) + shift)
                    p = jnp.exp2(lg - (m_new - shift))
                else:
                    bias = jnp.concatenate(
                        [bias_ref[h, jnp.clip(i - (c * nsub + s), 0, 2)] for s in range(nsub)], axis=0)
                    lg = jnp.where(sel, logits[:, cs] + bias, NEG)
                    m_new = jnp.maximum(ms[h], jnp.max(lg, axis=0, keepdims=True))
                    p = jnp.exp2(lg - m_new)
                alpha = jnp.exp2(ms[h] - m_new)
                ms_new.append(m_new)
                ps.append(p.astype(BF16))
                alphas.append(alpha)
            vb = v_ref[pl.ds(k0, kc), g * 2 * HEAD_DIM:(g + 1) * 2 * HEAD_DIM]
            pv = lax.dot_general(vb, jnp.concatenate(ps, axis=1), TN, preferred_element_type=F32)
            acc_ref[g] = jnp.concatenate(alphas, axis=1) * acc_ref[g] + pv[0:HEAD_DIM]
            for r in range(REP):
                ls_new.append(alphas[r] * ls[g * REP + r] + pv[HEAD_DIM:HEAD_DIM + 1, r * tq:(r + 1) * tq])
        return tuple(ms_new), tuple(ls_new)

    def att_trips(far, cs, carry):
        ties, ms, ls = carry
        staged = []
        for c in cs:
            ties, sel, logits = att_logits(far, c, ties)
            staged.append((c, sel, logits))
        for c, sel, logits in staged:
            ms, ls = att_softmax(far, c, sel, logits, ms, ls)
        return ties, ms, ls

    def att_loop(far, lo, hi, carry):
        odd = lax.rem(hi - lo, 2)
        carry = lax.fori_loop(lo, lo + odd, lambda c, cr: att_trips(far, [c], cr), carry)
        first = lo + odd
        return lax.fori_loop(0, (hi - first) // 2,
                             lambda j, cr: att_trips(far, [first + 2 * j, first + 2 * j + 1], cr), carry)

    carry = (jnp.zeros((1, tq), F32),
             tuple(jnp.full((1, tq), NEG, F32) for _ in range(N_HEADS)),
             tuple(jnp.zeros((1, tq), F32) for _ in range(N_HEADS)))
    carry = att_loop(True, 0, n_far, carry)
    _, _, ls = att_loop(False, n_far, nch, carry)

    blocks = []
    for g in range(N_KV_HEADS):
        for r in range(REP):
            blocks.append(acc_ref[g][:, r * tq:(r + 1) * tq] * (1.0 / ls[g * REP + r]))
    o_ref[...] = jnp.concatenate(blocks, axis=0).T


def _attn_prompt(q, qidx, misc, k, vaug, bias_t, *, b, t, tq, kc):
    m = q.shape[0]
    nq = t // tq
    topk = min(TOPK_MAX, t // 4)
    qrow = lambda bi, qi: (bi * nq + qi, 0)
    brow = lambda bi, qi: (bi, 0)
    sc_rows = _pick(t, (1024, 512, 256))
    ki = _pick(t, (512, 256, 128))
    body = functools.partial(_attn_prompt_body, tq=tq, kc=kc, ki=ki, sc_rows=sc_rows, topk=topk)
    return pl.pallas_call(
        body,
        grid=(b, nq),
        in_specs=[
            pl.BlockSpec((tq, N_HEADS * HEAD_DIM), qrow),
            pl.BlockSpec((tq, IDX_HEADS * IDX_DIM), qrow),
            pl.BlockSpec((tq, LANE), qrow),
            pl.BlockSpec((t, LANE), brow),
            pl.BlockSpec((t, VAUG_W), brow),
            pl.BlockSpec((t, LANE), brow),
            pl.BlockSpec((N_HEADS, 3, tq, tq), lambda bi, qi: (0, 0, 0, 0)),
        ],
        out_specs=pl.BlockSpec((tq, N_HEADS * HEAD_DIM), qrow),
        out_shape=jax.ShapeDtypeStruct((m, N_HEADS * HEAD_DIM), F32),
        scratch_shapes=[
            pltpu.VMEM((t, tq), I32),
            pltpu.VMEM((IDX_DIM, IDX_HEADS * tq), BF16),
            pltpu.VMEM((N_KV_HEADS * HEAD_DIM, N_HEADS * tq), BF16),
            pltpu.VMEM((kc, kc), BF16),
            pltpu.VMEM((N_KV_HEADS, HEAD_DIM, REP * tq), F32),
        ],
        compiler_params=_params(("parallel", "arbitrary")),
        name="attn_prompt",
    )(q, qidx, misc, k, vaug, misc, bias_t)


def _attn_sample_body(pt_ref, qi_ref, w_ref, qg_ref, bias_ref, kinew_ref, knew_ref, vnew_ref, *rest,
                      pps, n_pages, past, topk):
    kidx_refs = rest[0:pps]
    k_refs = rest[pps:2 * pps]
    v_refs = rest[2 * pps:3 * pps]
    o_ref = rest[3 * pps]
    keys_ref, thr_ref, need_ref, ties_ref, m_ref, l_ref, acc_ref = rest[3 * pps + 1:]
    del pt_ref
    j = pl.program_id(1)
    ng = n_pages // pps
    tp = SUBLANE
    grows = REP * tp
    qpos = past + lax.broadcasted_iota(I32, (tp, 1), 0)
    lane = lax.broadcasted_iota(I32, (1, PAGE_SIZE), 1)

    def score_page(kidx_t, slot, kpos0):
        s = jnp.dot(qi_ref[0].astype(BF16), kidx_t.astype(BF16), preferred_element_type=F32)
        w = w_ref[0] * (IDX_DIM ** -0.5 * IDX_HEADS ** -0.5)
        sc = jnp.maximum(s[0:tp], 0.0) * w[:, 0:1]
        for h in range(1, IDX_HEADS):
            sc = sc + jnp.maximum(s[h * tp:(h + 1) * tp], 0.0) * w[:, h:h + 1]
        sc = jnp.where(kpos0 + lane <= qpos, sc, -jnp.inf)
        keys_ref[slot] = _sort_key(sc)

    @pl.when(j < ng)
    def _():
        for i in range(pps):
            page = j * pps + i
            score_page(kidx_refs[i][0], page, page * PAGE_SIZE)

    @pl.when(j == ng - 1)
    def _():
        score_page(kinew_ref[0], n_pages, past)
        for s in range(n_pages + 1, n_pages + pps):
            keys_ref[s] = jnp.full((tp, PAGE_SIZE), INT_MIN, I32)

        n_acc = 8
        n_slots = keys_ref.shape[0]

        def count(pred_fn):
            hit = jnp.where(pred_fn(keys_ref[...]), 1.0, 0.0)
            part = jnp.sum(hit.reshape(n_slots // n_acc, n_acc, tp, PAGE_SIZE), axis=0)
            return jnp.sum(jnp.sum(part, axis=0), axis=1, keepdims=True)

        def bit_pass(bi, thr):
            cand = thr + lax.shift_left(jnp.int32(1), 31 - bi)
            return jnp.where(count(lambda kk: kk >= cand) >= topk, cand, thr)

        thr = lax.fori_loop(0, 32, bit_pass, jnp.full((tp, 1), INT_MIN, I32))
        thr_ref[...] = thr
        need_ref[...] = topk - count(lambda kk: kk > thr)
        ties_ref[...] = jnp.zeros(ties_ref.shape, F32)
        m_ref[...] = jnp.full(m_ref.shape, NEG, F32)
        l_ref[...] = jnp.zeros(l_ref.shape, F32)
        acc_ref[...] = jnp.zeros(acc_ref.shape, F32)

    def attend(pages):
        thr = thr_ref[...]
        need = need_ref[...]
        tri = jnp.where(lax.broadcasted_iota(I32, (PAGE_SIZE, PAGE_SIZE), 0)
                        < lax.broadcasted_iota(I32, (PAGE_SIZE, PAGE_SIZE), 1), 1.0, 0.0).astype(BF16)
        qg = (qg_ref[0] * (HEAD_DIM ** -0.5 * LOG2E)).astype(BF16)
        ties = ties_ref[...]
        lgs, vbs = [], []
        for kp, vp, slot, kpos0, bias in pages:
            kk = keys_ref[slot]
            eq = kk == thr
            eqf = jnp.where(eq, 1.0, 0.0)
            rank = jnp.dot(eqf.astype(BF16), tri, preferred_element_type=F32) + ties
            sel = ((kk > thr) | (eq & (rank < need))) & (kpos0 + lane <= qpos)
            ties = ties + jnp.sum(eqf, axis=1, keepdims=True)
            self = jnp.where(sel, 1.0, 0.0)
            sel_rows = jnp.concatenate([self] * N_HEADS, axis=0) > 0.5
            kb = kp.astype(BF16)
            lg = jnp.concatenate([jnp.dot(qg[g], kb[g], preferred_element_type=F32) for g in range(N_KV_HEADS)],
                                 axis=0)
            lgs.append(jnp.where(sel_rows, lg + bias, NEG))
            vbs.append(vp.astype(BF16))
        ties_ref[...] = ties
        lg = jnp.concatenate(lgs, axis=1)
        m_old = m_ref[...]
        m_new = jnp.maximum(m_old, jnp.max(lg, axis=1, keepdims=True))
        p = jnp.exp2(lg - m_new)
        alpha = jnp.exp2(m_old - m_new)
        l_ref[...] = alpha * l_ref[...] + jnp.sum(p, axis=1, keepdims=True)
        m_ref[...] = m_new
        pb = p.astype(BF16)
        pvs = []
        for g in range(N_KV_HEADS):
            pg = pb[g * grows:(g + 1) * grows]
            pv = None
            for n in range(len(pages)):
                d = lax.dot_general(pg[:, n * PAGE_SIZE:(n + 1) * PAGE_SIZE], vbs[n][g], NT,
                                    preferred_element_type=F32)
                pv = d if pv is None else pv + d
            pvs.append(pv)
        acc_ref[...] = alpha * acc_ref[...] + jnp.concatenate(pvs, axis=0)

    @pl.when(j >= ng)
    def _():
        pages = []
        for i in range(pps):
            page = (j - ng) * pps + i
            bias = jnp.where(page == n_pages - 1, bias_ref[1], bias_ref[0])
            pages.append((k_refs[i][0], v_refs[i][0], page, page * PAGE_SIZE, bias))
        attend(pages)

    @pl.when(j == 2 * ng - 1)
    def _():
        attend([(knew_ref[0], vnew_ref[0], n_pages, past, bias_ref[2])])
        o_ref[0] = acc_ref[...] / l_ref[...]


def _attn_sample(page_table, qi4, w8, qg, bias_s, kidx_new_t, k_new_t, v_new_t, pool_kidx_t, pool_k_t, pool_v_t,
                 *, pps):
    b, n_pages = page_table.shape
    past = n_pages * PAGE_SIZE
    t_new = 4
    topk = min(TOPK_MAX, (past + t_new) // 4)
    ng = n_pages // pps
    tp = SUBLANE
    rows = N_HEADS * tp
    per_b = lambda bi, j, pt: (bi, 0, 0)
    per_b4 = lambda bi, j, pt: (bi, 0, 0, 0)

    def kidx_map(i):
        return lambda bi, j, pt: (pt[bi * n_pages + jnp.minimum(j, ng - 1) * pps + i], 0, 0)

    def kv_map(i):
        return lambda bi, j, pt: (pt[bi * n_pages + jnp.maximum(j - ng, 0) * pps + i], 0, 0, 0)

    kv_block = (1, N_KV_HEADS, HEAD_DIM, PAGE_SIZE)
    in_specs = [
        pl.BlockSpec((1, IDX_HEADS * tp, IDX_DIM), per_b),
        pl.BlockSpec((1, tp, LANE), per_b),
        pl.BlockSpec((1, N_KV_HEADS, REP * tp, HEAD_DIM), per_b4),
        pl.BlockSpec((3, rows, PAGE_SIZE), lambda bi, j, pt: (0, 0, 0)),
        pl.BlockSpec((1, IDX_DIM, PAGE_SIZE), per_b),
        pl.BlockSpec(kv_block, per_b4),
        pl.BlockSpec(kv_block, per_b4),
    ]
    in_specs += [pl.BlockSpec((1, IDX_DIM, PAGE_SIZE), kidx_map(i)) for i in range(pps)]
    in_specs += [pl.BlockSpec(kv_block, kv_map(i)) for i in range(pps)]
    in_specs += [pl.BlockSpec(kv_block, kv_map(i)) for i in range(pps)]
    body = functools.partial(_attn_sample_body, pps=pps, n_pages=n_pages, past=past, topk=topk)
    grid_spec = pltpu.PrefetchScalarGridSpec(
        num_scalar_prefetch=1,
        grid=(b, 2 * ng),
        in_specs=in_specs,
        out_specs=pl.BlockSpec((1, rows, HEAD_DIM), per_b),
        scratch_shapes=[
            pltpu.VMEM((n_pages + pps, tp, PAGE_SIZE), I32),
            pltpu.VMEM((tp, 1), I32),
            pltpu.VMEM((tp, 1), F32),
            pltpu.VMEM((tp, 1), F32),
            pltpu.VMEM((rows, 1), F32),
            pltpu.VMEM((rows, 1), F32),
            pltpu.VMEM((rows, HEAD_DIM), F32),
        ],
    )
    return pl.pallas_call(
        body,
        grid_spec=grid_spec,
        out_shape=jax.ShapeDtypeStruct((b, rows, HEAD_DIM), F32),
        compiler_params=_params(("parallel", "arbitrary")),
        name="attn_sample",
    )(page_table.reshape(-1), qi4, w8, qg, bias_s, kidx_new_t, k_new_t, v_new_t,
      *([pool_kidx_t] * pps), *([pool_k_t] * pps), *([pool_v_t] * pps))


def _outproj_body(h_ref, a_ref, s_ref, wo_ref, n2_ref, wr_ref, br_ref, h1_ref, xn_ref, ri_ref, rw_ref):
    d_att = a_ref.shape[1]
    mix = (jnp.dot(a_ref[...].astype(BF16), wo_ref[0:d_att, :], preferred_element_type=F32)
           + jnp.dot(s_ref[...].astype(BF16), wo_ref[d_att:, :], preferred_element_type=F32))
    h1 = h_ref[...] + mix
    h1_ref[...] = h1
    ms = jnp.mean(h1 * h1, axis=-1, keepdims=True)
    xn = (h1 * lax.rsqrt(ms + EPS)) * n2_ref[...]
    xn_ref[...] = xn
    logits = jnp.dot(xn.astype(BF16), wr_ref[...], preferred_element_type=F32) + br_ref[...]
    tm = logits.shape[0]
    lane = lax.broadcasted_iota(I32, (tm, LANE), 1).astype(F32)
    ninf = -jnp.inf
    gl = jnp.where(lane < N_EXPERT_GROUPS, logits, ninf)
    gmax = jnp.max(gl, axis=-1, keepdims=True)
    grp = jnp.min(jnp.where(gl == gmax, lane, float(LANE)), axis=-1, keepdims=True)
    g_w = 1.0 / jnp.sum(jnp.exp(gl - gmax), axis=-1, keepdims=True)
    lo = N_EXPERT_GROUPS + grp * EXPERTS_PER_GROUP
    el = jnp.where((lane >= lo) & (lane < lo + EXPERTS_PER_GROUP), logits, ninf)
    v1 = jnp.max(el, axis=-1, keepdims=True)
    i1 = jnp.min(jnp.where(el == v1, lane, float(LANE)), axis=-1, keepdims=True)
    el2 = jnp.where(lane == i1, ninf, el)
    v2 = jnp.max(el2, axis=-1, keepdims=True)
    i2 = jnp.min(jnp.where(el2 == v2, lane, float(LANE)), axis=-1, keepdims=True)
    e2 = jnp.exp(v2 - v1)
    den = 1.0 + e2
    ids = jnp.where(lane == 0.0, i1 - N_EXPERT_GROUPS, jnp.where(lane == 1.0, i2 - N_EXPERT_GROUPS, 0.0))
    ri_ref[...] = ids.astype(I32)
    rw_ref[...] = jnp.where(lane == 0.0, g_w * (1.0 / den), jnp.where(lane == 1.0, g_w * (e2 / den), 0.0))


def _outproj(h, attn, ssd, w_out, norm2_w, w_router, b_router, tm):
    m, d = h.shape
    row = lambda i: (i, 0)
    const = lambda i: (0, 0)
    return pl.pallas_call(
        _outproj_body,
        grid=(m // tm,),
        in_specs=[
            pl.BlockSpec((tm, d), row),
            pl.BlockSpec((tm, attn.shape[1]), row),
            pl.BlockSpec((tm, ssd.shape[1]), row),
            pl.BlockSpec(w_out.shape, const),
            pl.BlockSpec((1, d), const),
            pl.BlockSpec((d, LANE), const),
            pl.BlockSpec((1, LANE), const),
        ],
        out_specs=[
            pl.BlockSpec((tm, d), row),
            pl.BlockSpec((tm, d), row),
            pl.BlockSpec((tm, LANE), row),
            pl.BlockSpec((tm, LANE), row),
        ],
        out_shape=[
            jax.ShapeDtypeStruct((m, d), F32),
            jax.ShapeDtypeStruct((m, d), F32),
            jax.ShapeDtypeStruct((m, LANE), I32),
            jax.ShapeDtypeStruct((m, LANE), F32),
        ],
        compiler_params=_params(("parallel",)),
        name="outproj_router",
    )(h, attn, ssd, w_out, norm2_w.reshape(1, d), w_router, b_router)


def _start_rows(idx_ref, n, src_hbm, dst, sem, idx_of):
    for r in range(n):
        pltpu.make_async_copy(src_hbm.at[pl.ds(idx_of(idx_ref, r), 1), :], dst.at[pl.ds(r, 1), :], sem).start()


def _wait_rows(n, src_hbm, dst, sem):
    pltpu.make_async_copy(src_hbm.at[pl.ds(0, n), :], dst, sem).wait()


def _expert_body(blk_e_ref, tok_ref, tok_next_ref, roww_ref, x_hbm, wg_ref, wu_ref, wd_ref, o_ref, xbuf, sem):
    del blk_e_ref
    i = pl.program_id(0)
    last = pl.num_programs(0) - 1
    slot = lax.rem(i, 2)
    other = 1 - slot
    rb = xbuf.shape[1]
    tok_of = lambda ref, r: ref[0, 0, r]

    @pl.when(i == 0)
    def _():
        _start_rows(tok_ref, rb, x_hbm, xbuf.at[0], sem.at[0], tok_of)

    _start_rows(tok_next_ref, rb, x_hbm, xbuf.at[other], sem.at[other], tok_of)
    _wait_rows(rb, x_hbm, xbuf.at[slot], sem.at[slot])
    xb = xbuf[slot].astype(BF16)
    hg = jnp.dot(xb, wg_ref[0], preferred_element_type=F32)
    hu = jnp.dot(xb, wu_ref[0], preferred_element_type=F32)
    hd = ((hg * _sigmoid(hg)) * hu).astype(BF16)
    y = jnp.dot(hd, wd_ref[0], preferred_element_type=F32)
    o_ref[...] = y * roww_ref[...]

    @pl.when(i == last)
    def _():
        _wait_rows(rb, x_hbm, xbuf.at[other], sem.at[other])


def _experts(blk_e, row_tok, row_w, xn, w_gate, w_up, w_down, rb):
    cap = row_tok.shape[0]
    n_blk = cap // rb
    d = xn.shape[1]
    de = w_gate.shape[2]
    tok3 = row_tok.reshape(n_blk, 1, rb)
    grid_spec = pltpu.PrefetchScalarGridSpec(
        num_scalar_prefetch=1,
        grid=(n_blk,),
        in_specs=[
            pl.BlockSpec((1, 1, rb), lambda i, be: (i, 0, 0), memory_space=pltpu.SMEM),
            pl.BlockSpec((1, 1, rb), lambda i, be: (jnp.minimum(i + 1, n_blk - 1), 0, 0), memory_space=pltpu.SMEM),
            pl.BlockSpec((rb, 1), lambda i, be: (i, 0)),
            pl.BlockSpec(memory_space=pl.ANY),
            pl.BlockSpec((1, d, de), lambda i, be: (be[i], 0, 0)),
            pl.BlockSpec((1, d, de), lambda i, be: (be[i], 0, 0)),
            pl.BlockSpec((1, de, d), lambda i, be: (be[i], 0, 0)),
        ],
        out_specs=pl.BlockSpec((rb, d), lambda i, be: (i, 0)),
        scratch_shapes=[pltpu.VMEM((2, rb, d), F32), pltpu.SemaphoreType.DMA((2,))],
    )
    return pl.pallas_call(
        _expert_body,
        grid_spec=grid_spec,
        out_shape=jax.ShapeDtypeStruct((cap, d), F32),
        compiler_params=_params(("arbitrary",)),
        name="experts",
    )(blk_e, tok3, tok3, row_w.reshape(cap, 1), xn, w_gate, w_up, w_down)


def _combine_body(pos_ref, pos_next_ref, h1_ref, yw_hbm, nf_ref, o_ref, buf, sem):
    i = pl.program_id(0)
    last = pl.num_programs(0) - 1
    slot = lax.rem(i, 2)
    other = 1 - slot
    tm = h1_ref.shape[0]

    def start(ref, s):
        for kk in range(TOP_K_INNER):
            _start_rows(ref, tm, yw_hbm, buf.at[s, kk], sem.at[s, kk],
                        lambda rf, r, kk=kk: rf[0, 0, TOP_K_INNER * r + kk])

    def wait(s):
        for kk in range(TOP_K_INNER):
            _wait_rows(tm, yw_hbm, buf.at[s, kk], sem.at[s, kk])

    @pl.when(i == 0)
    def _():
        start(pos_ref, 0)

    start(pos_next_ref, other)
    wait(slot)
    h = h1_ref[...] + (buf[slot, 0] + buf[slot, 1])
    ms = jnp.mean(h * h, axis=-1, keepdims=True)
    o_ref[...] = (h * lax.rsqrt(ms + EPS)) * nf_ref[...]

    @pl.when(i == last)
    def _():
        wait(other)


def _combine(pos, h1, yw, norm_f_w, tm):
    m, d = h1.shape
    nt = m // tm
    pos3 = pos.reshape(nt, 1, TOP_K_INNER * tm)
    return pl.pallas_call(
        _combine_body,
        grid=(nt,),
        in_specs=[
            pl.BlockSpec((1, 1, TOP_K_INNER * tm), lambda i: (i, 0, 0), memory_space=pltpu.SMEM),
            pl.BlockSpec((1, 1, TOP_K_INNER * tm), lambda i: (jnp.minimum(i + 1, nt - 1), 0, 0),
                         memory_space=pltpu.SMEM),
            pl.BlockSpec((tm, d), lambda i: (i, 0)),
            pl.BlockSpec(memory_space=pl.ANY),
            pl.BlockSpec((1, d), lambda i: (0, 0)),
        ],
        out_specs=pl.BlockSpec((tm, d), lambda i: (i, 0)),
        out_shape=jax.ShapeDtypeStruct((m, d), F32),
        scratch_shapes=[pltpu.VMEM((2, TOP_K_INNER, tm, d), F32), pltpu.SemaphoreType.DMA((2, TOP_K_INNER))],
        compiler_params=_params(("arbitrary",)),
        name="combine",
    )(pos3, pos3, h1, yw, norm_f_w.reshape(1, d))


def _permute_w_in(w_in):
    d = w_in.shape[0]
    c = 0
    parts = {}
    for name, n in (("q", 512), ("k", 128), ("v", 128), ("qi", 256), ("ki", 64), ("wi", 4),
                    ("z", 512), ("xbc", 768), ("dt", 8)):
        parts[name] = w_in[:, c:c + n]
        c += n
    pad = jnp.zeros((d, LANE - 64 - 4 - 8), w_in.dtype)
    return jnp.concatenate([parts["q"], parts["k"], parts["v"], parts["qi"], parts["ki"], parts["wi"],
                            parts["dt"], pad, parts["z"], parts["xbc"]], axis=1).astype(BF16)


def _route_tables(expert, gate, rb):
    a = expert.shape[0] * TOP_K_INNER
    e_flat = expert.reshape(-1)
    pid = jnp.arange(a, dtype=I32)
    _, order, gw_s = lax.sort((e_flat, pid, gate.reshape(-1)), num_keys=1, is_stable=True)
    eids = jnp.arange(N_EXPERTS, dtype=I32)
    sizes = jnp.sum((e_flat[:, None] == eids[None, :]).astype(I32), axis=0)
    gends = jnp.cumsum(sizes)
    padded = (sizes + rb - 1) // rb * rb
    pends = jnp.cumsum(padded)
    gap = padded - sizes
    cap = -(-(a + N_EXPERTS * (rb - 1)) // rb) * rb
    n_blk = cap // rb
    dest = pid + jnp.sum(jnp.where(gends[None, :] <= pid[:, None], gap[None, :], 0), axis=1)
    r = jnp.arange(cap, dtype=I32)
    done = pends[None, :] <= r[:, None]
    src = r - jnp.sum(jnp.where(done, gap[None, :], 0), axis=1)
    e_r = jnp.minimum(jnp.sum(done.astype(I32), axis=1), N_EXPERTS - 1)
    valid = src < jnp.sum(jnp.where(eids[None, :] == e_r[:, None], gends[None, :], 0), axis=1)
    src = jnp.minimum(src, a - 1)
    row_tok = jnp.where(valid, order[src] // TOP_K_INNER, 0)
    row_w = jnp.where(valid, gw_s[src], 0.0)
    _, pos = lax.sort((order, dest), num_keys=1)
    blk_e = jnp.minimum(jnp.sum((pends[None, :] <= (jnp.arange(n_blk, dtype=I32) * rb)[:, None]).astype(I32), axis=1),
                        N_EXPERTS - 1)
    return blk_e, row_tok, row_w, pos


def _pick(n, prefs):
    for p in prefs:
        if n % p == 0:
            return p
    return n


def _moe_and_final(h, attn, ssd, w_out, norm2_w, w_router, b_router, w_gate, w_up, w_down, norm_f_w):
    m = h.shape[0]
    tm = _pick(m, (512, 256, 128))
    h1, xn, ri, rw = _outproj(h, attn, ssd, w_out, norm2_w, w_router, b_router, tm)
    rb = 256 if m >= 4096 else 128
    blk_e, row_tok, row_w, pos = _route_tables(ri[:, :TOP_K_INNER], rw[:, :TOP_K_INNER], rb)
    yw = _experts(blk_e, row_tok, row_w, xn, w_gate, w_up, w_down, rb)
    return _combine(pos, h1, yw, norm_f_w, _pick(m, (256, 128)))


def kernel(x_prompt, x_sample, cache_k, cache_v, cache_kidx, state_conv, state_ssm, page_table, rel_bias,
           norm1_w, w_in, conv_w, conv_b, dt_bias, a_log, d_skip, ssd_norm_w, w_out, norm2_w,
           w_router_group, b_router_group, w_router_expert, b_router_expert, w_gate, w_up, w_down, norm_f_w):
    bp, tp_len, d = x_prompt.shape
    bs, ts, _ = x_sample.shape
    depth = w_in.shape[0]
    assert depth == 1 and ts == 4 and tp_len % SSD_CHUNK == 0
    l = 0

    w_perm = _permute_w_in(w_in[l])
    w_out_b = w_out[l].astype(BF16)
    n_r = N_EXPERT_GROUPS + N_EXPERTS
    w_router = jnp.concatenate([w_router_group[l], w_router_expert[l],
                                jnp.zeros((d, LANE - n_r), F32)], axis=1).astype(BF16)
    b_router = jnp.concatenate([b_router_group[l], b_router_expert[l], jnp.zeros((LANE - n_r,), F32)]).reshape(1, LANE)
    wg_b, wu_b, wd_b = w_gate[l].astype(BF16), w_up[l].astype(BF16), w_down[l].astype(BF16)
    rel_t = rel_bias.astype(F32).T

    mp = bp * tp_len
    xp = x_prompt.reshape(mp, d)
    q_p, k_p, v_p, qi_p, misc_p, z_p, xbc_p, vaug_p = _inproj(xp, norm1_w[l], w_perm, _pick(mp, (512, 256, 128)))
    conv0 = jnp.zeros((bp, CONV_W - 1, CONV_DIM), F32)
    ssm0 = jnp.zeros((bp, SSD_HEADS, SSD_HEAD_DIM, D_STATE), F32)
    ssd_p, ssm_p = _ssd(z_p, xbc_p, misc_p, conv0, ssm0, conv_w[l], conv_b[l], dt_bias[l], a_log[l], d_skip[l],
                        ssd_norm_w[l], b=bp, nc=tp_len // SSD_CHUNK, cl=SSD_CHUNK, t_valid=SSD_CHUNK)
    tq = 128
    kc = 256 if tp_len % 256 == 0 else 128
    jj = jnp.arange(tq, dtype=I32)[:, None]
    qq = jnp.arange(tq, dtype=I32)[None, :]
    dist_p = jnp.stack([dd * tq + qq - jj for dd in range(3)])
    bias_p = rel_t[:, _rel_bucket(dist_p)] * LOG2E
    attn_p = _attn_prompt(q_p, qi_p, misc_p, k_p, vaug_p, bias_p, b=bp, t=tp_len, tq=tq, kc=kc)
    y_p = _moe_and_final(xp, attn_p, ssd_p, w_out_b, norm2_w[l], w_router, b_router, wg_b, wu_b, wd_b, norm_f_w)

    ms = bs * ts
    xs = x_sample.reshape(ms, d)
    q_s, k_s, v_s, qi_s, misc_s, z_s, xbc_s, _ = _inproj(xs, norm1_w[l], w_perm, _pick(ms, (512, 256, 128)))
    tpad = SUBLANE

    def pad_t(a):
        n = a.shape[1]
        return jnp.pad(a.reshape(bs, ts, n), ((0, 0), (0, tpad - ts), (0, 0))).reshape(bs * tpad, n)

    ssd_s8, ssm_s = _ssd(pad_t(z_s), pad_t(xbc_s), pad_t(misc_s), state_conv[l], state_ssm[l], conv_w[l], conv_b[l],
                         dt_bias[l], a_log[l], d_skip[l], ssd_norm_w[l], b=bs, nc=1, cl=tpad, t_valid=ts)
    ssd_s = ssd_s8.reshape(bs, tpad, D_SSD)[:, :ts].reshape(ms, D_SSD)

    n_pages = page_table.shape[1]
    past = n_pages * PAGE_SIZE
    padq = ((0, 0), (0, 0), (0, tpad - ts), (0, 0))
    qi4 = jnp.pad(qi_s.reshape(bs, ts, IDX_HEADS, IDX_DIM).transpose(0, 2, 1, 3), padq)
    qi4 = qi4.reshape(bs, IDX_HEADS * tpad, IDX_DIM)
    w8 = jnp.pad(misc_s.reshape(bs, ts, LANE)[:, :, MISC_W:MISC_W + IDX_HEADS],
                 ((0, 0), (0, tpad - ts), (0, LANE - IDX_HEADS)))
    qh = jnp.pad(q_s.reshape(bs, ts, N_HEADS, HEAD_DIM).transpose(0, 2, 1, 3), padq)
    qg = qh.reshape(bs, N_KV_HEADS, REP * tpad, HEAD_DIM)
    tt = jnp.tile(jnp.arange(tpad, dtype=I32), N_HEADS)[:, None]
    hh = jnp.repeat(jnp.arange(N_HEADS, dtype=I32), tpad)[:, None]
    jl = jnp.arange(PAGE_SIZE, dtype=I32)[None, :]
    dist_s = jnp.stack([2 * PAGE_SIZE + tt - jl + PAGE_SIZE, PAGE_SIZE + tt - jl, tt - jl])
    bias_s = rel_t[hh[None], _rel_bucket(dist_s)] * LOG2E

    def new_page_t(a, lead):
        a = a.reshape(bs, ts, lead, HEAD_DIM).transpose(0, 2, 3, 1)
        return jnp.pad(a, ((0, 0), (0, 0), (0, 0), (0, PAGE_SIZE - ts)))

    kidx_new_t = new_page_t(misc_s[:, :IDX_DIM], 1)[:, 0]
    o_s = _attn_sample(page_table, qi4, w8, qg, bias_s, kidx_new_t,
                       new_page_t(k_s, N_KV_HEADS), new_page_t(v_s, N_KV_HEADS),
                       cache_kidx[l].transpose(0, 2, 1), cache_k[l].transpose(0, 2, 3, 1),
                       cache_v[l].transpose(0, 2, 3, 1), pps=_pick(n_pages, (16, 8)))
    o_s = o_s.reshape(bs, N_KV_HEADS, REP, tpad, HEAD_DIM)[:, :, :, :ts]
    attn_s = o_s.transpose(0, 3, 1, 2, 4).reshape(ms, N_HEADS * HEAD_DIM)
    y_s = _moe_and_final(xs, attn_s, ssd_s, w_out_b, norm2_w[l], w_router, b_router, wg_b, wu_b, wd_b, norm_f_w)

    def cache_out(k_, b_, t_):
        return k_.reshape(1, b_, t_, N_KV_HEADS, HEAD_DIM)

    return (
        y_p.reshape(bp, tp_len, d),
        y_s.reshape(bs, ts, d),
        cache_out(k_p, bp, tp_len), cache_out(v_p, bp, tp_len),
        misc_p[:, :IDX_DIM].reshape(1, bp, tp_len, IDX_DIM),
        xbc_p.reshape(bp, tp_len, CONV_DIM)[:, tp_len - (CONV_W - 1):][None],
        ssm_p[None],
        cache_out(k_s, bs, ts), cache_out(v_s, bs, ts),
        misc_s[:, :IDX_DIM].reshape(1, bs, ts, IDX_DIM),
        xbc_s.reshape(bs, ts, CONV_DIM)[:, ts - (CONV_W - 1):][None],
        ssm_s[None],
    )
```

```python
import functools
import math

import jax
import jax.numpy as jnp
from jax import lax
from jax.experimental import pallas as pl
from jax.experimental.pallas import tpu as pltpu

F32 = jnp.float32
BF16 = jnp.bfloat16
I32 = jnp.int32
HIGHEST = lax.Precision.HIGHEST

HEAD_DIM = 64
N_HEADS = 8
N_KV_HEADS = 2
REP = N_HEADS // N_KV_HEADS
IDX_HEADS = 4
IDX_DIM = 64
TOPK_MAX = 256
N_BUCKETS = 32
MAX_DISTANCE = 128
D_SSD = 512
SSD_HEADS = 8
SSD_HEAD_DIM = 64
SSD_GROUPS = 2
D_STATE = 64
CONV_W = 4
CONV_DIM = D_SSD + 2 * SSD_GROUPS * D_STATE
SSD_CHUNK = 128
N_EXPERT_GROUPS = 4
EXPERTS_PER_GROUP = 8
N_EXPERTS = N_EXPERT_GROUPS * EXPERTS_PER_GROUP
TOP_K_INNER = 2
PAGE_SIZE = 128
EPS = 1e-6

LANE = 128
SUBLANE = 8
VMEM_LIMIT = 56 * 1024 * 1024
NEG = -1e30
INT_MIN = -(2 ** 31)
LOG2E = math.log2(math.e)
SPLITS = (512, 128, 128, 256, 128, 512, 768)
MISC_W = 64
MISC_DT = 68
VAUG_W = N_KV_HEADS * 2 * HEAD_DIM

NT = (((1,), (1,)), ((), ()))
TN = (((0,), (0,)), ((), ()))


def _params(sem):
    return pltpu.CompilerParams(dimension_semantics=sem, vmem_limit_bytes=VMEM_LIMIT)


def _sigmoid(x):
    return 1.0 / (1.0 + jnp.exp(-x))


def _paired_loop(body, lo, hi, carry):
    odd = lax.rem(hi - lo, 2)
    carry = lax.fori_loop(lo, lo + odd, body, carry)
    first = lo + odd

    def two(j, cr):
        c = first + 2 * j
        return body(c + 1, body(c, cr))

    return lax.fori_loop(0, (hi - first) // 2, two, carry)


def _sort_key(x):
    bits = lax.bitcast_convert_type(x, I32)
    return bits ^ ((bits >> 31) & 0x7FFFFFFF)


def _inproj_body(x_ref, nw_ref, w_ref, *out_refs):
    x = x_ref[...]
    ms = jnp.mean(x * x, axis=-1, keepdims=True)
    xb = ((x * lax.rsqrt(ms + EPS)) * nw_ref[...]).astype(BF16)
    off = 0
    for o_ref, n in zip(out_refs, SPLITS):
        o_ref[...] = jnp.dot(xb, w_ref[:, off:off + n], preferred_element_type=F32)
        off += n
    v = out_refs[2][...]
    ones = jnp.ones((v.shape[0], HEAD_DIM), F32)
    out_refs[len(SPLITS)][...] = jnp.concatenate(
        [piece for g in range(N_KV_HEADS) for piece in (v[:, g * HEAD_DIM:(g + 1) * HEAD_DIM], ones)],
        axis=1).astype(BF16)


def _inproj(x, norm_w, w_perm, tm):
    m, d = x.shape
    n_tot = sum(SPLITS)
    return pl.pallas_call(
        _inproj_body,
        grid=(m // tm,),
        in_specs=[
            pl.BlockSpec((tm, d), lambda i: (i, 0)),
            pl.BlockSpec((1, d), lambda i: (0, 0)),
            pl.BlockSpec((d, n_tot), lambda i: (0, 0)),
        ],
        out_specs=[pl.BlockSpec((tm, n), lambda i: (i, 0)) for n in SPLITS + (VAUG_W,)],
        out_shape=([jax.ShapeDtypeStruct((m, n), F32) for n in SPLITS]
                   + [jax.ShapeDtypeStruct((m, VAUG_W), BF16)]),
        compiler_params=_params(("parallel",)),
        name="inproj",
    )(x, norm_w.reshape(1, d), w_perm)


def _ssd_body(z_ref, xbc_ref, misc_ref, cs_ref, h0_ref, cw_ref, cb_ref, dtb_ref, alog_ref, dsk_ref, nw_ref,
              y_ref, hl_ref, xp_ref, h_ref, yb_ref, *, cl, t_valid, nc):
    c = pl.program_id(1)

    @pl.when(c == 0)
    def _():
        xp_ref[5:8, :] = cs_ref[0]
        h_ref[...] = h0_ref[0]

    xp_ref[8:8 + cl, :] = xbc_ref[...]
    cw = cw_ref[...]
    yc = cb_ref[...]
    for j in range(CONV_W):
        yc = yc + xp_ref[5 + j:5 + j + cl, :] * cw[j:j + 1, :]
    act = yc * _sigmoid(yc)
    xp_ref[5:8, :] = xbc_ref[cl - 3:cl, :]

    xs = act[:, :D_SSD]
    bm = act[:, D_SSD:D_SSD + SSD_GROUPS * D_STATE]
    cm = act[:, D_SSD + SSD_GROUPS * D_STATE:]
    xb = xs.astype(BF16)
    cb16 = cm.astype(BF16)
    bb16 = bm.astype(BF16)

    dtr = misc_ref[:, MISC_DT:MISC_DT + SSD_HEADS] + dtb_ref[...]
    dt = jnp.maximum(dtr, 0.0) + jnp.log1p(jnp.exp(-jnp.abs(dtr)))
    if t_valid < cl:
        dt = jnp.where(lax.broadcasted_iota(I32, (cl, SSD_HEADS), 0) < t_valid, dt, 0.0)
    a_neg = -jnp.exp(alog_ref[...])
    a = dt * a_neg
    ri = lax.broadcasted_iota(I32, (cl, cl), 0)
    ci = lax.broadcasted_iota(I32, (cl, cl), 1)
    tril = ci <= ri
    acs = jnp.dot(jnp.where(tril, 1.0, 0.0), a, precision=HIGHEST, preferred_element_type=F32)
    eye = jnp.where(lax.broadcasted_iota(I32, (SSD_HEADS, SSD_HEADS), 0)
                    == lax.broadcasted_iota(I32, (SSD_HEADS, SSD_HEADS), 1), 1.0, 0.0)
    dt_t = lax.dot_general(eye, dt, NT, precision=HIGHEST, preferred_element_type=F32)
    acs_t = lax.dot_general(eye, acs, NT, precision=HIGHEST, preferred_element_type=F32)
    acs_last = acs[cl - 1:cl, :]
    w_end = jnp.exp(acs_last - acs) * dt
    e_acs = jnp.exp(acs)
    c_dec = jnp.exp(acs_last)
    dsk = dsk_ref[...]

    for g in range(SSD_GROUPS):
        gs = slice(g * D_STATE, (g + 1) * D_STATE)
        cg = cb16[:, gs]
        cbm = lax.dot_general(cg, bb16[:, gs], NT, preferred_element_type=F32)
        for r in range(SSD_HEADS // SSD_GROUPS):
            h = g * (SSD_HEADS // SSD_GROUPS) + r
            hs = slice(h * SSD_HEAD_DIM, (h + 1) * SSD_HEAD_DIM)
            seg = acs[:, h:h + 1] - acs_t[h:h + 1, :]
            decay = jnp.exp(jnp.where(tril, seg, -jnp.inf))
            sc = cbm * decay * dt_t[h:h + 1, :]
            xh = xs[:, hs]
            xhb = xb[:, hs]
            y_diag = jnp.dot(sc.astype(BF16), xhb, preferred_element_type=F32)
            bw = (bm[:, gs] * w_end[:, h:h + 1]).astype(BF16)
            st = lax.dot_general(xhb, bw, TN, preferred_element_type=F32)
            h_in = h_ref[h]
            y_off = lax.dot_general(cg, h_in.astype(BF16), NT, preferred_element_type=F32) * e_acs[:, h:h + 1]
            h_ref[h] = h_in * c_dec[:, h:h + 1] + st
            yb_ref[:, hs] = (y_diag + y_off) + dsk[:, h:h + 1] * xh

    zz = z_ref[...]
    y = yb_ref[...] * (zz * _sigmoid(zz))
    gw = D_SSD // SSD_GROUPS
    for g in range(SSD_GROUPS):
        yg = y[:, g * gw:(g + 1) * gw]
        yg = yg * lax.rsqrt(jnp.mean(yg * yg, axis=-1, keepdims=True) + EPS)
        y_ref[:, g * gw:(g + 1) * gw] = yg * nw_ref[:, g * gw:(g + 1) * gw]

    @pl.when(c == nc - 1)
    def _():
        hl_ref[0] = h_ref[...]


def _ssd(z, xbc, misc, conv_state, h0, conv_w, conv_b, dt_bias, a_log, d_skip, norm_w, *, b, nc, cl, t_valid):
    m = z.shape[0]
    row = lambda bi, ci: (bi * nc + ci, 0)
    const2 = lambda bi, ci: (0, 0)
    body = functools.partial(_ssd_body, cl=cl, t_valid=t_valid, nc=nc)
    return pl.pallas_call(
        body,
        grid=(b, nc),
        in_specs=[
            pl.BlockSpec((cl, D_SSD), row),
            pl.BlockSpec((cl, CONV_DIM), row),
            pl.BlockSpec((cl, LANE), row),
            pl.BlockSpec((1, CONV_W - 1, CONV_DIM), lambda bi, ci: (bi, 0, 0)),
            pl.BlockSpec((1, SSD_HEADS, SSD_HEAD_DIM, D_STATE), lambda bi, ci: (bi, 0, 0, 0)),
            pl.BlockSpec((CONV_W, CONV_DIM), const2),
            pl.BlockSpec((1, CONV_DIM), const2),
            pl.BlockSpec((1, SSD_HEADS), const2),
            pl.BlockSpec((1, SSD_HEADS), const2),
            pl.BlockSpec((1, SSD_HEADS), const2),
            pl.BlockSpec((1, D_SSD), const2),
        ],
        out_specs=[
            pl.BlockSpec((cl, D_SSD), row),
            pl.BlockSpec((1, SSD_HEADS, SSD_HEAD_DIM, D_STATE), lambda bi, ci: (bi, 0, 0, 0)),
        ],
        out_shape=[
            jax.ShapeDtypeStruct((m, D_SSD), F32),
            jax.ShapeDtypeStruct((b, SSD_HEADS, SSD_HEAD_DIM, D_STATE), F32),
        ],
        scratch_shapes=[
            pltpu.VMEM((SUBLANE + cl, CONV_DIM), F32),
            pltpu.VMEM((SSD_HEADS, SSD_HEAD_DIM, D_STATE), F32),
            pltpu.VMEM((cl, D_SSD), F32),
        ],
        compiler_params=_params(("parallel", "arbitrary")),
        name="ssd",
    )(z, xbc, misc, conv_state, h0, conv_w, conv_b.reshape(1, -1), dt_bias.reshape(1, -1),
      a_log.reshape(1, -1), d_skip.reshape(1, -1), norm_w.reshape(1, -1))


def _rel_bucket(dist):
    n = jnp.maximum(dist, 0)
    max_exact = N_BUCKETS // 2
    nf = jnp.maximum(n, 1).astype(F32)
    log_part = jnp.log(nf / max_exact) / math.log(MAX_DISTANCE / max_exact) * (N_BUCKETS - max_exact)
    large = jnp.minimum(max_exact + log_part.astype(I32), N_BUCKETS - 1)
    return jnp.where(n < max_exact, n, large)


def _attn_prompt_body(q_ref, qi_ref, mq_ref, k_ref, v_ref, mk_ref, bias_ref, o_ref,
                      keys_ref, qit_ref, qbd_ref, tri_ref, acc_ref, *, tq, kc, ki, sc_rows, topk):
    i = pl.program_id(1)
    nch = (i * tq + tq + kc - 1) // kc
    nsel = (i * tq + tq + sc_rows - 1) // sc_rows
    n_idx = (i * tq + tq + ki - 1) // ki
    nsub = kc // tq
    n_far = jnp.maximum((i - 1) // nsub, 0)
    qpos = i * tq + lax.broadcasted_iota(I32, (1, tq), 1)

    qi_t = qi_ref[...].T
    qit_ref[...] = jnp.concatenate(
        [qi_t[h * IDX_DIM:(h + 1) * IDX_DIM, :] for h in range(IDX_HEADS)], axis=1).astype(BF16)
    w_rows = mq_ref[...].T[MISC_W:MISC_W + IDX_HEADS, :] * (IDX_DIM ** -0.5 * IDX_HEADS ** -0.5)
    q_t = q_ref[...].T * (HEAD_DIM ** -0.5 * LOG2E)
    zeros = jnp.zeros((HEAD_DIM, tq), F32)
    cols = []
    for g in range(N_KV_HEADS):
        for r in range(REP):
            h = g * REP + r
            blk = q_t[h * HEAD_DIM:(h + 1) * HEAD_DIM, :]
            cols.append(jnp.concatenate([blk, zeros] if g == 0 else [zeros, blk], axis=0))
    qbd_ref[...] = jnp.concatenate(cols, axis=1).astype(BF16)
    tri_ref[...] = jnp.where(lax.broadcasted_iota(I32, (kc, kc), 1) < lax.broadcasted_iota(I32, (kc, kc), 0),
                             1.0, 0.0).astype(BF16)

    def idx_chunk(c, carry):
        k0 = pl.multiple_of(c * ki, ki)
        kidx = mk_ref[pl.ds(k0, ki), 0:IDX_DIM].astype(BF16)
        s = jnp.dot(kidx, qit_ref[...], preferred_element_type=F32)
        sc = jnp.maximum(s[:, 0:tq], 0.0) * w_rows[0:1, :]
        for h in range(1, IDX_HEADS):
            sc = sc + jnp.maximum(s[:, h * tq:(h + 1) * tq], 0.0) * w_rows[h:h + 1, :]
        kpos = k0 + lax.broadcasted_iota(I32, (ki, 1), 0)
        sc = jnp.where(kpos <= qpos, sc, -jnp.inf)
        keys_ref[pl.ds(k0, ki), :] = _sort_key(sc)
        return carry

    _paired_loop(idx_chunk, 0, n_idx, 0)

    def fill_chunk(c, carry):
        keys_ref[pl.ds(pl.multiple_of(c * ki, ki), ki), :] = jnp.full((ki, tq), INT_MIN, I32)
        return carry

    lax.fori_loop(n_idx, nsel * (sc_rows // ki), fill_chunk, 0)

    n_acc = 4
    acc_rows = n_acc * SUBLANE

    def count(pred_fn):
        def ch(c, part):
            k0 = pl.multiple_of(c * sc_rows, sc_rows)
            hit = jnp.where(pred_fn(keys_ref[pl.ds(k0, sc_rows), :]), 1.0, 0.0)
            return part + jnp.sum(hit.reshape(sc_rows // acc_rows, acc_rows, tq), axis=0)
        part = lax.fori_loop(0, nsel, ch, jnp.zeros((acc_rows, tq), F32))
        return jnp.sum(part, axis=0, keepdims=True)

    def bit_pass(bi, thr):
        cand = thr + lax.shift_left(jnp.int32(1), 31 - bi)
        cnt = count(lambda kk: kk >= cand)
        return jnp.where(cnt >= topk, cand, thr)

    thr = lax.fori_loop(0, 32, bit_pass, jnp.full((1, tq), INT_MIN, I32))
    need = topk - count(lambda kk: kk > thr)

    acc_ref[...] = jnp.zeros(acc_ref.shape, F32)

    def att_logits(far, c, ties_before):
        k0 = pl.multiple_of(c * kc, kc)
        kk = keys_ref[pl.ds(k0, kc), :]
        eq = kk == thr
        eqf = jnp.where(eq, 1.0, 0.0)
        rank = jnp.dot(tri_ref[...], eqf.astype(BF16), preferred_element_type=F32) + ties_before
        sel = (kk > thr) | (eq & (rank < need))
        if not far:
            sel = sel & (k0 + lax.broadcasted_iota(I32, (kc, 1), 0) <= qpos)
        kb = k_ref[pl.ds(k0, kc), :].astype(BF16)
        logits = jnp.dot(kb, qbd_ref[...], preferred_element_type=F32)
        return ties_before + jnp.sum(eqf, axis=0, keepdims=True), sel, logits

    def att_softmax(far, c, sel, logits, ms, ls):
        k0 = pl.multiple_of(c * kc, kc)
        ms_new, ls_new = [], []
        for g in range(N_KV_HEADS):
            ps, alphas = [], []
            for r in range(REP):
                h = g * REP + r
                cs = slice(h * tq, (h + 1) * tq)
                if far:
                    shift = bias_ref[h, 2, 0:1, :]
                    lg = jnp.where(sel, logits[:, cs], NEG)
                    m_new = jnp.maximum(ms[h], jnp.max(lg, axis=0, keepdims=True) + shift)
                    p = jnp.exp2(lg - (m_new - shift))
                else:
                    bias = jnp.concatenate(
                        [bias_ref[h, jnp.clip(i - (c * nsub + s), 0, 2)] for s in range(nsub)], axis=0)
                    lg = jnp.where(sel, logits[:, cs] + bias, NEG)
                    m_new = jnp.maximum(ms[h], jnp.max(lg, axis=0, keepdims=True))
                    p = jnp.exp2(lg - m_new)
                alpha = jnp.exp2(ms[h] - m_new)
                ms_new.append(m_new)
                ps.append(p.astype(BF16))
                alphas.append(alpha)
            vb = v_ref[pl.ds(k0, kc), g * 2 * HEAD_DIM:(g + 1) * 2 * HEAD_DIM]
            pv = lax.dot_general(vb, jnp.concatenate(ps, axis=1), TN, preferred_element_type=F32)
            acc_ref[g] = jnp.concatenate(alphas, axis=1) * acc_ref[g] + pv[0:HEAD_DIM]
            for r in range(REP):
                ls_new.append(alphas[r] * ls[g * REP + r] + pv[HEAD_DIM:HEAD_DIM + 1, r * tq:(r + 1) * tq])
        return tuple(ms_new), tuple(ls_new)

    def att_trips(far, cs, carry):
        ties, ms, ls = carry
        staged = []
        for c in cs:
            ties, sel, logits = att_logits(far, c, ties)
            staged.append((c, sel, logits))
        for c, sel, logits in staged:
            ms, ls = att_softmax(far, c, sel, logits, ms, ls)
        return ties, ms, ls

    def att_loop(far, lo, hi, carry):
        odd = lax.rem(hi - lo, 2)
        carry = lax.fori_loop(lo, lo + odd, lambda c, cr: att_trips(far, [c], cr), carry)
        first = lo + odd
        return lax.fori_loop(0, (hi - first) // 2,
                             lambda j, cr: att_trips(far, [first + 2 * j, first + 2 * j + 1], cr), carry)

    carry = (jnp.zeros((1, tq), F32),
             tuple(jnp.full((1, tq), NEG, F32) for _ in range(N_HEADS)),
             tuple(jnp.zeros((1, tq), F32) for _ in range(N_HEADS)))
    carry = att_loop(True, 0, n_far, carry)
    _, _, ls = att_loop(False, n_far, nch, carry)

    blocks = []
    for g in range(N_KV_HEADS):
        for r in range(REP):
            blocks.append(acc_ref[g][:, r * tq:(r + 1) * tq] * (1.0 / ls[g * REP + r]))
    o_ref[...] = jnp.concatenate(blocks, axis=0).T


def _attn_prompt(q, qidx, misc, k, vaug, bias_t, *, b, t, tq, kc):
    m = q.shape[0]
    nq = t // tq
    topk = min(TOPK_MAX, t // 4)
    qrow = lambda bi, qi: (bi * nq + qi, 0)
    brow = lambda bi, qi: (bi, 0)
    sc_rows = _pick(t, (1024, 512, 256))
    ki = _pick(t, (512, 256, 128))
    body = functools.partial(_attn_prompt_body, tq=tq, kc=kc, ki=ki, sc_rows=sc_rows, topk=topk)
    return pl.pallas_call(
        body,
        grid=(b, nq),
        in_specs=[
            pl.BlockSpec((tq, N_HEADS * HEAD_DIM), qrow),
            pl.BlockSpec((tq, IDX_HEADS * IDX_DIM), qrow),
            pl.BlockSpec((tq, LANE), qrow),
            pl.BlockSpec((t, LANE), brow),
            pl.BlockSpec((t, VAUG_W), brow),
            pl.BlockSpec((t, LANE), brow),
            pl.BlockSpec((N_HEADS, 3, tq, tq), lambda bi, qi: (0, 0, 0, 0)),
        ],
        out_specs=pl.BlockSpec((tq, N_HEADS * HEAD_DIM), qrow),
        out_shape=jax.ShapeDtypeStruct((m, N_HEADS * HEAD_DIM), F32),
        scratch_shapes=[
            pltpu.VMEM((t, tq), I32),
            pltpu.VMEM((IDX_DIM, IDX_HEADS * tq), BF16),
            pltpu.VMEM((N_KV_HEADS * HEAD_DIM, N_HEADS * tq), BF16),
            pltpu.VMEM((kc, kc), BF16),
            pltpu.VMEM((N_KV_HEADS, HEAD_DIM, REP * tq), F32),
        ],
        compiler_params=_params(("parallel", "arbitrary")),
        name="attn_prompt",
    )(q, qidx, misc, k, vaug, misc, bias_t)


def _attn_sample_body(pt_ref, qi_ref, w_ref, qg_ref, bias_ref, kinew_ref, knew_ref, vnew_ref, *rest,
                      pps, n_pages, past, topk):
    kidx_refs = rest[0:pps]
    k_refs = rest[pps:2 * pps]
    v_refs = rest[2 * pps:3 * pps]
    o_ref = rest[3 * pps]
    keys_ref, thr_ref, need_ref, ties_ref, m_ref, l_ref, acc_ref = rest[3 * pps + 1:]
    del pt_ref
    j = pl.program_id(1)
    ng = n_pages // pps
    tp = SUBLANE
    grows = REP * tp
    qpos = past + lax.broadcasted_iota(I32, (tp, 1), 0)
    lane = lax.broadcasted_iota(I32, (1, PAGE_SIZE), 1)

    def score_page(kidx_t, slot, kpos0):
        s = jnp.dot(qi_ref[0].astype(BF16), kidx_t.astype(BF16), preferred_element_type=F32)
        w = w_ref[0] * (IDX_DIM ** -0.5 * IDX_HEADS ** -0.5)
        sc = jnp.maximum(s[0:tp], 0.0) * w[:, 0:1]
        for h in range(1, IDX_HEADS):
            sc = sc + jnp.maximum(s[h * tp:(h + 1) * tp], 0.0) * w[:, h:h + 1]
        sc = jnp.where(kpos0 + lane <= qpos, sc, -jnp.inf)
        keys_ref[slot] = _sort_key(sc)

    @pl.when(j < ng)
    def _():
        for i in range(pps):
            page = j * pps + i
            score_page(kidx_refs[i][0], page, page * PAGE_SIZE)

    @pl.when(j == ng - 1)
    def _():
        score_page(kinew_ref[0], n_pages, past)
        for s in range(n_pages + 1, n_pages + pps):
            keys_ref[s] = jnp.full((tp, PAGE_SIZE), INT_MIN, I32)

        n_acc = 8
        n_slots = keys_ref.shape[0]

        def count(pred_fn):
            hit = jnp.where(pred_fn(keys_ref[...]), 1.0, 0.0)
            part = jnp.sum(hit.reshape(n_slots // n_acc, n_acc, tp, PAGE_SIZE), axis=0)
            return jnp.sum(jnp.sum(part, axis=0), axis=1, keepdims=True)

        def bit_pass(bi, thr):
            cand = thr + lax.shift_left(jnp.int32(1), 31 - bi)
            return jnp.where(count(lambda kk: kk >= cand) >= topk, cand, thr)

        thr = lax.fori_loop(0, 32, bit_pass, jnp.full((tp, 1), INT_MIN, I32))
        thr_ref[...] = thr
        need_ref[...] = topk - count(lambda kk: kk > thr)
        ties_ref[...] = jnp.zeros(ties_ref.shape, F32)
        m_ref[...] = jnp.full(m_ref.shape, NEG, F32)
        l_ref[...] = jnp.zeros(l_ref.shape, F32)
        acc_ref[...] = jnp.zeros(acc_ref.shape, F32)

    def attend(pages):
        thr = thr_ref[...]
        need = need_ref[...]
        tri = jnp.where(lax.broadcasted_iota(I32, (PAGE_SIZE, PAGE_SIZE), 0)
                        < lax.broadcasted_iota(I32, (PAGE_SIZE, PAGE_SIZE), 1), 1.0, 0.0).astype(BF16)
        qg = (qg_ref[0] * (HEAD_DIM ** -0.5 * LOG2E)).astype(BF16)
        ties = ties_ref[...]
        lgs, vbs = [], []
        for kp, vp, slot, kpos0, bias in pages:
            kk = keys_ref[slot]
            eq = kk == thr
            eqf = jnp.where(eq, 1.0, 0.0)
            rank = jnp.dot(eqf.astype(BF16), tri, preferred_element_type=F32) + ties
            sel = ((kk > thr) | (eq & (rank < need))) & (kpos0 + lane <= qpos)
            ties = ties + jnp.sum(eqf, axis=1, keepdims=True)
            self = jnp.where(sel, 1.0, 0.0)
            sel_rows = jnp.concatenate([self] * N_HEADS, axis=0) > 0.5
            kb = kp.astype(BF16)
            lg = jnp.concatenate([jnp.dot(qg[g], kb[g], preferred_element_type=F32) for g in range(N_KV_HEADS)],
                                 axis=0)
            lgs.append(jnp.where(sel_rows, lg + bias, NEG))
            vbs.append(vp.astype(BF16))
        ties_ref[...] = ties
        lg = jnp.concatenate(lgs, axis=1)
        m_old = m_ref[...]
        m_new = jnp.maximum(m_old, jnp.max(lg, axis=1, keepdims=True))
        p = jnp.exp2(lg - m_new)
        alpha = jnp.exp2(m_old - m_new)
        l_ref[...] = alpha * l_ref[...] + jnp.sum(p, axis=1, keepdims=True)
        m_ref[...] = m_new
        pb = p.astype(BF16)
        pvs = []
        for g in range(N_KV_HEADS):
            pg = pb[g * grows:(g + 1) * grows]
            pv = None
            for n in range(len(pages)):
                d = lax.dot_general(pg[:, n * PAGE_SIZE:(n + 1) * PAGE_SIZE], vbs[n][g], NT,
                                    preferred_element_type=F32)
                pv = d if pv is None else pv + d
            pvs.append(pv)
        acc_ref[...] = alpha * acc_ref[...] + jnp.concatenate(pvs, axis=0)

    @pl.when(j >= ng)
    def _():
        pages = []
        for i in range(pps):
            page = (j - ng) * pps + i
            bias = jnp.where(page == n_pages - 1, bias_ref[1], bias_ref[0])
            pages.append((k_refs[i][0], v_refs[i][0], page, page * PAGE_SIZE, bias))
        attend(pages)

    @pl.when(j == 2 * ng - 1)
    def _():
        attend([(knew_ref[0], vnew_ref[0], n_pages, past, bias_ref[2])])
        o_ref[0] = acc_ref[...] / l_ref[...]


def _attn_sample(page_table, qi4, w8, qg, bias_s, kidx_new_t, k_new_t, v_new_t, pool_kidx_t, pool_k_t, pool_v_t,
                 *, pps):
    b, n_pages = page_table.shape
    past = n_pages * PAGE_SIZE
    t_new = 4
    topk = min(TOPK_MAX, (past + t_new) // 4)
    ng = n_pages // pps
    tp = SUBLANE
    rows = N_HEADS * tp
    per_b = lambda bi, j, pt: (bi, 0, 0)
    per_b4 = lambda bi, j, pt: (bi, 0, 0, 0)

    def kidx_map(i):
        return lambda bi, j, pt: (pt[bi * n_pages + jnp.minimum(j, ng - 1) * pps + i], 0, 0)

    def kv_map(i):
        return lambda bi, j, pt: (pt[bi * n_pages + jnp.maximum(j - ng, 0) * pps + i], 0, 0, 0)

    kv_block = (1, N_KV_HEADS, HEAD_DIM, PAGE_SIZE)
    in_specs = [
        pl.BlockSpec((1, IDX_HEADS * tp, IDX_DIM), per_b),
        pl.BlockSpec((1, tp, LANE), per_b),
        pl.BlockSpec((1, N_KV_HEADS, REP * tp, HEAD_DIM), per_b4),
        pl.BlockSpec((3, rows, PAGE_SIZE), lambda bi, j, pt: (0, 0, 0)),
        pl.BlockSpec((1, IDX_DIM, PAGE_SIZE), per_b),
        pl.BlockSpec(kv_block, per_b4),
        pl.BlockSpec(kv_block, per_b4),
    ]
    in_specs += [pl.BlockSpec((1, IDX_DIM, PAGE_SIZE), kidx_map(i)) for i in range(pps)]
    in_specs += [pl.BlockSpec(kv_block, kv_map(i)) for i in range(pps)]
    in_specs += [pl.BlockSpec(kv_block, kv_map(i)) for i in range(pps)]
    body = functools.partial(_attn_sample_body, pps=pps, n_pages=n_pages, past=past, topk=topk)
    grid_spec = pltpu.PrefetchScalarGridSpec(
        num_scalar_prefetch=1,
        grid=(b, 2 * ng),
        in_specs=in_specs,
        out_specs=pl.BlockSpec((1, rows, HEAD_DIM), per_b),
        scratch_shapes=[
            pltpu.VMEM((n_pages + pps, tp, PAGE_SIZE), I32),
            pltpu.VMEM((tp, 1), I32),
            pltpu.VMEM((tp, 1), F32),
            pltpu.VMEM((tp, 1), F32),
            pltpu.VMEM((rows, 1), F32),
            pltpu.VMEM((rows, 1), F32),
            pltpu.VMEM((rows, HEAD_DIM), F32),
        ],
    )
    return pl.pallas_call(
        body,
        grid_spec=grid_spec,
        out_shape=jax.ShapeDtypeStruct((b, rows, HEAD_DIM), F32),
        compiler_params=_params(("parallel", "arbitrary")),
        name="attn_sample",
    )(page_table.reshape(-1), qi4, w8, qg, bias_s, kidx_new_t, k_new_t, v_new_t,
      *([pool_kidx_t] * pps), *([pool_k_t] * pps), *([pool_v_t] * pps))


def _outproj_body(h_ref, a_ref, s_ref, wo_ref, n2_ref, wr_ref, br_ref, h1_ref, xn_ref, ri_ref, rw_ref):
    d_att = a_ref.shape[1]
    mix = (jnp.dot(a_ref[...].astype(BF16), wo_ref[0:d_att, :], preferred_element_type=F32)
           + jnp.dot(s_ref[...].astype(BF16), wo_ref[d_att:, :], preferred_element_type=F32))
    h1 = h_ref[...] + mix
    h1_ref[...] = h1
    ms = jnp.mean(h1 * h1, axis=-1, keepdims=True)
    xn = (h1 * lax.rsqrt(ms + EPS)) * n2_ref[...]
    xn_ref[...] = xn
    logits = jnp.dot(xn.astype(BF16), wr_ref[...], preferred_element_type=F32) + br_ref[...]
    tm = logits.shape[0]
    lane = lax.broadcasted_iota(I32, (tm, LANE), 1).astype(F32)
    ninf = -jnp.inf
    gl = jnp.where(lane < N_EXPERT_GROUPS, logits, ninf)
    gmax = jnp.max(gl, axis=-1, keepdims=True)
    grp = jnp.min(jnp.where(gl == gmax, lane, float(LANE)), axis=-1, keepdims=True)
    g_w = 1.0 / jnp.sum(jnp.exp(gl - gmax), axis=-1, keepdims=True)
    lo = N_EXPERT_GROUPS + grp * EXPERTS_PER_GROUP
    el = jnp.where((lane >= lo) & (lane < lo + EXPERTS_PER_GROUP), logits, ninf)
    v1 = jnp.max(el, axis=-1, keepdims=True)
    i1 = jnp.min(jnp.where(el == v1, lane, float(LANE)), axis=-1, keepdims=True)
    el2 = jnp.where(lane == i1, ninf, el)
    v2 = jnp.max(el2, axis=-1, keepdims=True)
    i2 = jnp.min(jnp.where(el2 == v2, lane, float(LANE)), axis=-1, keepdims=True)
    e2 = jnp.exp(v2 - v1)
    den = 1.0 + e2
    ids = jnp.where(lane == 0.0, i1 - N_EXPERT_GROUPS, jnp.where(lane == 1.0, i2 - N_EXPERT_GROUPS, 0.0))
    ri_ref[...] = ids.astype(I32)
    rw_ref[...] = jnp.where(lane == 0.0, g_w * (1.0 / den), jnp.where(lane == 1.0, g_w * (e2 / den), 0.0))


def _outproj(h, attn, ssd, w_out, norm2_w, w_router, b_router, tm):
    m, d = h.shape
    row = lambda i: (i, 0)
    const = lambda i: (0, 0)
    return pl.pallas_call(
        _outproj_body,
        grid=(m // tm,),
        in_specs=[
            pl.BlockSpec((tm, d), row),
            pl.BlockSpec((tm, attn.shape[1]), row),
            pl.BlockSpec((tm, ssd.shape[1]), row),
            pl.BlockSpec(w_out.shape, const),
            pl.BlockSpec((1, d), const),
            pl.BlockSpec((d, LANE), const),
            pl.BlockSpec((1, LANE), const),
        ],
        out_specs=[
            pl.BlockSpec((tm, d), row),
            pl.BlockSpec((tm, d), row),
            pl.BlockSpec((tm, LANE), row),
            pl.BlockSpec((tm, LANE), row),
        ],
        out_shape=[
            jax.ShapeDtypeStruct((m, d), F32),
            jax.ShapeDtypeStruct((m, d), F32),
            jax.ShapeDtypeStruct((m, LANE), I32),
            jax.ShapeDtypeStruct((m, LANE), F32),
        ],
        compiler_params=_params(("parallel",)),
        name="outproj_router",
    )(h, attn, ssd, w_out, norm2_w.reshape(1, d), w_router, b_router)


def _start_rows(idx_ref, n, src_hbm, dst, sem, idx_of):
    for r in range(n):
        pltpu.make_async_copy(src_hbm.at[pl.ds(idx_of(idx_ref, r), 1), :], dst.at[pl.ds(r, 1), :], sem).start()


def _wait_rows(n, src_hbm, dst, sem):
    pltpu.make_async_copy(src_hbm.at[pl.ds(0, n), :], dst, sem).wait()


def _expert_body(blk_e_ref, tok_ref, tok_next_ref, roww_ref, x_hbm, wg_ref, wu_ref, wd_ref, o_ref, xbuf, sem):
    del blk_e_ref
    i = pl.program_id(0)
    last = pl.num_programs(0) - 1
    slot = lax.rem(i, 2)
    other = 1 - slot
    rb = xbuf.shape[1]
    tok_of = lambda ref, r: ref[0, 0, r]

    @pl.when(i == 0)
    def _():
        _start_rows(tok_ref, rb, x_hbm, xbuf.at[0], sem.at[0], tok_of)

    _wait_rows(rb, x_hbm, xbuf.at[slot], sem.at[slot])
    xb = xbuf[slot].astype(BF16)
    wg, wu, wd = wg_ref[0], wu_ref[0], wd_ref[0]
    _start_rows(tok_next_ref, rb, x_hbm, xbuf.at[other], sem.at[other], tok_of)
    hg = jnp.dot(xb, wg, preferred_element_type=F32)
    hu = jnp.dot(xb, wu, preferred_element_type=F32)
    hd = ((hg * _sigmoid(hg)) * hu).astype(BF16)
    y = jnp.dot(hd, wd, preferred_element_type=F32)
    o_ref[...] = y * roww_ref[...]

    @pl.when(i == last)
    def _():
        _wait_rows(rb, x_hbm, xbuf.at[other], sem.at[other])


def _experts(blk_e, row_tok, row_w, xn, w_gate, w_up, w_down, rb):
    cap = row_tok.shape[0]
    n_blk = cap // rb
    d = xn.shape[1]
    de = w_gate.shape[2]
    tok3 = row_tok.reshape(n_blk, 1, rb)
    grid_spec = pltpu.PrefetchScalarGridSpec(
        num_scalar_prefetch=1,
        grid=(n_blk,),
        in_specs=[
            pl.BlockSpec((1, 1, rb), lambda i, be: (i, 0, 0), memory_space=pltpu.SMEM),
            pl.BlockSpec((1, 1, rb), lambda i, be: (jnp.minimum(i + 1, n_blk - 1), 0, 0), memory_space=pltpu.SMEM),
            pl.BlockSpec((rb, 1), lambda i, be: (i, 0)),
            pl.BlockSpec(memory_space=pl.ANY),
            pl.BlockSpec((1, d, de), lambda i, be: (be[i], 0, 0)),
            pl.BlockSpec((1, d, de), lambda i, be: (be[i], 0, 0)),
            pl.BlockSpec((1, de, d), lambda i, be: (be[i], 0, 0)),
        ],
        out_specs=pl.BlockSpec((rb, d), lambda i, be: (i, 0)),
        scratch_shapes=[pltpu.VMEM((2, rb, d), F32), pltpu.SemaphoreType.DMA((2,))],
    )
    return pl.pallas_call(
        _expert_body,
        grid_spec=grid_spec,
        out_shape=jax.ShapeDtypeStruct((cap, d), F32),
        compiler_params=_params(("arbitrary",)),
        name="experts",
    )(blk_e, tok3, tok3, row_w.reshape(cap, 1), xn, w_gate, w_up, w_down)


def _combine_body(pos_ref, pos_next_ref, h1_ref, yw_hbm, nf_ref, o_ref, buf, sem):
    i = pl.program_id(0)
    last = pl.num_programs(0) - 1
    slot = lax.rem(i, 2)
    other = 1 - slot
    tm = h1_ref.shape[0]

    def start(ref, s):
        for kk in range(TOP_K_INNER):
            _start_rows(ref, tm, yw_hbm, buf.at[s, kk], sem.at[s, kk],
                        lambda rf, r, kk=kk: rf[0, 0, TOP_K_INNER * r + kk])

    def wait(s):
        for kk in range(TOP_K_INNER):
            _wait_rows(tm, yw_hbm, buf.at[s, kk], sem.at[s, kk])

    @pl.when(i == 0)
    def _():
        start(pos_ref, 0)

    start(pos_next_ref, other)
    wait(slot)
    h = h1_ref[...] + (buf[slot, 0] + buf[slot, 1])
    ms = jnp.mean(h * h, axis=-1, keepdims=True)
    o_ref[...] = (h * lax.rsqrt(ms + EPS)) * nf_ref[...]

    @pl.when(i == last)
    def _():
        wait(other)


def _combine(pos, h1, yw, norm_f_w, tm):
    m, d = h1.shape
    nt = m // tm
    pos3 = pos.reshape(nt, 1, TOP_K_INNER * tm)
    return pl.pallas_call(
        _combine_body,
        grid=(nt,),
        in_specs=[
            pl.BlockSpec((1, 1, TOP_K_INNER * tm), lambda i: (i, 0, 0), memory_space=pltpu.SMEM),
            pl.BlockSpec((1, 1, TOP_K_INNER * tm), lambda i: (jnp.minimum(i + 1, nt - 1), 0, 0),
                         memory_space=pltpu.SMEM),
            pl.BlockSpec((tm, d), lambda i: (i, 0)),
            pl.BlockSpec(memory_space=pl.ANY),
            pl.BlockSpec((1, d), lambda i: (0, 0)),
        ],
        out_specs=pl.BlockSpec((tm, d), lambda i: (i, 0)),
        out_shape=jax.ShapeDtypeStruct((m, d), F32),
        scratch_shapes=[pltpu.VMEM((2, TOP_K_INNER, tm, d), F32), pltpu.SemaphoreType.DMA((2, TOP_K_INNER))],
        compiler_params=_params(("arbitrary",)),
        name="combine",
    )(pos3, pos3, h1, yw, norm_f_w.reshape(1, d))


def _permute_w_in(w_in):
    d = w_in.shape[0]
    c = 0
    parts = {}
    for name, n in (("q", 512), ("k", 128), ("v", 128), ("qi", 256), ("ki", 64), ("wi", 4),
                    ("z", 512), ("xbc", 768), ("dt", 8)):
        parts[name] = w_in[:, c:c + n]
        c += n
    pad = jnp.zeros((d, LANE - 64 - 4 - 8), w_in.dtype)
    return jnp.concatenate([parts["q"], parts["k"], parts["v"], parts["qi"], parts["ki"], parts["wi"],
                            parts["dt"], pad, parts["z"], parts["xbc"]], axis=1).astype(BF16)


def _route_tables(expert, gate, rb):
    a = expert.shape[0] * TOP_K_INNER
    e_flat = expert.reshape(-1)
    pid = jnp.arange(a, dtype=I32)
    _, order, gw_s = lax.sort((e_flat, pid, gate.reshape(-1)), num_keys=1, is_stable=True)
    eids = jnp.arange(N_EXPERTS, dtype=I32)
    sizes = jnp.sum((e_flat[:, None] == eids[None, :]).astype(I32), axis=0)
    gends = jnp.cumsum(sizes)
    padded = (sizes + rb - 1) // rb * rb
    pends = jnp.cumsum(padded)
    gap = padded - sizes
    cap = -(-(a + N_EXPERTS * (rb - 1)) // rb) * rb
    n_blk = cap // rb
    dest = pid + jnp.sum(jnp.where(gends[None, :] <= pid[:, None], gap[None, :], 0), axis=1)
    r = jnp.arange(cap, dtype=I32)
    done = pends[None, :] <= r[:, None]
    src = r - jnp.sum(jnp.where(done, gap[None, :], 0), axis=1)
    e_r = jnp.minimum(jnp.sum(done.astype(I32), axis=1), N_EXPERTS - 1)
    valid = src < jnp.sum(jnp.where(eids[None, :] == e_r[:, None], gends[None, :], 0), axis=1)
    src = jnp.minimum(src, a - 1)
    row_tok = jnp.where(valid, order[src] // TOP_K_INNER, 0)
    row_w = jnp.where(valid, gw_s[src], 0.0)
    _, pos = lax.sort((order, dest), num_keys=1)
    blk_e = jnp.minimum(jnp.sum((pends[None, :] <= (jnp.arange(n_blk, dtype=I32) * rb)[:, None]).astype(I32), axis=1),
                        N_EXPERTS - 1)
    return blk_e, row_tok, row_w, pos


def _pick(n, prefs):
    for p in prefs:
        if n % p == 0:
            return p
    return n


def _moe_and_final(h, attn, ssd, w_out, norm2_w, w_router, b_router, w_gate, w_up, w_down, norm_f_w):
    m = h.shape[0]
    tm = _pick(m, (512, 256, 128))
    h1, xn, ri, rw = _outproj(h, attn, ssd, w_out, norm2_w, w_router, b_router, tm)
    rb = 256 if m >= 4096 else 128
    blk_e, row_tok, row_w, pos = _route_tables(ri[:, :TOP_K_INNER], rw[:, :TOP_K_INNER], rb)
    yw = _experts(blk_e, row_tok, row_w, xn, w_gate, w_up, w_down, rb)
    return _combine(pos, h1, yw, norm_f_w, _pick(m, (256, 128)))


def kernel(x_prompt, x_sample, cache_k, cache_v, cache_kidx, state_conv, state_ssm, page_table, rel_bias,
           norm1_w, w_in, conv_w, conv_b, dt_bias, a_log, d_skip, ssd_norm_w, w_out, norm2_w,
           w_router_group, b_router_group, w_router_expert, b_router_expert, w_gate, w_up, w_down, norm_f_w):
    bp, tp_len, d = x_prompt.shape
    bs, ts, _ = x_sample.shape
    depth = w_in.shape[0]
    assert depth == 1 and ts == 4 and tp_len % SSD_CHUNK == 0
    l = 0

    w_perm = _permute_w_in(w_in[l])
    w_out_b = w_out[l].astype(BF16)
    n_r = N_EXPERT_GROUPS + N_EXPERTS
    w_router = jnp.concatenate([w_router_group[l], w_router_expert[l],
                                jnp.zeros((d, LANE - n_r), F32)], axis=1).astype(BF16)
    b_router = jnp.concatenate([b_router_group[l], b_router_expert[l], jnp.zeros((LANE - n_r,), F32)]).reshape(1, LANE)
    wg_b, wu_b, wd_b = w_gate[l].astype(BF16), w_up[l].astype(BF16), w_down[l].astype(BF16)
    rel_t = rel_bias.astype(F32).T

    mp = bp * tp_len
    xp = x_prompt.reshape(mp, d)
    q_p, k_p, v_p, qi_p, misc_p, z_p, xbc_p, vaug_p = _inproj(xp, norm1_w[l], w_perm, _pick(mp, (512, 256, 128)))
    conv0 = jnp.zeros((bp, CONV_W - 1, CONV_DIM), F32)
    ssm0 = jnp.zeros((bp, SSD_HEADS, SSD_HEAD_DIM, D_STATE), F32)
    ssd_p, ssm_p = _ssd(z_p, xbc_p, misc_p, conv0, ssm0, conv_w[l], conv_b[l], dt_bias[l], a_log[l], d_skip[l],
                        ssd_norm_w[l], b=bp, nc=tp_len // SSD_CHUNK, cl=SSD_CHUNK, t_valid=SSD_CHUNK)
    tq = 128
    kc = 256 if tp_len % 256 == 0 else 128
    jj = jnp.arange(tq, dtype=I32)[:, None]
    qq = jnp.arange(tq, dtype=I32)[None, :]
    dist_p = jnp.stack([dd * tq + qq - jj for dd in range(3)])
    bias_p = rel_t[:, _rel_bucket(dist_p)] * LOG2E
    attn_p = _attn_prompt(q_p, qi_p, misc_p, k_p, vaug_p, bias_p, b=bp, t=tp_len, tq=tq, kc=kc)
    y_p = _moe_and_final(xp, attn_p, ssd_p, w_out_b, norm2_w[l], w_router, b_router, wg_b, wu_b, wd_b, norm_f_w)

    ms = bs * ts
    xs = x_sample.reshape(ms, d)
    q_s, k_s, v_s, qi_s, misc_s, z_s, xbc_s, _ = _inproj(xs, norm1_w[l], w_perm, _pick(ms, (512, 256, 128)))
    tpad = SUBLANE

    def pad_t(a):
        n = a.shape[1]
        return jnp.pad(a.reshape(bs, ts, n), ((0, 0), (0, tpad - ts), (0, 0))).reshape(bs * tpad, n)

    ssd_s8, ssm_s = _ssd(pad_t(z_s), pad_t(xbc_s), pad_t(misc_s), state_conv[l], state_ssm[l], conv_w[l], conv_b[l],
                         dt_bias[l], a_log[l], d_skip[l], ssd_norm_w[l], b=bs, nc=1, cl=tpad, t_valid=ts)
    ssd_s = ssd_s8.reshape(bs, tpad, D_SSD)[:, :ts].reshape(ms, D_SSD)

    n_pages = page_table.shape[1]
    past = n_pages * PAGE_SIZE
    padq = ((0, 0), (0, 0), (0, tpad - ts), (0, 0))
    qi4 = jnp.pad(qi_s.reshape(bs, ts, IDX_HEADS, IDX_DIM).transpose(0, 2, 1, 3), padq)
    qi4 = qi4.reshape(bs, IDX_HEADS * tpad, IDX_DIM)
    w8 = jnp.pad(misc_s.reshape(bs, ts, LANE)[:, :, MISC_W:MISC_W + IDX_HEADS],
                 ((0, 0), (0, tpad - ts), (0, LANE - IDX_HEADS)))
    qh = jnp.pad(q_s.reshape(bs, ts, N_HEADS, HEAD_DIM).transpose(0, 2, 1, 3), padq)
    qg = qh.reshape(bs, N_KV_HEADS, REP * tpad, HEAD_DIM)
    tt = jnp.tile(jnp.arange(tpad, dtype=I32), N_HEADS)[:, None]
    hh = jnp.repeat(jnp.arange(N_HEADS, dtype=I32), tpad)[:, None]
    jl = jnp.arange(PAGE_SIZE, dtype=I32)[None, :]
    dist_s = jnp.stack([2 * PAGE_SIZE + tt - jl + PAGE_SIZE, PAGE_SIZE + tt - jl, tt - jl])
    bias_s = rel_t[hh[None], _rel_bucket(dist_s)] * LOG2E

    def new_page_t(a, lead):
        a = a.reshape(bs, ts, lead, HEAD_DIM).transpose(0, 2, 3, 1)
        return jnp.pad(a, ((0, 0), (0, 0), (0, 0), (0, PAGE_SIZE - ts)))

    kidx_new_t = new_page_t(misc_s[:, :IDX_DIM], 1)[:, 0]
    o_s = _attn_sample(page_table, qi4, w8, qg, bias_s, kidx_new_t,
                       new_page_t(k_s, N_KV_HEADS), new_page_t(v_s, N_KV_HEADS),
                       cache_kidx[l].transpose(0, 2, 1), cache_k[l].transpose(0, 2, 3, 1),
                       cache_v[l].transpose(0, 2, 3, 1), pps=_pick(n_pages, (16, 8)))
    o_s = o_s.reshape(bs, N_KV_HEADS, REP, tpad, HEAD_DIM)[:, :, :, :ts]
    attn_s = o_s.transpose(0, 3, 1, 2, 4).reshape(ms, N_HEADS * HEAD_DIM)
    y_s = _moe_and_final(xs, attn_s, ssd_s, w_out_b, norm2_w[l], w_router, b_router, wg_b, wu_b, wd_b, norm_f_w)

    def cache_out(k_, b_, t_):
        return k_.reshape(1, b_, t_, N_KV_HEADS, HEAD_DIM)

    return (
        y_p.reshape(bp, tp_len, d),
        y_s.reshape(bs, ts, d),
        cache_out(k_p, bp, tp_len), cache_out(v_p, bp, tp_len),
        misc_p[:, :IDX_DIM].reshape(1, bp, tp_len, IDX_DIM),
        xbc_p.reshape(bp, tp_len, CONV_DIM)[:, tp_len - (CONV_W - 1):][None],
        ssm_p[None],
        cache_out(k_s, bs, ts), cache_out(v_s, bs, ts),
        misc_s[:, :IDX_DIM].reshape(1, bs, ts, IDX_DIM),
        xbc_s.reshape(bs, ts, CONV_DIM)[:, ts - (CONV_W - 1):][None],
        ssm_s[None],
    )
```
